```python
import math
import jax, jax.numpy as jnp
from jax import lax
import numpy as np

D_MODEL = 2048
BATCH = 4
SEQ = 4096
DEPTH = 2

D_MIX = D_MODEL
D_A = D_MIX // 4
D_B = 3 * D_MIX // 8
D_C = D_MIX - D_A - D_B
POOL_WINDOWS = (2, 4, 8, 16)
N_POOL_GROUPS = len(POOL_WINDOWS)
POOL_GROUP_DIM = D_A // N_POOL_GROUPS
CHUNK = 128
SGU_HEAD_DIM = 128
SGU_HEADS = D_B // SGU_HEAD_DIM
CONV_WIDTH = 31
N_IN = D_A + 2 * D_B + 2 * D_C
N_GROUPS = 4
E_PER_GROUP = 8
N_EXPERTS = N_GROUPS * E_PER_GROUP
TOP_K = 2
D_EXPERT = D_MODEL // 4
MOE_BLOCK = 128
EPS = 1e-6

kernel_name = "hybrid_pool_sgu_conv_hmoe_block"


def rms_norm(x, g):
    xf = x.astype(jnp.float32)
    y = xf * lax.rsqrt(jnp.mean(xf * xf, axis=-1, keepdims=True) + EPS)
    return (y * g.astype(jnp.float32)).astype(x.dtype)


def layer_norm(x, g, b):
    xf = x.astype(jnp.float32)
    mu = jnp.mean(xf, axis=-1, keepdims=True)
    xc = xf - mu
    y = xc * lax.rsqrt(jnp.mean(xc * xc, axis=-1, keepdims=True) + EPS)
    return (y * g.astype(jnp.float32) + b.astype(jnp.float32)).astype(x.dtype)


def causal_mean(xg, w):
    s = xg.shape[1]
    cs = jnp.cumsum(xg, axis=1)
    prev = jnp.pad(cs[:, : s - w], ((0, 0), (w, 0), (0, 0)))
    cnt = jnp.minimum(jnp.arange(1, s + 1, dtype=jnp.float32), float(w))
    return (cs - prev) / cnt[None, :, None]


def pool_mixer(xa, pool_w, pool_scale):
    b, s, _ = xa.shape
    xf = xa.astype(jnp.float32).reshape(b, s, N_POOL_GROUPS, POOL_GROUP_DIM)
    diffs = []
    for gi, w in enumerate(POOL_WINDOWS):
        xg = xf[:, :, gi]
        diffs.append(causal_mean(xg, w) - xg)
    d = jnp.stack(diffs, axis=2).astype(xa.dtype)
    y = jnp.einsum('bsgc,gcd->bsgd', d, pool_w).reshape(b, s, D_A)
    return y * pool_scale


def sgu_mixer(xb, ln_g, ln_b, w_s, b_s):
    z = jax.nn.gelu(xb)
    u, v = jnp.split(z, 2, axis=-1)
    v = layer_norm(v, ln_g, ln_b)
    b, s, c = v.shape
    nc = s // CHUNK
    vh = v.reshape(b, nc, CHUNK, SGU_HEADS, SGU_HEAD_DIM)
    mask = jnp.tril(jnp.ones((CHUNK, CHUNK), dtype=w_s.dtype))
    gate = jnp.einsum('hts,bnshd->bnthd', w_s * mask, vh) + b_s.T[None, None, :, :, None]
    return u * gate.reshape(b, s, c)


def conv_mixer(xc, dw_w, dw_b, ln_g, ln_b, pw):
    a, g = jnp.split(xc, 2, axis=-1)
    z = a * jax.nn.sigmoid(g)
    k = dw_w[:, None, :]
    z = lax.conv_general_dilated(z, k, window_strides=(1,), padding=[(CONV_WIDTH - 1, 0)],
                                 dimension_numbers=('NWC', 'WIO', 'NWC'),
                                 feature_group_count=D_C) + dw_b
    z = jax.nn.silu(layer_norm(z, ln_g, ln_b))
    return z @ pw


def hier_moe(h, rg_w, rg_b, re_w, re_b, w1, w3, w2):
    b, s, d = h.shape
    t = b * s
    hf = h.reshape(t, d)
    lg = (hf @ rg_w + rg_b).astype(jnp.float32)
    pg = jax.nn.softmax(lg, axis=-1)
    g_idx = jnp.argmax(pg, axis=-1)
    g_w = jnp.take_along_axis(pg, g_idx[:, None], axis=-1)
    le = (hf @ re_w + re_b).astype(jnp.float32).reshape(t, N_GROUPS, E_PER_GROUP)
    le = jnp.take_along_axis(le, g_idx[:, None, None], axis=1)[:, 0]
    pe = jax.nn.softmax(le, axis=-1)
    top_p, top_i = lax.top_k(pe, TOP_K)
    wts = g_w * top_p / jnp.sum(top_p, axis=-1, keepdims=True)
    expert_ids = (g_idx[:, None] * E_PER_GROUP + top_i).reshape(-1).astype(jnp.int32)
    tok_ids = jnp.repeat(jnp.arange(t, dtype=jnp.int32), TOP_K)
    w_flat = wts.reshape(-1).astype(h.dtype)
    n_assign = t * TOP_K
    order = jnp.argsort(expert_ids)
    sorted_e = expert_ids[order]
    counts = jnp.bincount(expert_ids, length=N_EXPERTS)
    padded = ((counts + MOE_BLOCK - 1) // MOE_BLOCK) * MOE_BLOCK
    pend = jnp.cumsum(padded)
    pstart = pend - padded
    start = jnp.cumsum(counts) - counts
    dest = pstart[sorted_e] + (jnp.arange(n_assign, dtype=jnp.int32) - start[sorted_e])
    n_blocks = -(-n_assign // MOE_BLOCK) + N_EXPERTS
    p_len = n_blocks * MOE_BLOCK
    buf_tok = jnp.full((p_len,), t, dtype=jnp.int32).at[dest].set(tok_ids[order])
    buf_w = jnp.zeros((p_len,), dtype=h.dtype).at[dest].set(w_flat[order])
    starts = jnp.arange(n_blocks, dtype=jnp.int32) * MOE_BLOCK
    block_e = jnp.minimum(jnp.sum(pend[None, :] <= starts[:, None], axis=1), N_EXPERTS - 1)
    h_pad = jnp.concatenate([hf, jnp.zeros((1, d), hf.dtype)], axis=0)
    xb = h_pad[buf_tok].reshape(n_blocks, MOE_BLOCK, d)

    def expert_block(args):
        xblk, e = args
        return (jax.nn.silu(xblk @ w1[e]) * (xblk @ w3[e])) @ w2[e]

    yb = lax.map(expert_block, (xb, block_e)).reshape(p_len, d) * buf_w[:, None]
    out = jax.ops.segment_sum(yb, buf_tok, num_segments=t + 1)[:t]
    return out.reshape(b, s, d)


def setup_inputs(seed: int = 0) -> dict:
    key = jax.random.key(seed)
    ks = jax.random.split(key, 24)
    L, D = DEPTH, D_MODEL
    f32 = jnp.float32

    def nrm(k, shape, scale):
        return jax.random.normal(k, shape, f32) * scale

    def gain(k, shape):
        return 1.0 + 0.1 * jax.random.normal(k, shape, f32)

    return {
        "x": jax.random.normal(ks[0], (BATCH, SEQ, D), f32),
        "norm_mix": gain(ks[1], (L, D)),
        "w_in": nrm(ks[2], (L, D, N_IN), D ** -0.5),
        "pool_w": nrm(ks[3], (L, N_POOL_GROUPS, POOL_GROUP_DIM, POOL_GROUP_DIM), POOL_GROUP_DIM ** -0.5),
        "pool_scale": gain(ks[4], (L, D_A)),
        "sgu_ln_g": gain(ks[5], (L, D_B)),
        "sgu_ln_b": nrm(ks[6], (L, D_B), 0.02),
        "sgu_w": nrm(ks[7], (L, SGU_HEADS, CHUNK, CHUNK), CHUNK ** -0.5),
        "sgu_b": gain(ks[8], (L, SGU_HEADS, CHUNK)),
        "conv_dw_w": nrm(ks[9], (L, CONV_WIDTH, D_C), CONV_WIDTH ** -0.5),
        "conv_dw_b": nrm(ks[10], (L, D_C), 0.02),
        "conv_ln_g": gain(ks[11], (L, D_C)),
        "conv_ln_b": nrm(ks[12], (L, D_C), 0.02),
        "conv_pw": nrm(ks[13], (L, D_C, D_C), D_C ** -0.5),
        "w_out": nrm(ks[14], (L, D_MIX, D), D_MIX ** -0.5),
        "norm_ffn": gain(ks[15], (L, D)),
        "router_g_w": nrm(ks[16], (L, D, N_GROUPS), D ** -0.5),
        "router_g_b": nrm(ks[17], (L, N_GROUPS), 0.01),
        "router_e_w": nrm(ks[18], (L, D, N_EXPERTS), D ** -0.5),
        "router_e_b": nrm(ks[19], (L, N_EXPERTS), 0.01),
        "exp_w1": nrm(ks[20], (L, N_EXPERTS, D, D_EXPERT), D ** -0.5),
        "exp_w3": nrm(ks[21], (L, N_EXPERTS, D, D_EXPERT), D ** -0.5),
        "exp_w2": nrm(ks[22], (L, N_EXPERTS, D_EXPERT, D), D_EXPERT ** -0.5),
        "norm_final": gain(ks[23], (D,)),
    }


def reference(x, norm_mix, w_in, pool_w, pool_scale, sgu_ln_g, sgu_ln_b, sgu_w, sgu_b,
              conv_dw_w, conv_dw_b, conv_ln_g, conv_ln_b, conv_pw, w_out, norm_ffn,
              router_g_w, router_g_b, router_e_w, router_e_b, exp_w1, exp_w3, exp_w2,
              norm_final):
    for l in range(DEPTH):
        h = rms_norm(x, norm_mix[l])
        proj = h @ w_in[l]
        xa = proj[..., :D_A]
        xb = proj[..., D_A:D_A + 2 * D_B]
        xc = proj[..., D_A + 2 * D_B:]
        ya = pool_mixer(xa, pool_w[l], pool_scale[l])
        yb = sgu_mixer(xb, sgu_ln_g[l], sgu_ln_b[l], sgu_w[l], sgu_b[l])
        yc = conv_mixer(xc, conv_dw_w[l], conv_dw_b[l], conv_ln_g[l], conv_ln_b[l], conv_pw[l])
        y = jnp.concatenate([ya, yb, yc], axis=-1) @ w_out[l]
        x = x + y
        h2 = rms_norm(x, norm_ffn[l])
        x = x + hier_moe(h2, router_g_w[l], router_g_b[l], router_e_w[l], router_e_b[l],
                         exp_w1[l], exp_w3[l], exp_w2[l])
    return rms_norm(x, norm_final)
```

```python
import functools

import jax
import jax.numpy as jnp
from jax import lax
from jax.experimental import pallas as pl
from jax.experimental.pallas import tpu as pltpu

D_MODEL = 2048
D_A = D_MODEL // 4
D_B = 3 * D_MODEL // 8
D_C = D_MODEL - D_A - D_B
POOL_WINDOWS = (2, 4, 8, 16)
POOL_GROUP_DIM = D_A // len(POOL_WINDOWS)
CHUNK = 128
SGU_HEAD_DIM = 128
SGU_HEADS = D_B // SGU_HEAD_DIM
CONV_WIDTH = 31
N_IN = D_A + 2 * D_B + 2 * D_C
N_GROUPS = 4
E_PER_GROUP = 8
N_EXPERTS = N_GROUPS * E_PER_GROUP
TOP_K = 2
D_EXPERT = D_MODEL // 4
EPS = 1e-6

LANES = 128
POOL_HALO = 16
CONV_HALO = 32
CONV_ROWS = 32
TM_MIX = 256
TM_OUT = 256
BM = 256
ROUTER_COLS = LANES
VMEM_LIMIT = 56 * 1024 * 1024

_F32 = jnp.float32
_BF16 = jnp.bfloat16


def _resident(shape):
    nd = len(shape)
    return pl.BlockSpec(shape, lambda *_: (0,) * nd, pipeline_mode=pl.Buffered(1))


def _rms_scale(x):
    return lax.rsqrt(jnp.mean(x * x, axis=-1, keepdims=True) + EPS)


def _layer_norm(x, g, b):
    mu = jnp.mean(x, axis=-1, keepdims=True)
    xc = x - mu
    return xc * lax.rsqrt(jnp.mean(xc * xc, axis=-1, keepdims=True) + EPS) * g + b


def _gelu_tanh(x):
    return 0.5 * x * (1.0 + jnp.tanh(0.7978845608028654 * (x + 0.044715 * (x * x * x))))


def _sigmoid(x):
    return 1.0 / (1.0 + jnp.exp(-x))


def _mixer_kernel(combine, tiles_per_seq, *refs):
    if combine:
        x_ref, y0_ref, y1_ref = refs[:3]
        refs = refs[3:]
    else:
        x_ref = refs[0]
        refs = refs[1:]
    (g_ref, win_ref, poolw_ref, pscale_ref, sg_ref, sb_ref, sw_ref, sbias_ref,
     dww_ref, dwb_ref, cg_ref, cb_ref, pw_ref) = refs[:13]
    refs = refs[13:]
    if combine:
        ycat_ref, xo_ref = refs[:2]
        refs = refs[2:]
    else:
        ycat_ref = refs[0]
        refs = refs[1:]
    h_s, xa_s, z_s, cv_s = refs
    tm = ycat_ref.shape[0]

    tile = pl.program_id(0) % tiles_per_seq
    first = tile == 0

    if combine:
        x = x_ref[...] + y0_ref[...] + y1_ref[...]
        xo_ref[...] = x
    else:
        x = x_ref[...]
    h_s[...] = (x * _rms_scale(x) * g_ref[...]).astype(_BF16)

    @pl.when(first)
    def _():
        xa_s[0:POOL_HALO, :] = jnp.zeros((POOL_HALO, D_A), _F32)

    @pl.when(jnp.logical_not(first))
    def _():
        xa_s[0:POOL_HALO, :] = xa_s[tm:tm + POOL_HALO, :]

    xa_s[POOL_HALO:POOL_HALO + tm, :] = jnp.dot(h_s[...], win_ref[:, 0:D_A], preferred_element_type=_F32)
    pos = tile * tm + lax.broadcasted_iota(jnp.int32, (tm, 1), 0)
    for gi, w in enumerate(POOL_WINDOWS):
        cols = slice(gi * POOL_GROUP_DIM, (gi + 1) * POOL_GROUP_DIM)
        xg = xa_s[POOL_HALO:POOL_HALO + tm, cols]
        acc = xg
        for k in range(1, w):
            acc = acc + xa_s[POOL_HALO - k:POOL_HALO - k + tm, cols]
        cnt = jnp.minimum(pos + 1, w).astype(_F32)
        d = acc / cnt - xg
        yg = jnp.dot(d.astype(_BF16), poolw_ref[gi], preferred_element_type=_F32) * pscale_ref[:, cols]
        ycat_ref[:, cols] = yg.astype(_BF16)

    pb = jnp.dot(h_s[...], win_ref[:, D_A:D_A + 2 * D_B], preferred_element_type=_F32)
    zb = _gelu_tanh(pb)
    u = zb[:, :D_B]
    vb = _layer_norm(zb[:, D_B:], sg_ref[...], sb_ref[...]).astype(_BF16)
    r_i = lax.broadcasted_iota(jnp.int32, (CHUNK, CHUNK), 0)
    c_i = lax.broadcasted_iota(jnp.int32, (CHUNK, CHUNK), 1)
    causal = r_i >= c_i
    for hd in range(SGU_HEADS):
        hc = slice(hd * SGU_HEAD_DIM, (hd + 1) * SGU_HEAD_DIM)
        wm = jnp.where(causal, sw_ref[hd], 0.0).astype(_BF16)
        for c in range(tm // CHUNK):
            rows = slice(c * CHUNK, (c + 1) * CHUNK)
            gate = jnp.dot(wm, vb[rows, hc], preferred_element_type=_F32) + sbias_ref[:, hc]
            ycat_ref[rows, D_A + hd * SGU_HEAD_DIM:D_A + (hd + 1) * SGU_HEAD_DIM] = (u[rows, hc] * gate).astype(_BF16)

    pc = jnp.dot(h_s[...], win_ref[:, D_A + 2 * D_B:N_IN], preferred_element_type=_F32)

    @pl.when(first)
    def _():
        z_s[0:CONV_HALO, :] = jnp.zeros((CONV_HALO, D_C), _F32)

    @pl.when(jnp.logical_not(first))
    def _():
        z_s[0:CONV_HALO, :] = z_s[tm:tm + CONV_HALO, :]

    z_s[CONV_HALO:CONV_HALO + tm, :] = pc[:, :D_C] * _sigmoid(pc[:, D_C:])

    for r0 in range(0, tm, CONV_ROWS):
        acc = jnp.zeros((CONV_ROWS, D_C), _F32)
        for k in range(CONV_WIDTH):
            off = r0 + CONV_HALO - (CONV_WIDTH - 1) + k
            acc = acc + dww_ref[k:k + 1, :] * z_s[off:off + CONV_ROWS, :]
        cv_s[r0:r0 + CONV_ROWS, :] = acc + dwb_ref[...]
    zn = _layer_norm(cv_s[...], cg_ref[...], cb_ref[...])
    zs = (zn * _sigmoid(zn)).astype(_BF16)
    ycat_ref[:, D_A + D_B:D_MODEL] = jnp.dot(zs, pw_ref[...], preferred_element_type=_F32).astype(_BF16)


def _mixer_call(combine, t, seq, x, y, g, win, poolw, pscale, sg, sb, sw, sbias, dww, dwb, cg, cb, pw):
    tm = TM_MIX
    n_tiles = t // tm
    row = lambda i: (i, 0)
    act = pl.BlockSpec((tm, D_MODEL), row)
    in_specs = [act]
    args = [x]
    if combine:
        in_specs += [pl.BlockSpec((tm, D_MODEL), row), pl.BlockSpec((tm, D_MODEL), lambda i: (i + n_tiles, 0))]
        args += [y, y]
    weights = [g, win, poolw, pscale, sg, sb, sw, sbias, dww, dwb, cg, cb, pw]
    in_specs += [_resident(w.shape) for w in weights]
    args += weights
    out_shape = [jax.ShapeDtypeStruct((t, D_MODEL), _BF16)]
    out_specs = [pl.BlockSpec((tm, D_MODEL), row)]
    if combine:
        out_shape.append(jax.ShapeDtypeStruct((t, D_MODEL), _F32))
        out_specs.append(pl.BlockSpec((tm, D_MODEL), row))
    return pl.pallas_call(
        functools.partial(_mixer_kernel, combine, seq // tm),
        grid=(n_tiles,),
        in_specs=in_specs,
        out_specs=out_specs,
        out_shape=out_shape,
        scratch_shapes=[
            pltpu.VMEM((tm, D_MODEL), _BF16),
            pltpu.VMEM((POOL_HALO + tm, D_A), _F32),
            pltpu.VMEM((CONV_HALO + tm, D_C), _F32),
            pltpu.VMEM((tm, D_C), _F32),
        ],
        compiler_params=pltpu.CompilerParams(dimension_semantics=("arbitrary",), vmem_limit_bytes=VMEM_LIMIT),
        name="mixer_combine" if combine else "mixer",
    )(*args)


def _pack_bf16_pairs(hb):
    u = lax.bitcast_convert_type(hb.astype(_F32), jnp.uint32)
    c = hb.shape[1] // 2
    return (u[:, c:] & jnp.uint32(0xFFFF0000)) | (u[:, :c] >> 16)


def _unpack_bf16_pairs(p):
    lo = lax.bitcast_convert_type(p << 16, _F32).astype(_BF16)
    hi = lax.bitcast_convert_type(p & jnp.uint32(0xFFFF0000), _F32).astype(_BF16)
    return lo, hi


def _out_kernel(ycat_ref, x_ref, wout_ref, g_ref, wr_ref, br_ref, x1_ref, hp_ref, lg_ref):
    x1 = x_ref[...] + jnp.dot(ycat_ref[...], wout_ref[...], preferred_element_type=_F32)
    x1_ref[...] = x1
    hb = (x1 * _rms_scale(x1) * g_ref[...]).astype(_BF16)
    lg_ref[...] = jnp.dot(hb, wr_ref[...], preferred_element_type=_F32) + br_ref[...]
    hp_ref[...] = _pack_bf16_pairs(hb)


def _out_call(t, ycat, x, wout, g, wr, br):
    tm = TM_OUT
    row = lambda i: (i, 0)
    return pl.pallas_call(
        _out_kernel,
        grid=(t // tm,),
        in_specs=[pl.BlockSpec((tm, D_MODEL), row), pl.BlockSpec((tm, D_MODEL), row),
                  _resident(wout.shape), _resident(g.shape), _resident(wr.shape), _resident(br.shape)],
        out_specs=[pl.BlockSpec((tm, D_MODEL), row), pl.BlockSpec((tm, D_MODEL // 2), row),
                   pl.BlockSpec((tm, ROUTER_COLS), row)],
        out_shape=[jax.ShapeDtypeStruct((t, D_MODEL), _F32), jax.ShapeDtypeStruct((t, D_MODEL // 2), jnp.uint32),
                   jax.ShapeDtypeStruct((t, ROUTER_COLS), _F32)],
        compiler_params=pltpu.CompilerParams(dimension_semantics=("arbitrary",), vmem_limit_bytes=VMEM_LIMIT),
        name="out_proj",
    )(ycat, x, wout, g, wr, br)


def _wait_rows(src, dst, sem, n):
    n8 = pl.multiple_of(lax.shift_left(lax.shift_right_logical(n, 3), 3), 8)

    @pl.when(n8 > 0)
    def _():
        pltpu.make_async_copy(src.at[pl.ds(0, n8)], dst.at[pl.ds(0, n8)], sem).wait()

    def one(r, carry):
        pltpu.make_async_copy(src.at[pl.ds(0, 1)], dst.at[pl.ds(0, 1)], sem).wait()
        return carry

    lax.fori_loop(n8, n, one, 0)


def _moe_kernel(be_ref, cnt_ref, tok_ref, dst_ref, wt_ref, hp_hbm, w1_ref, w3_ref, w2_ref, y_hbm,
                xb_s, yb_s, gsem, ssem):
    del be_ref
    half = D_MODEL // 2
    step = pl.program_id(0)
    n = cnt_ref[step]

    @pl.when(step == 0)
    def _():
        xb_s[...] = jnp.zeros(xb_s.shape, xb_s.dtype)

    @pl.when(n > 0)
    def _():
        def gather(r, carry):
            pltpu.make_async_copy(hp_hbm.at[pl.ds(tok_ref[0, 0, r], 1)], xb_s.at[pl.ds(r, 1)], gsem).start()
            return carry

        lax.fori_loop(0, n, gather, 0)
        _wait_rows(hp_hbm, xb_s, gsem, n)

        lo, hi = _unpack_bf16_pairs(xb_s[...])
        a1 = (jnp.dot(lo, w1_ref[0, 0:half, :], preferred_element_type=_F32)
              + jnp.dot(hi, w1_ref[0, half:D_MODEL, :], preferred_element_type=_F32))
        a3 = (jnp.dot(lo, w3_ref[0, 0:half, :], preferred_element_type=_F32)
              + jnp.dot(hi, w3_ref[0, half:D_MODEL, :], preferred_element_type=_F32))
        hid = (a1 * _sigmoid(a1) * a3).astype(_BF16)
        yb_s[...] = jnp.dot(hid, w2_ref[0], preferred_element_type=_F32) * wt_ref[0]

        def scatter(r, carry):
            pltpu.make_async_copy(yb_s.at[pl.ds(r, 1)], y_hbm.at[pl.ds(dst_ref[0, 0, r], 1)], ssem).start()
            return carry

        lax.fori_loop(0, n, scatter, 0)
        _wait_rows(yb_s, y_hbm, ssem, n)


def _moe_call(t, n_blocks, block_e, block_cnt, tok, dst, wts, hp, w1, w3, w2):
    smem_rows = pl.BlockSpec((1, 1, BM), lambda i, be, nv: (i, 0, 0), memory_space=pltpu.SMEM)
    grid_spec = pltpu.PrefetchScalarGridSpec(
        num_scalar_prefetch=2,
        grid=(n_blocks,),
        in_specs=[
            smem_rows,
            smem_rows,
            pl.BlockSpec((1, BM, 1), lambda i, be, nv: (i, 0, 0)),
            pl.BlockSpec(memory_space=pl.ANY),
            pl.BlockSpec((1, D_MODEL, D_EXPERT), lambda i, be, nv: (be[i], 0, 0)),
            pl.BlockSpec((1, D_MODEL, D_EXPERT), lambda i, be, nv: (be[i], 0, 0)),
            pl.BlockSpec((1, D_EXPERT, D_MODEL), lambda i, be, nv: (be[i], 0, 0)),
        ],
        out_specs=pl.BlockSpec(memory_space=pl.ANY),
        scratch_shapes=[
            pltpu.VMEM((BM, D_MODEL // 2), jnp.uint32),
            pltpu.VMEM((BM, D_MODEL), _F32),
            pltpu.SemaphoreType.DMA(()),
            pltpu.SemaphoreType.DMA(()),
        ],
    )
    return pl.pallas_call(
        _moe_kernel,
        grid_spec=grid_spec,
        out_shape=jax.ShapeDtypeStruct((2 * t, D_MODEL), _F32),
        compiler_params=pltpu.CompilerParams(dimension_semantics=("arbitrary",), vmem_limit_bytes=VMEM_LIMIT),
        name="moe",
    )(block_e, block_cnt, tok, dst, wts, hp, w1, w3, w2)


def _route(t, logits):
    lg = logits[:, :N_GROUPS]
    pg = jax.nn.softmax(lg, axis=-1)
    g_idx = jnp.argmax(pg, axis=-1)
    g_w = jnp.take_along_axis(pg, g_idx[:, None], axis=-1)
    le = logits[:, N_GROUPS:N_GROUPS + N_EXPERTS].reshape(t, N_GROUPS, E_PER_GROUP)
    le = jnp.take_along_axis(le, g_idx[:, None, None], axis=1)[:, 0]
    pe = jax.nn.softmax(le, axis=-1)
    top_p, top_i = lax.top_k(pe, TOP_K)
    wts = g_w * top_p / jnp.sum(top_p, axis=-1, keepdims=True)
    expert_ids = (g_idx[:, None] * E_PER_GROUP + top_i).reshape(-1).astype(jnp.int32)
    w_flat = wts.reshape(-1)

    n_assign = t * TOP_K
    n_blocks = n_assign // BM + N_EXPERTS
    p_len = n_blocks * BM
    onehot = (expert_ids[:, None] == jnp.arange(N_EXPERTS, dtype=jnp.int32)[None, :]).astype(jnp.int32)
    csum = jnp.cumsum(onehot, axis=0)
    rank = jnp.sum(onehot * (csum - 1), axis=1)
    counts = csum[-1]
    padded = ((counts + BM - 1) // BM) * BM
    pend = jnp.cumsum(padded)
    pstart = pend - padded
    dest = pstart[expert_ids] + rank
    a_idx = jnp.arange(n_assign, dtype=jnp.int32)
    tok_of = a_idx // TOP_K
    row_of = (a_idx % TOP_K) * t + tok_of
    buf_tok = jnp.zeros((p_len,), jnp.int32).at[dest].set(tok_of)
    buf_dst = jnp.zeros((p_len,), jnp.int32).at[dest].set(row_of)
    buf_w = jnp.zeros((p_len,), _F32).at[dest].set(w_flat)
    starts = jnp.arange(n_blocks, dtype=jnp.int32) * BM
    block_e = jnp.minimum(jnp.sum(pend[None, :] <= starts[:, None], axis=1), N_EXPERTS - 1).astype(jnp.int32)
    seg_end = (pstart + counts)[block_e]
    block_cnt = jnp.where(starts < pend[-1], jnp.clip(seg_end - starts, 0, BM), 0).astype(jnp.int32)
    return (n_blocks, block_e, block_cnt, buf_tok.reshape(n_blocks, 1, BM), buf_dst.reshape(n_blocks, 1, BM),
            buf_w.reshape(n_blocks, BM, 1))


def _final_kernel(x_ref, y0_ref, y1_ref, g_ref, o_ref):
    x = x_ref[...] + y0_ref[...] + y1_ref[...]
    o_ref[...] = x * _rms_scale(x) * g_ref[...]


def _final_call(t, x, y, g):
    tm = TM_OUT
    n_tiles = t // tm
    row = lambda i: (i, 0)
    return pl.pallas_call(
        _final_kernel,
        grid=(n_tiles,),
        in_specs=[pl.BlockSpec((tm, D_MODEL), row), pl.BlockSpec((tm, D_MODEL), row),
                  pl.BlockSpec((tm, D_MODEL), lambda i: (i + n_tiles, 0)), _resident(g.shape)],
        out_specs=pl.BlockSpec((tm, D_MODEL), row),
        out_shape=jax.ShapeDtypeStruct((t, D_MODEL), _F32),
        compiler_params=pltpu.CompilerParams(dimension_semantics=("arbitrary",), vmem_limit_bytes=VMEM_LIMIT),
        name="final_norm",
    )(x, y, y, g)


def kernel(x, norm_mix, w_in, pool_w, pool_scale, sgu_ln_g, sgu_ln_b, sgu_w, sgu_b, conv_dw_w, conv_dw_b,
           conv_ln_g, conv_ln_b, conv_pw, w_out, norm_ffn, router_g_w, router_g_b, router_e_w, router_e_b,
           exp_w1, exp_w3, exp_w2, norm_final):
    b, s, d = x.shape
    depth = w_in.shape[0]
    assert d == D_MODEL and s % TM_MIX == 0 and (b * s) % TM_OUT == 0 and TM_MIX % CHUNK == 0
    t = b * s
    xf = x.reshape(t, d)
    y = None
    for l in range(depth):
        row2 = lambda v: v[l].reshape(1, -1)
        sbias = jnp.repeat(sgu_b[l].T, SGU_HEAD_DIM, axis=1)
        outs = _mixer_call(
            y is not None, t, s, xf, y, row2(norm_mix), w_in[l].astype(_BF16), pool_w[l].astype(_BF16),
            row2(pool_scale), row2(sgu_ln_g), row2(sgu_ln_b), sgu_w[l], sbias, conv_dw_w[l], row2(conv_dw_b),
            row2(conv_ln_g), row2(conv_ln_b), conv_pw[l].astype(_BF16))
        if y is not None:
            ycat, xf = outs
        else:
            (ycat,) = outs
        pad = ROUTER_COLS - N_GROUPS - N_EXPERTS
        wr = jnp.concatenate([router_g_w[l], router_e_w[l], jnp.zeros((d, pad), _F32)], axis=1).astype(_BF16)
        br = jnp.concatenate([router_g_b[l], router_e_b[l], jnp.zeros((pad,), _F32)]).reshape(1, ROUTER_COLS)
        xf, hp, logits = _out_call(t, ycat, xf, w_out[l].astype(_BF16), row2(norm_ffn), wr, br)
        n_blocks, block_e, block_cnt, tok, dst, wts = _route(t, logits)
        y = _moe_call(t, n_blocks, block_e, block_cnt, tok, dst, wts, hp,
                      exp_w1[l].astype(_BF16), exp_w3[l].astype(_BF16), exp_w2[l].astype(_BF16))
    out = _final_call(t, xf, y, norm_final.reshape(1, -1))
    return out.reshape(b, s, d)
```

```python
import functools

import jax
import jax.numpy as jnp
from jax import lax
from jax.experimental import pallas as pl
from jax.experimental.pallas import tpu as pltpu

D_MODEL = 2048
D_A = D_MODEL // 4
D_B = 3 * D_MODEL // 8
D_C = D_MODEL - D_A - D_B
POOL_WINDOWS = (2, 4, 8, 16)
POOL_GROUP_DIM = D_A // len(POOL_WINDOWS)
CHUNK = 128
SGU_HEAD_DIM = 128
SGU_HEADS = D_B // SGU_HEAD_DIM
CONV_WIDTH = 31
N_IN = D_A + 2 * D_B + 2 * D_C
N_GROUPS = 4
E_PER_GROUP = 8
N_EXPERTS = N_GROUPS * E_PER_GROUP
TOP_K = 2
D_EXPERT = D_MODEL // 4
EPS = 1e-6

LANES = 128
POOL_HALO = 16
CONV_HALO = 32
CONV_ROWS = 32
TM_MIX = 256
TM_OUT = 256
BM = 256
TOK_TILE_ROWS = D_MODEL // 2 // LANES
ISSUE_UNROLL_LOG2 = 3
ISSUE_UNROLL = 1 << ISSUE_UNROLL_LOG2
ROUTER_COLS = LANES
VMEM_LIMIT = 56 * 1024 * 1024

_F32 = jnp.float32
_BF16 = jnp.bfloat16


def _resident(shape):
    nd = len(shape)
    return pl.BlockSpec(shape, lambda *_: (0,) * nd, pipeline_mode=pl.Buffered(1))


def _rms_scale(x):
    return lax.rsqrt(jnp.mean(x * x, axis=-1, keepdims=True) + EPS)


def _layer_norm(x, g, b):
    mu = jnp.mean(x, axis=-1, keepdims=True)
    xc = x - mu
    return xc * lax.rsqrt(jnp.mean(xc * xc, axis=-1, keepdims=True) + EPS) * g + b


def _gelu_tanh(x):
    return 0.5 * x * (1.0 + jnp.tanh(0.7978845608028654 * (x + 0.044715 * (x * x * x))))


def _sigmoid(x):
    return 1.0 / (1.0 + jnp.exp(-x))


def _mixer_kernel(combine, tiles_per_seq, *refs):
    if combine:
        x_ref, y0_ref, y1_ref = refs[:3]
        refs = refs[3:]
    else:
        x_ref = refs[0]
        refs = refs[1:]
    (g_ref, win_ref, poolw_ref, pscale_ref, sg_ref, sb_ref, sw_ref, sbias_ref,
     dww_ref, dwb_ref, cg_ref, cb_ref, pw_ref) = refs[:13]
    refs = refs[13:]
    if combine:
        ycat_ref, xo_ref = refs[:2]
        refs = refs[2:]
    else:
        ycat_ref = refs[0]
        refs = refs[1:]
    h_s, xa_s, z_s, cv_s = refs
    tm = ycat_ref.shape[0]

    tile = pl.program_id(0) % tiles_per_seq
    first = tile == 0

    if combine:
        x = x_ref[...] + y0_ref[...] + y1_ref[...]
        xo_ref[...] = x
    else:
        x = x_ref[...]
    h_s[...] = (x * _rms_scale(x) * g_ref[...]).astype(_BF16)

    @pl.when(first)
    def _():
        xa_s[0:POOL_HALO, :] = jnp.zeros((POOL_HALO, D_A), _F32)

    @pl.when(jnp.logical_not(first))
    def _():
        xa_s[0:POOL_HALO, :] = xa_s[tm:tm + POOL_HALO, :]

    xa_s[POOL_HALO:POOL_HALO + tm, :] = jnp.dot(h_s[...], win_ref[:, 0:D_A], preferred_element_type=_F32)
    pos = tile * tm + lax.broadcasted_iota(jnp.int32, (tm, 1), 0)
    for gi, w in enumerate(POOL_WINDOWS):
        cols = slice(gi * POOL_GROUP_DIM, (gi + 1) * POOL_GROUP_DIM)
        xg = xa_s[POOL_HALO:POOL_HALO + tm, cols]
        acc = xg
        for k in range(1, w):
            acc = acc + xa_s[POOL_HALO - k:POOL_HALO - k + tm, cols]
        cnt = jnp.minimum(pos + 1, w).astype(_F32)
        d = acc / cnt - xg
        yg = jnp.dot(d.astype(_BF16), poolw_ref[gi], preferred_element_type=_F32) * pscale_ref[:, cols]
        ycat_ref[:, cols] = yg.astype(_BF16)

    pb = jnp.dot(h_s[...], win_ref[:, D_A:D_A + 2 * D_B], preferred_element_type=_F32)
    zb = _gelu_tanh(pb)
    u = zb[:, :D_B]
    vb = _layer_norm(zb[:, D_B:], sg_ref[...], sb_ref[...]).astype(_BF16)
    r_i = lax.broadcasted_iota(jnp.int32, (CHUNK, CHUNK), 0)
    c_i = lax.broadcasted_iota(jnp.int32, (CHUNK, CHUNK), 1)
    causal = r_i >= c_i
    for hd in range(SGU_HEADS):
        hc = slice(hd * SGU_HEAD_DIM, (hd + 1) * SGU_HEAD_DIM)
        wm = jnp.where(causal, sw_ref[hd], 0.0).astype(_BF16)
        for c in range(tm // CHUNK):
            rows = slice(c * CHUNK, (c + 1) * CHUNK)
            gate = jnp.dot(wm, vb[rows, hc], preferred_element_type=_F32) + sbias_ref[:, hc]
            ycat_ref[rows, D_A + hd * SGU_HEAD_DIM:D_A + (hd + 1) * SGU_HEAD_DIM] = (u[rows, hc] * gate).astype(_BF16)

    pc = jnp.dot(h_s[...], win_ref[:, D_A + 2 * D_B:N_IN], preferred_element_type=_F32)

    @pl.when(first)
    def _():
        z_s[0:CONV_HALO, :] = jnp.zeros((CONV_HALO, D_C), _F32)

    @pl.when(jnp.logical_not(first))
    def _():
        z_s[0:CONV_HALO, :] = z_s[tm:tm + CONV_HALO, :]

    z_s[CONV_HALO:CONV_HALO + tm, :] = pc[:, :D_C] * _sigmoid(pc[:, D_C:])

    for r0 in range(0, tm, CONV_ROWS):
        acc = jnp.zeros((CONV_ROWS, D_C), _F32)
        for k in range(CONV_WIDTH):
            off = r0 + CONV_HALO - (CONV_WIDTH - 1) + k
            acc = acc + dww_ref[k:k + 1, :] * z_s[off:off + CONV_ROWS, :]
        cv_s[r0:r0 + CONV_ROWS, :] = acc + dwb_ref[...]
    zn = _layer_norm(cv_s[...], cg_ref[...], cb_ref[...])
    zs = (zn * _sigmoid(zn)).astype(_BF16)
    ycat_ref[:, D_A + D_B:D_MODEL] = jnp.dot(zs, pw_ref[...], preferred_element_type=_F32).astype(_BF16)


def _mixer_call(combine, t, seq, x, y, g, win, poolw, pscale, sg, sb, sw, sbias, dww, dwb, cg, cb, pw):
    tm = TM_MIX
    n_tiles = t // tm
    row = lambda i: (i, 0)
    act = pl.BlockSpec((tm, D_MODEL), row)
    in_specs = [act]
    args = [x]
    if combine:
        in_specs += [pl.BlockSpec((tm, D_MODEL), row), pl.BlockSpec((tm, D_MODEL), lambda i: (i + n_tiles, 0))]
        args += [y, y]
    weights = [g, win, poolw, pscale, sg, sb, sw, sbias, dww, dwb, cg, cb, pw]
    in_specs += [_resident(w.shape) for w in weights]
    args += weights
    out_shape = [jax.ShapeDtypeStruct((t, D_MODEL), _BF16)]
    out_specs = [pl.BlockSpec((tm, D_MODEL), row)]
    if combine:
        out_shape.append(jax.ShapeDtypeStruct((t, D_MODEL), _F32))
        out_specs.append(pl.BlockSpec((tm, D_MODEL), row))
    return pl.pallas_call(
        functools.partial(_mixer_kernel, combine, seq // tm),
        grid=(n_tiles,),
        in_specs=in_specs,
        out_specs=out_specs,
        out_shape=out_shape,
        scratch_shapes=[
            pltpu.VMEM((tm, D_MODEL), _BF16),
            pltpu.VMEM((POOL_HALO + tm, D_A), _F32),
            pltpu.VMEM((CONV_HALO + tm, D_C), _F32),
            pltpu.VMEM((tm, D_C), _F32),
        ],
        compiler_params=pltpu.CompilerParams(dimension_semantics=("arbitrary",), vmem_limit_bytes=VMEM_LIMIT),
        name="mixer_combine" if combine else "mixer",
    )(*args)


def _pack_bf16_pairs(hb):
    u = lax.bitcast_convert_type(hb.astype(_F32), jnp.uint32)
    c = hb.shape[1] // 2
    return (u[:, c:] & jnp.uint32(0xFFFF0000)) | (u[:, :c] >> 16)


def _unpack_bf16_pairs(p):
    lo = lax.bitcast_convert_type(p << 16, _F32).astype(_BF16)
    hi = lax.bitcast_convert_type(p & jnp.uint32(0xFFFF0000), _F32).astype(_BF16)
    return lo, hi


def _out_kernel(ycat_ref, x_ref, wout_ref, g_ref, wr_ref, br_ref, x1_ref, hp_ref, lg_ref):
    x1 = x_ref[...] + jnp.dot(ycat_ref[...], wout_ref[...], preferred_element_type=_F32)
    x1_ref[...] = x1
    hb = (x1 * _rms_scale(x1) * g_ref[...]).astype(_BF16)
    lg_ref[...] = jnp.dot(hb, wr_ref[...], preferred_element_type=_F32) + br_ref[...]
    packed = _pack_bf16_pairs(hb)
    tm = packed.shape[0]
    for s in range(TOK_TILE_ROWS):
        hp_ref[pl.ds(s, tm, stride=TOK_TILE_ROWS), :] = packed[:, s * LANES:(s + 1) * LANES]


def _out_call(t, ycat, x, wout, g, wr, br):
    tm = TM_OUT
    row = lambda i: (i, 0)
    return pl.pallas_call(
        _out_kernel,
        grid=(t // tm,),
        in_specs=[pl.BlockSpec((tm, D_MODEL), row), pl.BlockSpec((tm, D_MODEL), row),
                  _resident(wout.shape), _resident(g.shape), _resident(wr.shape), _resident(br.shape)],
        out_specs=[pl.BlockSpec((tm, D_MODEL), row), pl.BlockSpec((tm * TOK_TILE_ROWS, LANES), row),
                   pl.BlockSpec((tm, ROUTER_COLS), row)],
        out_shape=[jax.ShapeDtypeStruct((t, D_MODEL), _F32),
                   jax.ShapeDtypeStruct((t * TOK_TILE_ROWS, LANES), jnp.uint32),
                   jax.ShapeDtypeStruct((t, ROUTER_COLS), _F32)],
        compiler_params=pltpu.CompilerParams(dimension_semantics=("arbitrary",), vmem_limit_bytes=VMEM_LIMIT),
        name="out_proj",
    )(ycat, x, wout, g, wr, br)


def _wait_rows(src, dst, sem, n):
    n8 = pl.multiple_of(lax.shift_left(lax.shift_right_logical(n, 3), 3), 8)

    @pl.when(n8 > 0)
    def _():
        pltpu.make_async_copy(src.at[pl.ds(0, n8)], dst.at[pl.ds(0, n8)], sem).wait()

    def one(r, carry):
        pltpu.make_async_copy(src.at[pl.ds(0, 1)], dst.at[pl.ds(0, 1)], sem).wait()
        return carry

    lax.fori_loop(n8, n, one, 0)


def _start_rows(n, start_one, static_when_full=True):
    def looped():
        groups = lax.shift_right_logical(n, ISSUE_UNROLL_LOG2)

        def group(gi, carry):
            r0 = gi * ISSUE_UNROLL
            for j in range(ISSUE_UNROLL):
                start_one(r0 + j)
            return carry

        lax.fori_loop(0, groups, group, 0)

        def one(r, carry):
            start_one(r)
            return carry

        lax.fori_loop(groups * ISSUE_UNROLL, n, one, 0)

    if not static_when_full:
        looped()
        return

    @pl.when(n == BM)
    def _():
        for r in range(BM):
            start_one(r)

    @pl.when(n < BM)
    def _():
        looped()


def _wait_tokens(xb, sem, n):
    @pl.when(n > 0)
    def _():
        rows = pl.multiple_of(n * TOK_TILE_ROWS, TOK_TILE_ROWS)
        view = xb.at[pl.ds(0, rows)]
        pltpu.make_async_copy(view, view, sem).wait()


def _moe_kernel(be_ref, cnt_ref, tok_ref, tok_next_ref, dst_ref, wt_ref, hp_hbm, w1_ref, w3_ref, w2_ref, y_hbm,
                xb_s, yb_s, gsem, ssem):
    del be_ref
    half = D_MODEL // 2
    step = pl.program_id(0)
    last = pl.num_programs(0) - 1
    slot = lax.rem(step, 2)
    n = cnt_ref[step]

    def start_gather(table_ref, rows, to_slot, static_when_full):
        def one(r):
            r0 = r * TOK_TILE_ROWS if isinstance(r, int) else pl.multiple_of(r * TOK_TILE_ROWS, TOK_TILE_ROWS)
            pltpu.make_async_copy(hp_hbm.at[table_ref[0, 0, r]], xb_s.at[to_slot, pl.ds(r0, TOK_TILE_ROWS)],
                                  gsem.at[to_slot]).start()

        _start_rows(rows, one, static_when_full)

    @pl.when(step == 0)
    def _():
        xb_s[...] = jnp.zeros(xb_s.shape, xb_s.dtype)
        start_gather(tok_ref, n, 0, False)

    @pl.when(step < last)
    def _():
        start_gather(tok_next_ref, cnt_ref[step + 1], 1 - slot, True)

    _wait_tokens(xb_s.at[slot], gsem.at[slot], n)

    @pl.when(step >= 2)
    def _():
        _wait_rows(yb_s.at[slot], y_hbm, ssem.at[slot], cnt_ref[step - 2])

    @pl.when(n > 0)
    def _():
        xb = xb_s.at[slot]
        packed = jnp.concatenate([xb[pl.ds(s, BM, stride=TOK_TILE_ROWS), :] for s in range(TOK_TILE_ROWS)], axis=1)
        lo, hi = _unpack_bf16_pairs(packed)
        a1 = (jnp.dot(lo, w1_ref[0, 0:half, :].astype(_BF16), preferred_element_type=_F32)
              + jnp.dot(hi, w1_ref[0, half:D_MODEL, :].astype(_BF16), preferred_element_type=_F32))
        a3 = (jnp.dot(lo, w3_ref[0, 0:half, :].astype(_BF16), preferred_element_type=_F32)
              + jnp.dot(hi, w3_ref[0, half:D_MODEL, :].astype(_BF16), preferred_element_type=_F32))
        hid = (a1 * _sigmoid(a1) * a3).astype(_BF16)
        yb_s[slot] = jnp.dot(hid, w2_ref[0].astype(_BF16), preferred_element_type=_F32) * wt_ref[0]

        def one(r):
            pltpu.make_async_copy(yb_s.at[slot, pl.ds(r, 1)], y_hbm.at[pl.ds(dst_ref[0, 0, r], 1)],
                                  ssem.at[slot]).start()

        _start_rows(n, one)

    @pl.when(step == last)
    def _():
        _wait_rows(yb_s.at[slot], y_hbm, ssem.at[slot], n)

        @pl.when(step >= 1)
        def _():
            _wait_rows(yb_s.at[1 - slot], y_hbm, ssem.at[1 - slot], cnt_ref[step - 1])


def _moe_call(t, layer, n_blocks, block_e, block_cnt, tok, dst, wts, hp, w1, w3, w2):
    smem_rows = pl.BlockSpec((1, 1, BM), lambda i, be, nv: (i, 0, 0), memory_space=pltpu.SMEM)
    smem_rows_next = pl.BlockSpec((1, 1, BM), lambda i, be, nv: (jnp.minimum(i + 1, n_blocks - 1), 0, 0),
                                  memory_space=pltpu.SMEM)
    grid_spec = pltpu.PrefetchScalarGridSpec(
        num_scalar_prefetch=2,
        grid=(n_blocks,),
        in_specs=[
            smem_rows,
            smem_rows_next,
            smem_rows,
            pl.BlockSpec((1, BM, 1), lambda i, be, nv: (i, 0, 0)),
            pl.BlockSpec(memory_space=pl.ANY),
            pl.BlockSpec((None, 1, D_MODEL, D_EXPERT), lambda i, be, nv: (layer, be[i], 0, 0)),
            pl.BlockSpec((None, 1, D_MODEL, D_EXPERT), lambda i, be, nv: (layer, be[i], 0, 0)),
            pl.BlockSpec((None, 1, D_EXPERT, D_MODEL), lambda i, be, nv: (layer, be[i], 0, 0)),
        ],
        out_specs=pl.BlockSpec(memory_space=pl.ANY),
        scratch_shapes=[
            pltpu.VMEM((2, BM * TOK_TILE_ROWS, LANES), jnp.uint32),
            pltpu.VMEM((2, BM, D_MODEL), _F32),
            pltpu.SemaphoreType.DMA((2,)),
            pltpu.SemaphoreType.DMA((2,)),
        ],
    )
    return pl.pallas_call(
        _moe_kernel,
        grid_spec=grid_spec,
        out_shape=jax.ShapeDtypeStruct((2 * t, D_MODEL), _F32),
        compiler_params=pltpu.CompilerParams(dimension_semantics=("arbitrary",), vmem_limit_bytes=VMEM_LIMIT),
        name="moe",
    )(block_e, block_cnt, tok, tok, dst, wts, hp.reshape(t, TOK_TILE_ROWS, LANES), w1, w3, w2)


def _route(t, logits):
    lg = logits[:, :N_GROUPS]
    pg = jax.nn.softmax(lg, axis=-1)
    g_idx = jnp.argmax(pg, axis=-1)
    g_w = jnp.take_along_axis(pg, g_idx[:, None], axis=-1)
    le = logits[:, N_GROUPS:N_GROUPS + N_EXPERTS].reshape(t, N_GROUPS, E_PER_GROUP)
    le = jnp.take_along_axis(le, g_idx[:, None, None], axis=1)[:, 0]
    pe = jax.nn.softmax(le, axis=-1)
    top_p, top_i = lax.top_k(pe, TOP_K)
    wts = g_w * top_p / jnp.sum(top_p, axis=-1, keepdims=True)
    expert_ids = (g_idx[:, None] * E_PER_GROUP + top_i).reshape(-1).astype(jnp.int32)
    w_flat = wts.reshape(-1)

    n_assign = t * TOP_K
    n_blocks = n_assign // BM + N_EXPERTS
    p_len = n_blocks * BM
    onehot = (expert_ids[:, None] == jnp.arange(N_EXPERTS, dtype=jnp.int32)[None, :]).astype(jnp.int32)
    csum = jnp.cumsum(onehot, axis=0)
    rank = jnp.sum(onehot * (csum - 1), axis=1)
    counts = csum[-1]
    padded = ((counts + BM - 1) // BM) * BM
    pend = jnp.cumsum(padded)
    pstart = pend - padded
    dest = pstart[expert_ids] + rank
    a_idx = jnp.arange(n_assign, dtype=jnp.int32)
    slot_a = jnp.full((p_len,), -1, jnp.int32).at[dest].set(a_idx)
    a_safe = jnp.maximum(slot_a, 0)
    buf_tok = a_safe // TOP_K
    buf_dst = (a_safe % TOP_K) * t + buf_tok
    buf_w = jnp.where(slot_a >= 0, w_flat[a_safe], 0.0)
    starts = jnp.arange(n_blocks, dtype=jnp.int32) * BM
    block_e = jnp.minimum(jnp.sum(pend[None, :] <= starts[:, None], axis=1), N_EXPERTS - 1).astype(jnp.int32)
    seg_end = (pstart + counts)[block_e]
    block_cnt = jnp.where(starts < pend[-1], jnp.clip(seg_end - starts, 0, BM), 0).astype(jnp.int32)
    return (n_blocks, block_e, block_cnt, buf_tok.reshape(n_blocks, 1, BM), buf_dst.reshape(n_blocks, 1, BM),
            buf_w.reshape(n_blocks, BM, 1))


def _final_kernel(x_ref, y0_ref, y1_ref, g_ref, o_ref):
    x = x_ref[...] + y0_ref[...] + y1_ref[...]
    o_ref[...] = x * _rms_scale(x) * g_ref[...]


def _final_call(t, x, y, g):
    tm = TM_OUT
    n_tiles = t // tm
    row = lambda i: (i, 0)
    return pl.pallas_call(
        _final_kernel,
        grid=(n_tiles,),
        in_specs=[pl.BlockSpec((tm, D_MODEL), row), pl.BlockSpec((tm, D_MODEL), row),
                  pl.BlockSpec((tm, D_MODEL), lambda i: (i + n_tiles, 0)), _resident(g.shape)],
        out_specs=pl.BlockSpec((tm, D_MODEL), row),
        out_shape=jax.ShapeDtypeStruct((t, D_MODEL), _F32),
        compiler_params=pltpu.CompilerParams(dimension_semantics=("arbitrary",), vmem_limit_bytes=VMEM_LIMIT),
        name="final_norm",
    )(x, y, y, g)


def kernel(x, norm_mix, w_in, pool_w, pool_scale, sgu_ln_g, sgu_ln_b, sgu_w, sgu_b, conv_dw_w, conv_dw_b,
           conv_ln_g, conv_ln_b, conv_pw, w_out, norm_ffn, router_g_w, router_g_b, router_e_w, router_e_b,
           exp_w1, exp_w3, exp_w2, norm_final):
    b, s, d = x.shape
    depth = w_in.shape[0]
    assert d == D_MODEL and s % TM_MIX == 0 and (b * s) % TM_OUT == 0 and TM_MIX % CHUNK == 0
    t = b * s
    xf = x.reshape(t, d)
    y = None
    for l in range(depth):
        row2 = lambda v: v[l].reshape(1, -1)
        sbias = jnp.repeat(sgu_b[l].T, SGU_HEAD_DIM, axis=1)
        outs = _mixer_call(
            y is not None, t, s, xf, y, row2(norm_mix), w_in[l].astype(_BF16), pool_w[l].astype(_BF16),
            row2(pool_scale), row2(sgu_ln_g), row2(sgu_ln_b), sgu_w[l], sbias, conv_dw_w[l], row2(conv_dw_b),
            row2(conv_ln_g), row2(conv_ln_b), conv_pw[l].astype(_BF16))
        if y is not None:
            ycat, xf = outs
        else:
            (ycat,) = outs
        pad = ROUTER_COLS - N_GROUPS - N_EXPERTS
        wr = jnp.concatenate([router_g_w[l], router_e_w[l], jnp.zeros((d, pad), _F32)], axis=1).astype(_BF16)
        br = jnp.concatenate([router_g_b[l], router_e_b[l], jnp.zeros((pad,), _F32)]).reshape(1, ROUTER_COLS)
        xf, hp, logits = _out_call(t, ycat, xf, w_out[l].astype(_BF16), row2(norm_ffn), wr, br)
        n_blocks, block_e, block_cnt, tok, dst, wts = _route(t, logits)
        y = _moe_call(t, l, n_blocks, block_e, block_cnt, tok, dst, wts, hp, exp_w1, exp_w3, exp_w2)
    out = _final_call(t, xf, y, norm_final.reshape(1, -1))
    return out.reshape(b, s, d)
```

```python
import functools

import jax
import jax.numpy as jnp
from jax import lax
from jax.experimental import pallas as pl
from jax.experimental.pallas import tpu as pltpu

D_MODEL = 2048
D_A = D_MODEL // 4
D_B = 3 * D_MODEL // 8
D_C = D_MODEL - D_A - D_B
POOL_WINDOWS = (2, 4, 8, 16)
POOL_GROUP_DIM = D_A // len(POOL_WINDOWS)
CHUNK = 128
SGU_HEAD_DIM = 128
SGU_HEADS = D_B // SGU_HEAD_DIM
CONV_WIDTH = 31
N_IN = D_A + 2 * D_B + 2 * D_C
N_GROUPS = 4
E_PER_GROUP = 8
N_EXPERTS = N_GROUPS * E_PER_GROUP
TOP_K = 2
D_EXPERT = D_MODEL // 4
EPS = 1e-6

LANES = 128
POOL_HALO = 16
CONV_HALO = 32
CONV_ROWS = 64
CONV_PITCH = 2
TM_MIX = 256
TM_OUT = 256
BM = 256
TOK_TILE_ROWS = D_MODEL // 2 // LANES
ISSUE_UNROLL_LOG2 = 3
ISSUE_UNROLL = 1 << ISSUE_UNROLL_LOG2
ROUTER_COLS = LANES
VMEM_LIMIT = 56 * 1024 * 1024

_F32 = jnp.float32
_BF16 = jnp.bfloat16


def _resident(shape):
    nd = len(shape)
    return pl.BlockSpec(shape, lambda *_: (0,) * nd, pipeline_mode=pl.Buffered(1))


def _rms_scale(x):
    return lax.rsqrt(jnp.mean(x * x, axis=-1, keepdims=True) + EPS)


def _layer_norm(x, g, b):
    mu = jnp.mean(x, axis=-1, keepdims=True)
    xc = x - mu
    return xc * lax.rsqrt(jnp.mean(xc * xc, axis=-1, keepdims=True) + EPS) * g + b


def _gelu_tanh(x):
    return 0.5 * x * (1.0 + jnp.tanh(0.7978845608028654 * (x + 0.044715 * (x * x * x))))


def _sigmoid(x):
    return 1.0 / (1.0 + jnp.exp(-x))


def _mixer_kernel(combine, tiles_per_seq, *refs):
    if combine:
        x_ref, y0_ref, y1_ref = refs[:3]
        refs = refs[3:]
    else:
        x_ref = refs[0]
        refs = refs[1:]
    (g_ref, win_ref, poolw_ref, pscale_ref, sg_ref, sb_ref, sw_ref, sbias_ref,
     dww_ref, dwb_ref, cg_ref, cb_ref, pw_ref) = refs[:13]
    refs = refs[13:]
    if combine:
        ycat_ref, xo_ref = refs[:2]
        refs = refs[2:]
    else:
        ycat_ref = refs[0]
        refs = refs[1:]
    h_s, xa_s, z_s, cv_s = refs
    tm = ycat_ref.shape[0]

    tile = pl.program_id(0) % tiles_per_seq
    first = tile == 0

    if combine:
        x = x_ref[...] + y0_ref[...] + y1_ref[...]
        xo_ref[...] = x
    else:
        x = x_ref[...]
    n_ct = D_C // LANES
    halo_src = pl.ds(CONV_PITCH * tm, CONV_HALO, stride=CONV_PITCH)
    halo_dst = pl.ds(0, CONV_HALO, stride=CONV_PITCH)

    @pl.when(first)
    def _():
        xa_s[0:POOL_HALO, :] = jnp.zeros((POOL_HALO, D_A), _F32)
        for c in range(n_ct):
            z_s[c, halo_dst, :] = jnp.zeros((CONV_HALO, LANES), _F32)

    @pl.when(jnp.logical_not(first))
    def _():
        xa_s[0:POOL_HALO, :] = xa_s[tm:tm + POOL_HALO, :]
        for c in range(n_ct):
            z_s[c, halo_dst, :] = z_s[c, halo_src, :]

    h_s[...] = (x * _rms_scale(x) * g_ref[...]).astype(_BF16)

    pc = jnp.dot(h_s[...], win_ref[:, D_A + 2 * D_B:N_IN], preferred_element_type=_F32)
    zc = pc[:, :D_C] * _sigmoid(pc[:, D_C:])
    for c in range(n_ct):
        z_s[c, pl.ds(CONV_PITCH * CONV_HALO, tm, stride=CONV_PITCH), :] = zc[:, c * LANES:(c + 1) * LANES]
    for r0 in range(0, tm, CONV_ROWS):
        for c in range(n_ct):
            lanes = slice(c * LANES, (c + 1) * LANES)
            acc = jnp.zeros((CONV_ROWS, LANES), _F32)
            for k in range(CONV_WIDTH):
                off = r0 + CONV_HALO - (CONV_WIDTH - 1) + k
                acc = acc + dww_ref[k:k + 1, lanes] * z_s[c, pl.ds(CONV_PITCH * off, CONV_ROWS, stride=CONV_PITCH), :]
            cv_s[r0:r0 + CONV_ROWS, lanes] = acc + dwb_ref[:, lanes]

    xa_s[POOL_HALO:POOL_HALO + tm, :] = jnp.dot(h_s[...], win_ref[:, 0:D_A], preferred_element_type=_F32)
    pos = tile * tm + lax.broadcasted_iota(jnp.int32, (tm, 1), 0)
    for gi, w in enumerate(POOL_WINDOWS):
        cols = slice(gi * POOL_GROUP_DIM, (gi + 1) * POOL_GROUP_DIM)
        xg = xa_s[POOL_HALO:POOL_HALO + tm, cols]
        acc = xg
        for k in range(1, w):
            acc = acc + xa_s[POOL_HALO - k:POOL_HALO - k + tm, cols]
        cnt = jnp.minimum(pos + 1, w).astype(_F32)
        d = acc / cnt - xg
        yg = jnp.dot(d.astype(_BF16), poolw_ref[gi], preferred_element_type=_F32) * pscale_ref[:, cols]
        ycat_ref[:, cols] = yg.astype(_BF16)

    pb = jnp.dot(h_s[...], win_ref[:, D_A:D_A + 2 * D_B], preferred_element_type=_F32)
    zb = _gelu_tanh(pb)
    u = zb[:, :D_B]
    vb = _layer_norm(zb[:, D_B:], sg_ref[...], sb_ref[...]).astype(_BF16)
    r_i = lax.broadcasted_iota(jnp.int32, (CHUNK, CHUNK), 0)
    c_i = lax.broadcasted_iota(jnp.int32, (CHUNK, CHUNK), 1)
    causal = r_i >= c_i
    for hd in range(SGU_HEADS):
        hc = slice(hd * SGU_HEAD_DIM, (hd + 1) * SGU_HEAD_DIM)
        wm = jnp.where(causal, sw_ref[hd], 0.0).astype(_BF16)
        for c in range(tm // CHUNK):
            rows = slice(c * CHUNK, (c + 1) * CHUNK)
            gate = jnp.dot(wm, vb[rows, hc], preferred_element_type=_F32) + sbias_ref[:, hc]
            ycat_ref[rows, D_A + hd * SGU_HEAD_DIM:D_A + (hd + 1) * SGU_HEAD_DIM] = (u[rows, hc] * gate).astype(_BF16)

    zn = _layer_norm(cv_s[...], cg_ref[...], cb_ref[...])
    zs = (zn * _sigmoid(zn)).astype(_BF16)
    ycat_ref[:, D_A + D_B:D_MODEL] = jnp.dot(zs, pw_ref[...], preferred_element_type=_F32).astype(_BF16)


def _mixer_call(combine, t, seq, x, y, g, win, poolw, pscale, sg, sb, sw, sbias, dww, dwb, cg, cb, pw):
    tm = TM_MIX
    n_tiles = t // tm
    row = lambda i: (i, 0)
    act = pl.BlockSpec((tm, D_MODEL), row)
    in_specs = [act]
    args = [x]
    if combine:
        in_specs += [pl.BlockSpec((tm, D_MODEL), row), pl.BlockSpec((tm, D_MODEL), lambda i: (i + n_tiles, 0))]
        args += [y, y]
    weights = [g, win, poolw, pscale, sg, sb, sw, sbias, dww, dwb, cg, cb, pw]
    in_specs += [_resident(w.shape) for w in weights]
    args += weights
    out_shape = [jax.ShapeDtypeStruct((t, D_MODEL), _BF16)]
    out_specs = [pl.BlockSpec((tm, D_MODEL), row)]
    if combine:
        out_shape.append(jax.ShapeDtypeStruct((t, D_MODEL), _F32))
        out_specs.append(pl.BlockSpec((tm, D_MODEL), row))
    return pl.pallas_call(
        functools.partial(_mixer_kernel, combine, seq // tm),
        grid=(n_tiles,),
        in_specs=in_specs,
        out_specs=out_specs,
        out_shape=out_shape,
        scratch_shapes=[
            pltpu.VMEM((tm, D_MODEL), _BF16),
            pltpu.VMEM((POOL_HALO + tm, D_A), _F32),
            pltpu.VMEM((D_C // LANES, CONV_PITCH * (CONV_HALO + tm), LANES), _F32),
            pltpu.VMEM((tm, D_C), _F32),
        ],
        compiler_params=pltpu.CompilerParams(dimension_semantics=("arbitrary",), vmem_limit_bytes=VMEM_LIMIT),
        name="mixer_combine" if combine else "mixer",
    )(*args)


def _pack_bf16_pairs(hb):
    u = lax.bitcast_convert_type(hb.astype(_F32), jnp.uint32)
    c = hb.shape[1] // 2
    return (u[:, c:] & jnp.uint32(0xFFFF0000)) | (u[:, :c] >> 16)


def _unpack_bf16_pairs(p):
    lo = lax.bitcast_convert_type(p << 16, _F32).astype(_BF16)
    hi = lax.bitcast_convert_type(p & jnp.uint32(0xFFFF0000), _F32).astype(_BF16)
    return lo, hi


def _out_kernel(ycat_ref, x_ref, wout_ref, g_ref, wr_ref, br_ref, x1_ref, hp_ref, lg_ref):
    x1 = x_ref[...] + jnp.dot(ycat_ref[...], wout_ref[...], preferred_element_type=_F32)
    x1_ref[...] = x1
    hb = (x1 * _rms_scale(x1) * g_ref[...]).astype(_BF16)
    lg_ref[...] = jnp.dot(hb, wr_ref[...], preferred_element_type=_F32) + br_ref[...]
    packed = _pack_bf16_pairs(hb)
    tm = packed.shape[0]
    for s in range(TOK_TILE_ROWS):
        hp_ref[pl.ds(s, tm, stride=TOK_TILE_ROWS), :] = packed[:, s * LANES:(s + 1) * LANES]


def _out_call(t, ycat, x, wout, g, wr, br):
    tm = TM_OUT
    row = lambda i: (i, 0)
    return pl.pallas_call(
        _out_kernel,
        grid=(t // tm,),
        in_specs=[pl.BlockSpec((tm, D_MODEL), row), pl.BlockSpec((tm, D_MODEL), row),
                  _resident(wout.shape), _resident(g.shape), _resident(wr.shape), _resident(br.shape)],
        out_specs=[pl.BlockSpec((tm, D_MODEL), row), pl.BlockSpec((tm * TOK_TILE_ROWS, LANES), row),
                   pl.BlockSpec((tm, ROUTER_COLS), row)],
        out_shape=[jax.ShapeDtypeStruct((t, D_MODEL), _F32),
                   jax.ShapeDtypeStruct((t * TOK_TILE_ROWS, LANES), jnp.uint32),
                   jax.ShapeDtypeStruct((t, ROUTER_COLS), _F32)],
        compiler_params=pltpu.CompilerParams(dimension_semantics=("arbitrary",), vmem_limit_bytes=VMEM_LIMIT),
        name="out_proj",
    )(ycat, x, wout, g, wr, br)


def _wait_rows(src, dst, sem, n):
    n8 = pl.multiple_of(lax.shift_left(lax.shift_right_logical(n, 3), 3), 8)

    @pl.when(n8 > 0)
    def _():
        pltpu.make_async_copy(src.at[pl.ds(0, n8)], dst.at[pl.ds(0, n8)], sem).wait()

    def one(r, carry):
        pltpu.make_async_copy(src.at[pl.ds(0, 1)], dst.at[pl.ds(0, 1)], sem).wait()
        return carry

    lax.fori_loop(n8, n, one, 0)


def _start_rows(n, start_one, static_when_full=True):
    def looped():
        groups = lax.shift_right_logical(n, ISSUE_UNROLL_LOG2)

        def group(gi, carry):
            r0 = gi * ISSUE_UNROLL
            for j in range(ISSUE_UNROLL):
                start_one(r0 + j)
            return carry

        lax.fori_loop(0, groups, group, 0)

        def one(r, carry):
            start_one(r)
            return carry

        lax.fori_loop(groups * ISSUE_UNROLL, n, one, 0)

    if not static_when_full:
        looped()
        return

    @pl.when(n == BM)
    def _():
        for r in range(BM):
            start_one(r)

    @pl.when(n < BM)
    def _():
        looped()


def _wait_tokens(xb, sem, n):
    @pl.when(n > 0)
    def _():
        rows = pl.multiple_of(n * TOK_TILE_ROWS, TOK_TILE_ROWS)
        view = xb.at[pl.ds(0, rows)]
        pltpu.make_async_copy(view, view, sem).wait()


def _moe_kernel(be_ref, cnt_ref, tok_ref, tok_next_ref, dst_ref, wt_ref, hp_hbm, w1_ref, w3_ref, w2_ref, y_hbm,
                xb_s, yb_s, gsem, ssem):
    del be_ref
    half = D_MODEL // 2
    step = pl.program_id(0)
    last = pl.num_programs(0) - 1
    slot = lax.rem(step, 2)
    n = cnt_ref[step]

    def start_gather(table_ref, rows, to_slot, static_when_full):
        def one(r):
            r0 = r * TOK_TILE_ROWS if isinstance(r, int) else pl.multiple_of(r * TOK_TILE_ROWS, TOK_TILE_ROWS)
            pltpu.make_async_copy(hp_hbm.at[table_ref[0, 0, r]], xb_s.at[to_slot, pl.ds(r0, TOK_TILE_ROWS)],
                                  gsem.at[to_slot]).start()

        _start_rows(rows, one, static_when_full)

    @pl.when(step == 0)
    def _():
        xb_s[...] = jnp.zeros(xb_s.shape, xb_s.dtype)
        start_gather(tok_ref, n, 0, False)

    @pl.when(step < last)
    def _():
        start_gather(tok_next_ref, cnt_ref[step + 1], 1 - slot, True)

    _wait_tokens(xb_s.at[slot], gsem.at[slot], n)

    @pl.when(step >= 2)
    def _():
        _wait_rows(yb_s.at[slot], y_hbm, ssem.at[slot], cnt_ref[step - 2])

    @pl.when(n > 0)
    def _():
        xb = xb_s.at[slot]
        packed = jnp.concatenate([xb[pl.ds(s, BM, stride=TOK_TILE_ROWS), :] for s in range(TOK_TILE_ROWS)], axis=1)
        lo, hi = _unpack_bf16_pairs(packed)
        a1 = (jnp.dot(lo, w1_ref[0, 0:half, :].astype(_BF16), preferred_element_type=_F32)
              + jnp.dot(hi, w1_ref[0, half:D_MODEL, :].astype(_BF16), preferred_element_type=_F32))
        a3 = (jnp.dot(lo, w3_ref[0, 0:half, :].astype(_BF16), preferred_element_type=_F32)
              + jnp.dot(hi, w3_ref[0, half:D_MODEL, :].astype(_BF16), preferred_element_type=_F32))
        hid = (a1 * _sigmoid(a1) * a3).astype(_BF16)
        yb_s[slot] = jnp.dot(hid, w2_ref[0].astype(_BF16), preferred_element_type=_F32) * wt_ref[0]

        def one(r):
            pltpu.make_async_copy(yb_s.at[slot, pl.ds(r, 1)], y_hbm.at[pl.ds(dst_ref[0, 0, r], 1)],
                                  ssem.at[slot]).start()

        _start_rows(n, one)

    @pl.when(step == last)
    def _():
        _wait_rows(yb_s.at[slot], y_hbm, ssem.at[slot], n)

        @pl.when(step >= 1)
        def _():
            _wait_rows(yb_s.at[1 - slot], y_hbm, ssem.at[1 - slot], cnt_ref[step - 1])


def _moe_call(t, layer, n_blocks, block_e, block_cnt, tok, dst, wts, hp, w1, w3, w2):
    smem_rows = pl.BlockSpec((1, 1, BM), lambda i, be, nv: (i, 0, 0), memory_space=pltpu.SMEM)
    smem_rows_next = pl.BlockSpec((1, 1, BM), lambda i, be, nv: (jnp.minimum(i + 1, n_blocks - 1), 0, 0),
                                  memory_space=pltpu.SMEM)
    grid_spec = pltpu.PrefetchScalarGridSpec(
        num_scalar_prefetch=2,
        grid=(n_blocks,),
        in_specs=[
            smem_rows,
            smem_rows_next,
            smem_rows,
            pl.BlockSpec((1, BM, 1), lambda i, be, nv: (i, 0, 0)),
            pl.BlockSpec(memory_space=pl.ANY),
            pl.BlockSpec((None, 1, D_MODEL, D_EXPERT), lambda i, be, nv: (layer, be[i], 0, 0)),
            pl.BlockSpec((None, 1, D_MODEL, D_EXPERT), lambda i, be, nv: (layer, be[i], 0, 0)),
            pl.BlockSpec((None, 1, D_EXPERT, D_MODEL), lambda i, be, nv: (layer, be[i], 0, 0)),
        ],
        out_specs=pl.BlockSpec(memory_space=pl.ANY),
        scratch_shapes=[
            pltpu.VMEM((2, BM * TOK_TILE_ROWS, LANES), jnp.uint32),
            pltpu.VMEM((2, BM, D_MODEL), _F32),
            pltpu.SemaphoreType.DMA((2,)),
            pltpu.SemaphoreType.DMA((2,)),
        ],
    )
    return pl.pallas_call(
        _moe_kernel,
        grid_spec=grid_spec,
        out_shape=jax.ShapeDtypeStruct((2 * t, D_MODEL), _F32),
        compiler_params=pltpu.CompilerParams(dimension_semantics=("arbitrary",), vmem_limit_bytes=VMEM_LIMIT),
        name="moe",
    )(block_e, block_cnt, tok, tok, dst, wts, hp.reshape(t, TOK_TILE_ROWS, LANES), w1, w3, w2)


def _route(t, logits):
    lg = logits[:, :N_GROUPS]
    pg = jax.nn.softmax(lg, axis=-1)
    g_idx = jnp.argmax(pg, axis=-1)
    g_w = jnp.take_along_axis(pg, g_idx[:, None], axis=-1)
    le = logits[:, N_GROUPS:N_GROUPS + N_EXPERTS].reshape(t, N_GROUPS, E_PER_GROUP)
    le = jnp.take_along_axis(le, g_idx[:, None, None], axis=1)[:, 0]
    pe = jax.nn.softmax(le, axis=-1)
    top_p, top_i = lax.top_k(pe, TOP_K)
    wts = g_w * top_p / jnp.sum(top_p, axis=-1, keepdims=True)
    expert_ids = (g_idx[:, None] * E_PER_GROUP + top_i).reshape(-1).astype(jnp.int32)
    w_flat = wts.reshape(-1)

    n_assign = t * TOP_K
    n_blocks = n_assign // BM + N_EXPERTS
    p_len = n_blocks * BM
    onehot = (expert_ids[:, None] == jnp.arange(N_EXPERTS, dtype=jnp.int32)[None, :]).astype(jnp.int32)
    csum = jnp.cumsum(onehot, axis=0)
    rank = jnp.sum(onehot * (csum - 1), axis=1)
    counts = csum[-1]
    padded = ((counts + BM - 1) // BM) * BM
    pend = jnp.cumsum(padded)
    pstart = pend - padded
    dest = pstart[expert_ids] + rank
    a_idx = jnp.arange(n_assign, dtype=jnp.int32)
    slot_a = jnp.full((p_len,), -1, jnp.int32).at[dest].set(a_idx)
    a_safe = jnp.maximum(slot_a, 0)
    buf_tok = a_safe // TOP_K
    buf_dst = (a_safe % TOP_K) * t + buf_tok
    buf_w = jnp.where(slot_a >= 0, w_flat[a_safe], 0.0)
    starts = jnp.arange(n_blocks, dtype=jnp.int32) * BM
    block_e = jnp.minimum(jnp.sum(pend[None, :] <= starts[:, None], axis=1), N_EXPERTS - 1).astype(jnp.int32)
    seg_end = (pstart + counts)[block_e]
    block_cnt = jnp.where(starts < pend[-1], jnp.clip(seg_end - starts, 0, BM), 0).astype(jnp.int32)
    return (n_blocks, block_e, block_cnt, buf_tok.reshape(n_blocks, 1, BM), buf_dst.reshape(n_blocks, 1, BM),
            buf_w.reshape(n_blocks, BM, 1))


def _final_kernel(x_ref, y0_ref, y1_ref, g_ref, o_ref):
    x = x_ref[...] + y0_ref[...] + y1_ref[...]
    o_ref[...] = x * _rms_scale(x) * g_ref[...]


def _final_call(t, x, y, g):
    tm = TM_OUT
    n_tiles = t // tm
    row = lambda i: (i, 0)
    return pl.pallas_call(
        _final_kernel,
        grid=(n_tiles,),
        in_specs=[pl.BlockSpec((tm, D_MODEL), row), pl.BlockSpec((tm, D_MODEL), row),
                  pl.BlockSpec((tm, D_MODEL), lambda i: (i + n_tiles, 0)), _resident(g.shape)],
        out_specs=pl.BlockSpec((tm, D_MODEL), row),
        out_shape=jax.ShapeDtypeStruct((t, D_MODEL), _F32),
        compiler_params=pltpu.CompilerParams(dimension_semantics=("arbitrary",), vmem_limit_bytes=VMEM_LIMIT),
        name="final_norm",
    )(x, y, y, g)


def kernel(x, norm_mix, w_in, pool_w, pool_scale, sgu_ln_g, sgu_ln_b, sgu_w, sgu_b, conv_dw_w, conv_dw_b,
           conv_ln_g, conv_ln_b, conv_pw, w_out, norm_ffn, router_g_w, router_g_b, router_e_w, router_e_b,
           exp_w1, exp_w3, exp_w2, norm_final):
    b, s, d = x.shape
    depth = w_in.shape[0]
    assert d == D_MODEL and s % TM_MIX == 0 and (b * s) % TM_OUT == 0 and TM_MIX % CHUNK == 0
    t = b * s
    xf = x.reshape(t, d)
    y = None
    for l in range(depth):
        row2 = lambda v: v[l].reshape(1, -1)
        sbias = jnp.repeat(sgu_b[l].T, SGU_HEAD_DIM, axis=1)
        outs = _mixer_call(
            y is not None, t, s, xf, y, row2(norm_mix), w_in[l].astype(_BF16), pool_w[l].astype(_BF16),
            row2(pool_scale), row2(sgu_ln_g), row2(sgu_ln_b), sgu_w[l], sbias, conv_dw_w[l], row2(conv_dw_b),
            row2(conv_ln_g), row2(conv_ln_b), conv_pw[l].astype(_BF16))
        if y is not None:
            ycat, xf = outs
        else:
            (ycat,) = outs
        pad = ROUTER_COLS - N_GROUPS - N_EXPERTS
        wr = jnp.concatenate([router_g_w[l], router_e_w[l], jnp.zeros((d, pad), _F32)], axis=1).astype(_BF16)
        br = jnp.concatenate([router_g_b[l], router_e_b[l], jnp.zeros((pad,), _F32)]).reshape(1, ROUTER_COLS)
        xf, hp, logits = _out_call(t, ycat, xf, w_out[l].astype(_BF16), row2(norm_ffn), wr, br)
        n_blocks, block_e, block_cnt, tok, dst, wts = _route(t, logits)
        y = _moe_call(t, l, n_blocks, block_e, block_cnt, tok, dst, wts, hp, exp_w1, exp_w3, exp_w2)
    out = _final_call(t, xf, y, norm_final.reshape(1, -1))
    return out.reshape(b, s, d)
```

```python
import functools

import jax
import jax.numpy as jnp
from jax import lax
from jax.experimental import pallas as pl
from jax.experimental.pallas import tpu as pltpu

D_MODEL = 2048
D_A = D_MODEL // 4
D_B = 3 * D_MODEL // 8
D_C = D_MODEL - D_A - D_B
POOL_WINDOWS = (2, 4, 8, 16)
POOL_GROUP_DIM = D_A // len(POOL_WINDOWS)
CHUNK = 128
SGU_HEAD_DIM = 128
SGU_HEADS = D_B // SGU_HEAD_DIM
CONV_WIDTH = 31
N_IN = D_A + 2 * D_B + 2 * D_C
N_GROUPS = 4
E_PER_GROUP = 8
N_EXPERTS = N_GROUPS * E_PER_GROUP
TOP_K = 2
D_EXPERT = D_MODEL // 4
EPS = 1e-6

LANES = 128
POOL_HALO = 16
CONV_HALO = 32
CONV_ROWS = 64
CONV_PITCH = 2
TM_MIX = 256
TM_OUT = 256
BM = 256
TOK_TILE_ROWS = D_MODEL // 2 // LANES
ISSUE_UNROLL_LOG2 = 3
ISSUE_UNROLL = 1 << ISSUE_UNROLL_LOG2
ROUTER_COLS = LANES
VMEM_LIMIT = 56 * 1024 * 1024

_F32 = jnp.float32
_BF16 = jnp.bfloat16


def _resident(shape):
    nd = len(shape)
    return pl.BlockSpec(shape, lambda *_: (0,) * nd, pipeline_mode=pl.Buffered(1))


def _rms_scale(x):
    return lax.rsqrt(jnp.mean(x * x, axis=-1, keepdims=True) + EPS)


def _layer_norm(x, g, b):
    mu = jnp.mean(x, axis=-1, keepdims=True)
    xc = x - mu
    return xc * lax.rsqrt(jnp.mean(xc * xc, axis=-1, keepdims=True) + EPS) * g + b


def _gelu_tanh(x):
    return 0.5 * x * (1.0 + jnp.tanh(0.7978845608028654 * (x + 0.044715 * (x * x * x))))


def _sigmoid(x):
    return 1.0 / (1.0 + jnp.exp(-x))


def _mixer_kernel(combine, tiles_per_seq, *refs):
    if combine:
        x_ref, y0_ref, y1_ref = refs[:3]
        refs = refs[3:]
    else:
        x_ref = refs[0]
        refs = refs[1:]
    (g_ref, win_ref, poolw_ref, pscale_ref, sg_ref, sb_ref, sw_ref, sbias_ref,
     dww_ref, dwb_ref, cg_ref, cb_ref, pw_ref) = refs[:13]
    refs = refs[13:]
    if combine:
        ycat_ref, xo_ref = refs[:2]
        refs = refs[2:]
    else:
        ycat_ref = refs[0]
        refs = refs[1:]
    h_s, xa_s, z_s, cv_s = refs
    tm = ycat_ref.shape[0]

    tile = pl.program_id(0) % tiles_per_seq
    first = tile == 0

    if combine:
        x = x_ref[...] + y0_ref[...] + y1_ref[...]
        xo_ref[...] = x
    else:
        x = x_ref[...]
    n_ct = D_C // LANES
    halo_src = pl.ds(CONV_PITCH * tm, CONV_HALO, stride=CONV_PITCH)
    halo_dst = pl.ds(0, CONV_HALO, stride=CONV_PITCH)

    @pl.when(first)
    def _():
        xa_s[0:POOL_HALO, :] = jnp.zeros((POOL_HALO, D_A), _F32)
        for c in range(n_ct):
            z_s[c, halo_dst, :] = jnp.zeros((CONV_HALO, LANES), _F32)

    @pl.when(jnp.logical_not(first))
    def _():
        xa_s[0:POOL_HALO, :] = xa_s[tm:tm + POOL_HALO, :]
        for c in range(n_ct):
            z_s[c, halo_dst, :] = z_s[c, halo_src, :]

    h_s[...] = (x * _rms_scale(x) * g_ref[...]).astype(_BF16)

    pc = jnp.dot(h_s[...], win_ref[:, D_A + 2 * D_B:N_IN], preferred_element_type=_F32)
    zc = pc[:, :D_C] * _sigmoid(pc[:, D_C:])
    for c in range(n_ct):
        z_s[c, pl.ds(CONV_PITCH * CONV_HALO, tm, stride=CONV_PITCH), :] = zc[:, c * LANES:(c + 1) * LANES]
    for r0 in range(0, tm, CONV_ROWS):
        for c in range(n_ct):
            lanes = slice(c * LANES, (c + 1) * LANES)
            acc = jnp.zeros((CONV_ROWS, LANES), _F32)
            for k in range(CONV_WIDTH):
                off = r0 + CONV_HALO - (CONV_WIDTH - 1) + k
                acc = acc + dww_ref[k:k + 1, lanes] * z_s[c, pl.ds(CONV_PITCH * off, CONV_ROWS, stride=CONV_PITCH), :]
            cv_s[r0:r0 + CONV_ROWS, lanes] = acc + dwb_ref[:, lanes]

    xa_s[POOL_HALO:POOL_HALO + tm, :] = jnp.dot(h_s[...], win_ref[:, 0:D_A], preferred_element_type=_F32)
    pos = tile * tm + lax.broadcasted_iota(jnp.int32, (tm, 1), 0)
    for gi, w in enumerate(POOL_WINDOWS):
        cols = slice(gi * POOL_GROUP_DIM, (gi + 1) * POOL_GROUP_DIM)
        xg = xa_s[POOL_HALO:POOL_HALO + tm, cols]
        acc = xg
        for k in range(1, w):
            acc = acc + xa_s[POOL_HALO - k:POOL_HALO - k + tm, cols]
        cnt = jnp.minimum(pos + 1, w).astype(_F32)
        d = acc / cnt - xg
        yg = jnp.dot(d.astype(_BF16), poolw_ref[gi], preferred_element_type=_F32) * pscale_ref[:, cols]
        ycat_ref[:, cols] = yg.astype(_BF16)

    pb = jnp.dot(h_s[...], win_ref[:, D_A:D_A + 2 * D_B], preferred_element_type=_F32)
    zb = _gelu_tanh(pb)
    u = zb[:, :D_B]
    vb = _layer_norm(zb[:, D_B:], sg_ref[...], sb_ref[...]).astype(_BF16)
    r_i = lax.broadcasted_iota(jnp.int32, (CHUNK, CHUNK), 0)
    c_i = lax.broadcasted_iota(jnp.int32, (CHUNK, CHUNK), 1)
    causal = r_i >= c_i
    for hd in range(SGU_HEADS):
        hc = slice(hd * SGU_HEAD_DIM, (hd + 1) * SGU_HEAD_DIM)
        wm = jnp.where(causal, sw_ref[hd], 0.0).astype(_BF16)
        for c in range(tm // CHUNK):
            rows = slice(c * CHUNK, (c + 1) * CHUNK)
            gate = jnp.dot(wm, vb[rows, hc], preferred_element_type=_F32) + sbias_ref[:, hc]
            ycat_ref[rows, D_A + hd * SGU_HEAD_DIM:D_A + (hd + 1) * SGU_HEAD_DIM] = (u[rows, hc] * gate).astype(_BF16)

    zn = _layer_norm(cv_s[...], cg_ref[...], cb_ref[...])
    zs = (zn * _sigmoid(zn)).astype(_BF16)
    ycat_ref[:, D_A + D_B:D_MODEL] = jnp.dot(zs, pw_ref[...], preferred_element_type=_F32).astype(_BF16)


def _mixer_call(combine, t, seq, x, y, g, win, poolw, pscale, sg, sb, sw, sbias, dww, dwb, cg, cb, pw):
    tm = TM_MIX
    n_tiles = t // tm
    row = lambda i: (i, 0)
    act = pl.BlockSpec((tm, D_MODEL), row)
    in_specs = [act]
    args = [x]
    if combine:
        in_specs += [pl.BlockSpec((tm, D_MODEL), row), pl.BlockSpec((tm, D_MODEL), lambda i: (i + n_tiles, 0))]
        args += [y, y]
    weights = [g, win, poolw, pscale, sg, sb, sw, sbias, dww, dwb, cg, cb, pw]
    in_specs += [_resident(w.shape) for w in weights]
    args += weights
    out_shape = [jax.ShapeDtypeStruct((t, D_MODEL), _BF16)]
    out_specs = [pl.BlockSpec((tm, D_MODEL), row)]
    if combine:
        out_shape.append(jax.ShapeDtypeStruct((t, D_MODEL), _F32))
        out_specs.append(pl.BlockSpec((tm, D_MODEL), row))
    return pl.pallas_call(
        functools.partial(_mixer_kernel, combine, seq // tm),
        grid=(n_tiles,),
        in_specs=in_specs,
        out_specs=out_specs,
        out_shape=out_shape,
        scratch_shapes=[
            pltpu.VMEM((tm, D_MODEL), _BF16),
            pltpu.VMEM((POOL_HALO + tm, D_A), _F32),
            pltpu.VMEM((D_C // LANES, CONV_PITCH * (CONV_HALO + tm), LANES), _F32),
            pltpu.VMEM((tm, D_C), _F32),
        ],
        compiler_params=pltpu.CompilerParams(dimension_semantics=("arbitrary",), vmem_limit_bytes=VMEM_LIMIT),
        name="mixer_combine" if combine else "mixer",
    )(*args)


def _pack_bf16_pairs(hb):
    u = lax.bitcast_convert_type(hb.astype(_F32), jnp.uint32)
    c = hb.shape[1] // 2
    return (u[:, c:] & jnp.uint32(0xFFFF0000)) | (u[:, :c] >> 16)


def _unpack_bf16_pairs(p):
    lo = lax.bitcast_convert_type(p << 16, _F32).astype(_BF16)
    hi = lax.bitcast_convert_type(p & jnp.uint32(0xFFFF0000), _F32).astype(_BF16)
    return lo, hi


def _route_rows(logits):
    lane = lax.broadcasted_iota(jnp.int32, logits.shape, 1)
    neg = jnp.float32(-jnp.inf)
    big = jnp.int32(ROUTER_COLS)

    def first_max(v):
        m = jnp.max(v, axis=-1, keepdims=True)
        return m, jnp.min(jnp.where(v == m, lane, big), axis=-1, keepdims=True)

    lg = jnp.where(lane < N_GROUPS, logits, neg)
    mg, g_idx = first_max(lg)
    g_w = 1.0 / jnp.sum(jnp.exp(lg - mg), axis=-1, keepdims=True)
    e_lo = N_GROUPS + g_idx * E_PER_GROUP
    le = jnp.where((lane >= e_lo) & (lane < e_lo + E_PER_GROUP), logits, neg)
    m1, i1 = first_max(le)
    m2, i2 = first_max(jnp.where(lane == i1, neg, le))
    p2 = jnp.exp(m2 - m1)
    w1 = g_w / (1.0 + p2)
    w2 = g_w * p2 / (1.0 + p2)
    rec = jnp.where(lane == 0, (i1 - N_GROUPS).astype(_F32),
                    jnp.where(lane == 1, (i2 - N_GROUPS).astype(_F32),
                              jnp.where(lane == 2, w1, jnp.where(lane == 3, w2, 0.0))))
    return rec


def _out_kernel(ycat_ref, x_ref, wout_ref, g_ref, wr_ref, br_ref, x1_ref, hp_ref, lg_ref):
    x1 = x_ref[...] + jnp.dot(ycat_ref[...], wout_ref[...], preferred_element_type=_F32)
    x1_ref[...] = x1
    hb = (x1 * _rms_scale(x1) * g_ref[...]).astype(_BF16)
    lg_ref[...] = _route_rows(jnp.dot(hb, wr_ref[...], preferred_element_type=_F32) + br_ref[...])
    packed = _pack_bf16_pairs(hb)
    tm = packed.shape[0]
    for s in range(TOK_TILE_ROWS):
        hp_ref[pl.ds(s, tm, stride=TOK_TILE_ROWS), :] = packed[:, s * LANES:(s + 1) * LANES]


def _out_call(t, ycat, x, wout, g, wr, br):
    tm = TM_OUT
    row = lambda i: (i, 0)
    return pl.pallas_call(
        _out_kernel,
        grid=(t // tm,),
        in_specs=[pl.BlockSpec((tm, D_MODEL), row), pl.BlockSpec((tm, D_MODEL), row),
                  _resident(wout.shape), _resident(g.shape), _resident(wr.shape), _resident(br.shape)],
        out_specs=[pl.BlockSpec((tm, D_MODEL), row), pl.BlockSpec((tm * TOK_TILE_ROWS, LANES), row),
                   pl.BlockSpec((tm, ROUTER_COLS), row)],
        out_shape=[jax.ShapeDtypeStruct((t, D_MODEL), _F32),
                   jax.ShapeDtypeStruct((t * TOK_TILE_ROWS, LANES), jnp.uint32),
                   jax.ShapeDtypeStruct((t, ROUTER_COLS), _F32)],
        compiler_params=pltpu.CompilerParams(dimension_semantics=("arbitrary",), vmem_limit_bytes=VMEM_LIMIT),
        name="out_proj",
    )(ycat, x, wout, g, wr, br)


def _wait_rows(src, dst, sem, n):
    n8 = pl.multiple_of(lax.shift_left(lax.shift_right_logical(n, 3), 3), 8)

    @pl.when(n8 > 0)
    def _():
        pltpu.make_async_copy(src.at[pl.ds(0, n8)], dst.at[pl.ds(0, n8)], sem).wait()

    def one(r, carry):
        pltpu.make_async_copy(src.at[pl.ds(0, 1)], dst.at[pl.ds(0, 1)], sem).wait()
        return carry

    lax.fori_loop(n8, n, one, 0)


def _start_rows(n, start_one, static_when_full=True):
    def looped():
        groups = lax.shift_right_logical(n, ISSUE_UNROLL_LOG2)

        def group(gi, carry):
            r0 = gi * ISSUE_UNROLL
            for j in range(ISSUE_UNROLL):
                start_one(r0 + j)
            return carry

        lax.fori_loop(0, groups, group, 0)

        def one(r, carry):
            start_one(r)
            return carry

        lax.fori_loop(groups * ISSUE_UNROLL, n, one, 0)

    if not static_when_full:
        looped()
        return

    @pl.when(n == BM)
    def _():
        for r in range(BM):
            start_one(r)

    @pl.when(n < BM)
    def _():
        looped()


def _wait_tokens(xb, sem, n):
    @pl.when(n > 0)
    def _():
        rows = pl.multiple_of(n * TOK_TILE_ROWS, TOK_TILE_ROWS)
        view = xb.at[pl.ds(0, rows)]
        pltpu.make_async_copy(view, view, sem).wait()


def _moe_kernel(be_ref, cnt_ref, tok_ref, tok_next_ref, dst_ref, wt_ref, hp_hbm, w1_ref, w3_ref, w2_ref, y_hbm,
                xb_s, yb_s, gsem, ssem):
    del be_ref
    half = D_MODEL // 2
    step = pl.program_id(0)
    last = pl.num_programs(0) - 1
    slot = lax.rem(step, 2)
    n = cnt_ref[step]

    def start_gather(table_ref, rows, to_slot, static_when_full):
        def one(r):
            r0 = r * TOK_TILE_ROWS if isinstance(r, int) else pl.multiple_of(r * TOK_TILE_ROWS, TOK_TILE_ROWS)
            pltpu.make_async_copy(hp_hbm.at[table_ref[0, 0, r]], xb_s.at[to_slot, pl.ds(r0, TOK_TILE_ROWS)],
                                  gsem.at[to_slot]).start()

        _start_rows(rows, one, static_when_full)

    @pl.when(step == 0)
    def _():
        xb_s[...] = jnp.zeros(xb_s.shape, xb_s.dtype)
        start_gather(tok_ref, n, 0, False)

    @pl.when(step < last)
    def _():
        start_gather(tok_next_ref, cnt_ref[step + 1], 1 - slot, True)

    _wait_tokens(xb_s.at[slot], gsem.at[slot], n)

    @pl.when(step >= 2)
    def _():
        _wait_rows(yb_s.at[slot], y_hbm, ssem.at[slot], cnt_ref[step - 2])

    @pl.when(n > 0)
    def _():
        xb = xb_s.at[slot]
        packed = jnp.concatenate([xb[pl.ds(s, BM, stride=TOK_TILE_ROWS), :] for s in range(TOK_TILE_ROWS)], axis=1)
        lo, hi = _unpack_bf16_pairs(packed)
        a1 = (jnp.dot(lo, w1_ref[0, 0:half, :].astype(_BF16), preferred_element_type=_F32)
              + jnp.dot(hi, w1_ref[0, half:D_MODEL, :].astype(_BF16), preferred_element_type=_F32))
        a3 = (jnp.dot(lo, w3_ref[0, 0:half, :].astype(_BF16), preferred_element_type=_F32)
              + jnp.dot(hi, w3_ref[0, half:D_MODEL, :].astype(_BF16), preferred_element_type=_F32))
        hid = (a1 * _sigmoid(a1) * a3).astype(_BF16)
        yb_s[slot] = jnp.dot(hid, w2_ref[0].astype(_BF16), preferred_element_type=_F32) * wt_ref[0]

        def one(r):
            pltpu.make_async_copy(yb_s.at[slot, pl.ds(r, 1)], y_hbm.at[pl.ds(dst_ref[0, 0, r], 1)],
                                  ssem.at[slot]).start()

        _start_rows(n, one)

    @pl.when(step == last)
    def _():
        _wait_rows(yb_s.at[slot], y_hbm, ssem.at[slot], n)

        @pl.when(step >= 1)
        def _():
            _wait_rows(yb_s.at[1 - slot], y_hbm, ssem.at[1 - slot], cnt_ref[step - 1])


def _moe_call(t, layer, n_blocks, block_e, block_cnt, tok, dst, wts, hp, w1, w3, w2):
    smem_rows = pl.BlockSpec((1, 1, BM), lambda i, be, nv: (i, 0, 0), memory_space=pltpu.SMEM)
    smem_rows_next = pl.BlockSpec((1, 1, BM), lambda i, be, nv: (jnp.minimum(i + 1, n_blocks - 1), 0, 0),
                                  memory_space=pltpu.SMEM)
    grid_spec = pltpu.PrefetchScalarGridSpec(
        num_scalar_prefetch=2,
        grid=(n_blocks,),
        in_specs=[
            smem_rows,
            smem_rows_next,
            smem_rows,
            pl.BlockSpec((1, BM, 1), lambda i, be, nv: (i, 0, 0)),
            pl.BlockSpec(memory_space=pl.ANY),
            pl.BlockSpec((None, 1, D_MODEL, D_EXPERT), lambda i, be, nv: (layer, be[i], 0, 0)),
            pl.BlockSpec((None, 1, D_MODEL, D_EXPERT), lambda i, be, nv: (layer, be[i], 0, 0)),
            pl.BlockSpec((None, 1, D_EXPERT, D_MODEL), lambda i, be, nv: (layer, be[i], 0, 0)),
        ],
        out_specs=pl.BlockSpec(memory_space=pl.ANY),
        scratch_shapes=[
            pltpu.VMEM((2, BM * TOK_TILE_ROWS, LANES), jnp.uint32),
            pltpu.VMEM((2, BM, D_MODEL), _F32),
            pltpu.SemaphoreType.DMA((2,)),
            pltpu.SemaphoreType.DMA((2,)),
        ],
    )
    return pl.pallas_call(
        _moe_kernel,
        grid_spec=grid_spec,
        out_shape=jax.ShapeDtypeStruct((2 * t, D_MODEL), _F32),
        compiler_params=pltpu.CompilerParams(dimension_semantics=("arbitrary",), vmem_limit_bytes=VMEM_LIMIT),
        name="moe",
    )(block_e, block_cnt, tok, tok, dst, wts, hp.reshape(t, TOK_TILE_ROWS, LANES), w1, w3, w2)


def _route(t, rec):
    expert_ids = rec[:, 0:TOP_K].astype(jnp.int32).reshape(-1)
    w_flat = rec[:, TOP_K:2 * TOP_K].reshape(-1)

    n_assign = t * TOP_K
    n_blocks = n_assign // BM + N_EXPERTS
    p_len = n_blocks * BM
    order = jnp.argsort(expert_ids, stable=True).astype(jnp.int32)
    counts = jnp.sum((expert_ids[:, None] == jnp.arange(N_EXPERTS, dtype=jnp.int32)[None, :]).astype(jnp.int32), axis=0)
    start = jnp.cumsum(counts) - counts
    padded = ((counts + BM - 1) // BM) * BM
    pend = jnp.cumsum(padded)
    pstart = pend - padded
    starts = jnp.arange(n_blocks, dtype=jnp.int32) * BM
    block_e = jnp.minimum(jnp.sum(pend[None, :] <= starts[:, None], axis=1), N_EXPERTS - 1).astype(jnp.int32)
    seg_end = (pstart + counts)[block_e]
    block_cnt = jnp.where(starts < pend[-1], jnp.clip(seg_end - starts, 0, BM), 0).astype(jnp.int32)
    within = (starts - pstart[block_e])[:, None] + jnp.arange(BM, dtype=jnp.int32)[None, :]
    real = jnp.arange(BM, dtype=jnp.int32)[None, :] < block_cnt[:, None]
    sorted_pos = jnp.clip(start[block_e][:, None] + within, 0, n_assign - 1)
    a_safe = jnp.where(real, order[sorted_pos], 0)
    buf_tok = a_safe // TOP_K
    buf_dst = (a_safe % TOP_K) * t + buf_tok
    buf_w = jnp.where(real, w_flat[a_safe], 0.0)
    return (n_blocks, block_e, block_cnt, buf_tok.reshape(n_blocks, 1, BM), buf_dst.reshape(n_blocks, 1, BM),
            buf_w.reshape(n_blocks, BM, 1))


def _final_kernel(x_ref, y0_ref, y1_ref, g_ref, o_ref):
    x = x_ref[...] + y0_ref[...] + y1_ref[...]
    o_ref[...] = x * _rms_scale(x) * g_ref[...]


def _final_call(t, x, y, g):
    tm = TM_OUT
    n_tiles = t // tm
    row = lambda i: (i, 0)
    return pl.pallas_call(
        _final_kernel,
        grid=(n_tiles,),
        in_specs=[pl.BlockSpec((tm, D_MODEL), row), pl.BlockSpec((tm, D_MODEL), row),
                  pl.BlockSpec((tm, D_MODEL), lambda i: (i + n_tiles, 0)), _resident(g.shape)],
        out_specs=pl.BlockSpec((tm, D_MODEL), row),
        out_shape=jax.ShapeDtypeStruct((t, D_MODEL), _F32),
        compiler_params=pltpu.CompilerParams(dimension_semantics=("arbitrary",), vmem_limit_bytes=VMEM_LIMIT),
        name="final_norm",
    )(x, y, y, g)


def kernel(x, norm_mix, w_in, pool_w, pool_scale, sgu_ln_g, sgu_ln_b, sgu_w, sgu_b, conv_dw_w, conv_dw_b,
           conv_ln_g, conv_ln_b, conv_pw, w_out, norm_ffn, router_g_w, router_g_b, router_e_w, router_e_b,
           exp_w1, exp_w3, exp_w2, norm_final):
    b, s, d = x.shape
    depth = w_in.shape[0]
    assert d == D_MODEL and s % TM_MIX == 0 and (b * s) % TM_OUT == 0 and TM_MIX % CHUNK == 0
    t = b * s
    xf = x.reshape(t, d)
    y = None
    for l in range(depth):
        row2 = lambda v: v[l].reshape(1, -1)
        sbias = jnp.repeat(sgu_b[l].T, SGU_HEAD_DIM, axis=1)
        outs = _mixer_call(
            y is not None, t, s, xf, y, row2(norm_mix), w_in[l].astype(_BF16), pool_w[l].astype(_BF16),
            row2(pool_scale), row2(sgu_ln_g), row2(sgu_ln_b), sgu_w[l], sbias, conv_dw_w[l], row2(conv_dw_b),
            row2(conv_ln_g), row2(conv_ln_b), conv_pw[l].astype(_BF16))
        if y is not None:
            ycat, xf = outs
        else:
            (ycat,) = outs
        pad = ROUTER_COLS - N_GROUPS - N_EXPERTS
        wr = jnp.concatenate([router_g_w[l], router_e_w[l], jnp.zeros((d, pad), _F32)], axis=1).astype(_BF16)
        br = jnp.concatenate([router_g_b[l], router_e_b[l], jnp.zeros((pad,), _F32)]).reshape(1, ROUTER_COLS)
        xf, hp, logits = _out_call(t, ycat, xf, w_out[l].astype(_BF16), row2(norm_ffn), wr, br)
        n_blocks, block_e, block_cnt, tok, dst, wts = _route(t, logits)
        y = _moe_call(t, l, n_blocks, block_e, block_cnt, tok, dst, wts, hp, exp_w1, exp_w3, exp_w2)
    out = _final_call(t, xf, y, norm_final.reshape(1, -1))
    return out.reshape(b, s, d)
```

```python
import functools

import jax
import jax.numpy as jnp
from jax import lax
from jax.experimental import pallas as pl
from jax.experimental.pallas import tpu as pltpu

D_MODEL = 2048
D_A = D_MODEL // 4
D_B = 3 * D_MODEL // 8
D_C = D_MODEL - D_A - D_B
POOL_WINDOWS = (2, 4, 8, 16)
POOL_GROUP_DIM = D_A // len(POOL_WINDOWS)
CHUNK = 128
SGU_HEAD_DIM = 128
SGU_HEADS = D_B // SGU_HEAD_DIM
CONV_WIDTH = 31
N_IN = D_A + 2 * D_B + 2 * D_C
N_GROUPS = 4
E_PER_GROUP = 8
N_EXPERTS = N_GROUPS * E_PER_GROUP
TOP_K = 2
D_EXPERT = D_MODEL // 4
EPS = 1e-6

LANES = 128
SUBLANES = 8
POOL_HALO = 16
CONV_HALO = 32
CONV_ROWS = 64
CONV_PITCH = 2
TM_MIX = 256
TM_OUT = 256
BM = 256
TOK_TILE_ROWS = D_MODEL // 2 // LANES
ROUTER_COLS = LANES
REC_EXPERT = 0
REC_WEIGHT = TOP_K
REC_RANK = 2 * TOP_K
VMEM_LIMIT = 56 * 1024 * 1024

_F32 = jnp.float32
_BF16 = jnp.bfloat16


def _resident(shape):
    nd = len(shape)
    return pl.BlockSpec(shape, lambda *_: (0,) * nd, pipeline_mode=pl.Buffered(1))


def _rms_scale(x):
    return lax.rsqrt(jnp.mean(x * x, axis=-1, keepdims=True) + EPS)


def _layer_norm(x, g, b):
    mu = jnp.mean(x, axis=-1, keepdims=True)
    xc = x - mu
    return xc * lax.rsqrt(jnp.mean(xc * xc, axis=-1, keepdims=True) + EPS) * g + b


def _gelu_tanh(x):
    return 0.5 * x * (1.0 + jnp.tanh(0.7978845608028654 * (x + 0.044715 * (x * x * x))))


def _sigmoid(x):
    return 1.0 / (1.0 + jnp.exp(-x))


def _start_expert_rows(y_hbm, dest_ref, ybuf, sem, slot):
    tm = ybuf.shape[2]
    for j in range(tm):
        for k in range(TOP_K):
            pltpu.make_async_copy(y_hbm.at[pl.ds(dest_ref[0, 0, TOP_K * j + k], 1)],
                                  ybuf.at[slot, k, pl.ds(j, 1)], sem.at[slot]).start()


def _wait_expert_rows(ybuf, sem, slot):
    pltpu.make_async_copy(ybuf.at[slot], ybuf.at[slot], sem.at[slot]).wait()


def _combined_input(x_ref, rec_ref, y_hbm, dest_ref, dest_next_ref, ybuf, sem):
    step = pl.program_id(0)
    slot = lax.rem(step, 2)

    @pl.when(step == 0)
    def _():
        _start_expert_rows(y_hbm, dest_ref, ybuf, sem, 0)

    @pl.when(step < pl.num_programs(0) - 1)
    def _():
        _start_expert_rows(y_hbm, dest_next_ref, ybuf, sem, 1 - slot)

    _wait_expert_rows(ybuf, sem, slot)
    rec = rec_ref[...]
    moe = (rec[:, REC_WEIGHT:REC_WEIGHT + 1] * ybuf[slot, 0]
           + rec[:, REC_WEIGHT + 1:REC_WEIGHT + 2] * ybuf[slot, 1])
    return x_ref[...] + moe


def _combine_specs(tm, n_tiles):
    row = lambda i: (i, 0)
    in_specs = [
        pl.BlockSpec((tm, D_MODEL), row),
        pl.BlockSpec((tm, ROUTER_COLS), row),
        pl.BlockSpec(memory_space=pl.ANY),
        pl.BlockSpec((1, 1, TOP_K * tm), lambda i: (i, 0, 0), memory_space=pltpu.SMEM),
        pl.BlockSpec((1, 1, TOP_K * tm), lambda i: (jnp.minimum(i + 1, n_tiles - 1), 0, 0), memory_space=pltpu.SMEM),
    ]
    scratch = [pltpu.VMEM((2, TOP_K, tm, D_MODEL), _F32), pltpu.SemaphoreType.DMA((2,))]
    return in_specs, scratch


def _mixer_kernel(combine, tiles_per_seq, *refs):
    if combine:
        x_ref, rec_ref, y_hbm, dest_ref, dest_next_ref = refs[:5]
        refs = refs[5:]
    else:
        x_ref = refs[0]
        refs = refs[1:]
    (g_ref, win_ref, poolw_ref, pscale_ref, sg_ref, sb_ref, sw_ref, sbias_ref,
     dww_ref, dwb_ref, cg_ref, cb_ref, pw_ref) = refs[:13]
    refs = refs[13:]
    if combine:
        ycat_ref, xo_ref = refs[:2]
        h_s, xa_s, z_s, cv_s, ybuf, ysem = refs[2:]
    else:
        ycat_ref = refs[0]
        h_s, xa_s, z_s, cv_s = refs[1:]
    tm = ycat_ref.shape[0]

    tile = pl.program_id(0) % tiles_per_seq
    first = tile == 0

    if combine:
        x = _combined_input(x_ref, rec_ref, y_hbm, dest_ref, dest_next_ref, ybuf, ysem)
        xo_ref[...] = x
    else:
        x = x_ref[...]
    n_ct = D_C // LANES
    halo_src = pl.ds(CONV_PITCH * tm, CONV_HALO, stride=CONV_PITCH)
    halo_dst = pl.ds(0, CONV_HALO, stride=CONV_PITCH)

    @pl.when(first)
    def _():
        xa_s[0:POOL_HALO, :] = jnp.zeros((POOL_HALO, D_A), _F32)
        for c in range(n_ct):
            z_s[c, halo_dst, :] = jnp.zeros((CONV_HALO, LANES), _F32)

    @pl.when(jnp.logical_not(first))
    def _():
        xa_s[0:POOL_HALO, :] = xa_s[tm:tm + POOL_HALO, :]
        for c in range(n_ct):
            z_s[c, halo_dst, :] = z_s[c, halo_src, :]

    h_s[...] = (x * _rms_scale(x) * g_ref[...]).astype(_BF16)

    pc = jnp.dot(h_s[...], win_ref[:, D_A + 2 * D_B:N_IN], preferred_element_type=_F32)
    zc = pc[:, :D_C] * _sigmoid(pc[:, D_C:])
    for c in range(n_ct):
        z_s[c, pl.ds(CONV_PITCH * CONV_HALO, tm, stride=CONV_PITCH), :] = zc[:, c * LANES:(c + 1) * LANES]
    for r0 in range(0, tm, CONV_ROWS):
        for c in range(n_ct):
            lanes = slice(c * LANES, (c + 1) * LANES)
            acc = jnp.zeros((CONV_ROWS, LANES), _F32)
            for k in range(CONV_WIDTH):
                off = r0 + CONV_HALO - (CONV_WIDTH - 1) + k
                acc = acc + dww_ref[k:k + 1, lanes] * z_s[c, pl.ds(CONV_PITCH * off, CONV_ROWS, stride=CONV_PITCH), :]
            cv_s[r0:r0 + CONV_ROWS, lanes] = acc + dwb_ref[:, lanes]

    xa_s[POOL_HALO:POOL_HALO + tm, :] = jnp.dot(h_s[...], win_ref[:, 0:D_A], preferred_element_type=_F32)
    pos = tile * tm + lax.broadcasted_iota(jnp.int32, (tm, 1), 0)
    for gi, w in enumerate(POOL_WINDOWS):
        cols = slice(gi * POOL_GROUP_DIM, (gi + 1) * POOL_GROUP_DIM)
        xg = xa_s[POOL_HALO:POOL_HALO + tm, cols]
        acc = xg
        for k in range(1, w):
            acc = acc + xa_s[POOL_HALO - k:POOL_HALO - k + tm, cols]
        cnt = jnp.minimum(pos + 1, w).astype(_F32)
        d = acc / cnt - xg
        yg = jnp.dot(d.astype(_BF16), poolw_ref[gi], preferred_element_type=_F32) * pscale_ref[:, cols]
        ycat_ref[:, cols] = yg.astype(_BF16)

    pb = jnp.dot(h_s[...], win_ref[:, D_A:D_A + 2 * D_B], preferred_element_type=_F32)
    zb = _gelu_tanh(pb)
    u = zb[:, :D_B]
    vb = _layer_norm(zb[:, D_B:], sg_ref[...], sb_ref[...]).astype(_BF16)
    r_i = lax.broadcasted_iota(jnp.int32, (CHUNK, CHUNK), 0)
    c_i = lax.broadcasted_iota(jnp.int32, (CHUNK, CHUNK), 1)
    causal = r_i >= c_i
    for hd in range(SGU_HEADS):
        hc = slice(hd * SGU_HEAD_DIM, (hd + 1) * SGU_HEAD_DIM)
        wm = jnp.where(causal, sw_ref[hd], 0.0).astype(_BF16)
        for c in range(tm // CHUNK):
            rows = slice(c * CHUNK, (c + 1) * CHUNK)
            gate = jnp.dot(wm, vb[rows, hc], preferred_element_type=_F32) + sbias_ref[:, hc]
            ycat_ref[rows, D_A + hd * SGU_HEAD_DIM:D_A + (hd + 1) * SGU_HEAD_DIM] = (u[rows, hc] * gate).astype(_BF16)

    zn = _layer_norm(cv_s[...], cg_ref[...], cb_ref[...])
    zs = (zn * _sigmoid(zn)).astype(_BF16)
    ycat_ref[:, D_A + D_B:D_MODEL] = jnp.dot(zs, pw_ref[...], preferred_element_type=_F32).astype(_BF16)


def _mixer_call(t, seq, x, moe, g, win, poolw, pscale, sg, sb, sw, sbias, dww, dwb, cg, cb, pw):
    tm = TM_MIX
    n_tiles = t // tm
    row = lambda i: (i, 0)
    combine = moe is not None
    scratch = [
        pltpu.VMEM((tm, D_MODEL), _BF16),
        pltpu.VMEM((POOL_HALO + tm, D_A), _F32),
        pltpu.VMEM((D_C // LANES, CONV_PITCH * (CONV_HALO + tm), LANES), _F32),
        pltpu.VMEM((tm, D_C), _F32),
    ]
    if combine:
        rec, ys, dest = moe
        in_specs, extra = _combine_specs(tm, n_tiles)
        scratch += extra
        args = [x, rec, ys, dest, dest]
    else:
        in_specs = [pl.BlockSpec((tm, D_MODEL), row)]
        args = [x]
    weights = [g, win, poolw, pscale, sg, sb, sw, sbias, dww, dwb, cg, cb, pw]
    in_specs += [_resident(w.shape) for w in weights]
    args += weights
    out_shape = [jax.ShapeDtypeStruct((t, D_MODEL), _BF16)]
    out_specs = [pl.BlockSpec((tm, D_MODEL), row)]
    if combine:
        out_shape.append(jax.ShapeDtypeStruct((t, D_MODEL), _F32))
        out_specs.append(pl.BlockSpec((tm, D_MODEL), row))
    return pl.pallas_call(
        functools.partial(_mixer_kernel, combine, seq // tm),
        grid=(n_tiles,),
        in_specs=in_specs,
        out_specs=out_specs,
        out_shape=out_shape,
        scratch_shapes=scratch,
        compiler_params=pltpu.CompilerParams(dimension_semantics=("arbitrary",), vmem_limit_bytes=VMEM_LIMIT),
        name="mixer_combine" if combine else "mixer",
    )(*args)


def _pack_bf16_pairs(hb):
    u = lax.bitcast_convert_type(hb.astype(_F32), jnp.uint32)
    c = hb.shape[1] // 2
    return (u[:, c:] & jnp.uint32(0xFFFF0000)) | (u[:, :c] >> 16)


def _unpack_bf16_pairs(p):
    lo = lax.bitcast_convert_type(p << 16, _F32).astype(_BF16)
    hi = lax.bitcast_convert_type(p & jnp.uint32(0xFFFF0000), _F32).astype(_BF16)
    return lo, hi


def _route_rows(logits, seen):
    rows = logits.shape[0]
    lane = lax.broadcasted_iota(jnp.int32, logits.shape, 1)
    neg = jnp.float32(-jnp.inf)
    big = jnp.int32(ROUTER_COLS)

    def first_max(v):
        m = jnp.max(v, axis=-1, keepdims=True)
        return m, jnp.min(jnp.where(v == m, lane, big), axis=-1, keepdims=True)

    lg = jnp.where(lane < N_GROUPS, logits, neg)
    mg, g_idx = first_max(lg)
    g_w = 1.0 / jnp.sum(jnp.exp(lg - mg), axis=-1, keepdims=True)
    e_lo = N_GROUPS + g_idx * E_PER_GROUP
    le = jnp.where((lane >= e_lo) & (lane < e_lo + E_PER_GROUP), logits, neg)
    m1, i1 = first_max(le)
    m2, i2 = first_max(jnp.where(lane == i1, neg, le))
    p2 = jnp.exp(m2 - m1)
    w1 = g_w / (1.0 + p2)
    w2 = g_w * p2 / (1.0 + p2)
    e1 = i1 - N_GROUPS
    e2 = i2 - N_GROUPS

    hit1 = lane == e1
    hit2 = lane == e2
    onehot = jnp.where(hit1, 1.0, 0.0) + jnp.where(hit2, 1.0, 0.0)
    r_i = lax.broadcasted_iota(jnp.int32, (rows, rows), 0)
    c_i = lax.broadcasted_iota(jnp.int32, (rows, rows), 1)
    earlier = jnp.where(r_i > c_i, 1.0, 0.0).astype(_BF16)
    before = seen + jnp.dot(earlier, onehot.astype(_BF16), preferred_element_type=_F32)
    rank1 = jnp.sum(jnp.where(hit1, before, 0.0), axis=-1, keepdims=True)
    rank2 = jnp.sum(jnp.where(hit2, before, 0.0), axis=-1, keepdims=True)
    seen = seen + jnp.sum(onehot, axis=0, keepdims=True)

    rec = jnp.zeros(logits.shape, _F32)
    for k, v in ((REC_EXPERT, e1.astype(_F32)), (REC_EXPERT + 1, e2.astype(_F32)), (REC_WEIGHT, w1),
                 (REC_WEIGHT + 1, w2), (REC_RANK, rank1), (REC_RANK + 1, rank2)):
        rec = jnp.where(lane == k, v, rec)
    return rec, seen


def _out_kernel(ycat_ref, x_ref, wout_ref, g_ref, wr_ref, br_ref, x1_ref, hp_ref, rec_ref, cnt_ref, seen_s):
    @pl.when(pl.program_id(0) == 0)
    def _():
        seen_s[...] = jnp.zeros(seen_s.shape, _F32)

    x1 = x_ref[...] + jnp.dot(ycat_ref[...], wout_ref[...], preferred_element_type=_F32)
    x1_ref[...] = x1
    hb = (x1 * _rms_scale(x1) * g_ref[...]).astype(_BF16)
    logits = jnp.dot(hb, wr_ref[...], preferred_element_type=_F32) + br_ref[...]
    rec, seen = _route_rows(logits, seen_s[...])
    rec_ref[...] = rec
    seen_s[...] = seen
    cnt_ref[...] = jnp.broadcast_to(seen, cnt_ref.shape)
    packed = _pack_bf16_pairs(hb)
    tm = packed.shape[0]
    for s in range(TOK_TILE_ROWS):
        hp_ref[pl.ds(s, tm, stride=TOK_TILE_ROWS), :] = packed[:, s * LANES:(s + 1) * LANES]


def _out_call(t, ycat, x, wout, g, wr, br):
    tm = TM_OUT
    row = lambda i: (i, 0)
    return pl.pallas_call(
        _out_kernel,
        grid=(t // tm,),
        in_specs=[pl.BlockSpec((tm, D_MODEL), row), pl.BlockSpec((tm, D_MODEL), row),
                  _resident(wout.shape), _resident(g.shape), _resident(wr.shape), _resident(br.shape)],
        out_specs=[pl.BlockSpec((tm, D_MODEL), row), pl.BlockSpec((tm * TOK_TILE_ROWS, LANES), row),
                   pl.BlockSpec((tm, ROUTER_COLS), row), pl.BlockSpec((SUBLANES, ROUTER_COLS), lambda i: (0, 0))],
        out_shape=[jax.ShapeDtypeStruct((t, D_MODEL), _F32),
                   jax.ShapeDtypeStruct((t * TOK_TILE_ROWS, LANES), jnp.uint32),
                   jax.ShapeDtypeStruct((t, ROUTER_COLS), _F32),
                   jax.ShapeDtypeStruct((SUBLANES, ROUTER_COLS), _F32)],
        scratch_shapes=[pltpu.VMEM((1, ROUTER_COLS), _F32)],
        compiler_params=pltpu.CompilerParams(dimension_semantics=("arbitrary",), vmem_limit_bytes=VMEM_LIMIT),
        name="out_proj",
    )(ycat, x, wout, g, wr, br)


def _dispatch_kernel(pad_start_ref, pad_rows_ref, dest_ref, hp_hbm, xs_hbm, zero_s, sem, zsem):
    step = pl.program_id(0)
    last = pl.num_programs(0) - 1
    tm = dest_ref.shape[2] // TOP_K
    t0 = step * tm

    def wait_tile():
        view = xs_hbm.at[pl.ds(0, TOP_K * tm)]
        pltpu.make_async_copy(view, view, sem).wait()

    @pl.when(step == 0)
    def _():
        zero_s[...] = jnp.zeros(zero_s.shape, zero_s.dtype)
        for e in range(pad_rows_ref.shape[0]):
            n = pad_rows_ref[e]

            @pl.when(n > 0)
            def _():
                pltpu.make_async_copy(zero_s.at[pl.ds(0, n)], xs_hbm.at[pl.ds(pad_start_ref[e], n)], zsem).start()

        for e in range(pad_rows_ref.shape[0]):
            n = pad_rows_ref[e]

            @pl.when(n > 0)
            def _():
                pltpu.make_async_copy(zero_s.at[pl.ds(0, n)], xs_hbm.at[pl.ds(pad_start_ref[e], n)], zsem).wait()

    for j in range(tm):
        for k in range(TOP_K):
            pltpu.make_async_copy(hp_hbm.at[t0 + j], xs_hbm.at[dest_ref[0, 0, TOP_K * j + k]], sem).start()

    @pl.when(step > 0)
    def _():
        wait_tile()

    @pl.when(step == last)
    def _():
        wait_tile()


def _dispatch_call(t, p_len, pad_start, pad_rows, dest, hp):
    tm = TM_OUT
    grid_spec = pltpu.PrefetchScalarGridSpec(
        num_scalar_prefetch=2,
        grid=(t // tm,),
        in_specs=[pl.BlockSpec((1, 1, TOP_K * tm), lambda i, *_: (i, 0, 0), memory_space=pltpu.SMEM),
                  pl.BlockSpec(memory_space=pl.ANY)],
        out_specs=pl.BlockSpec(memory_space=pl.ANY),
        scratch_shapes=[pltpu.VMEM((BM, TOK_TILE_ROWS, LANES), jnp.uint32), pltpu.SemaphoreType.DMA(()),
                        pltpu.SemaphoreType.DMA(())],
    )
    return pl.pallas_call(
        _dispatch_kernel,
        grid_spec=grid_spec,
        out_shape=jax.ShapeDtypeStruct((p_len, TOK_TILE_ROWS, LANES), jnp.uint32),
        compiler_params=pltpu.CompilerParams(dimension_semantics=("arbitrary",), vmem_limit_bytes=VMEM_LIMIT),
        name="dispatch",
    )(pad_start, pad_rows, dest, hp.reshape(t, TOK_TILE_ROWS, LANES))


def _expert_kernel(be_ref, cnt_ref, nv_ref, xs_ref, w1_ref, w3_ref, w2_ref, y_ref):
    del be_ref, nv_ref
    half = D_MODEL // 2
    rows = cnt_ref[pl.program_id(0)]

    @pl.when(rows == 0)
    def _():
        y_ref[...] = jnp.zeros(y_ref.shape, y_ref.dtype)

    @pl.when(rows > 0)
    def _():
        packed = jnp.concatenate([xs_ref[pl.ds(s, BM, stride=TOK_TILE_ROWS), :] for s in range(TOK_TILE_ROWS)], axis=1)
        lo, hi = _unpack_bf16_pairs(packed)
        a1 = (jnp.dot(lo, w1_ref[0, 0:half, :].astype(_BF16), preferred_element_type=_F32)
              + jnp.dot(hi, w1_ref[0, half:D_MODEL, :].astype(_BF16), preferred_element_type=_F32))
        a3 = (jnp.dot(lo, w3_ref[0, 0:half, :].astype(_BF16), preferred_element_type=_F32)
              + jnp.dot(hi, w3_ref[0, half:D_MODEL, :].astype(_BF16), preferred_element_type=_F32))
        hid = (a1 * _sigmoid(a1) * a3).astype(_BF16)
        y_ref[...] = jnp.dot(hid, w2_ref[0].astype(_BF16), preferred_element_type=_F32)


def _expert_call(layer, n_blocks, block_e, block_cnt, n_valid, xs, w1, w3, w2):
    blk = lambda i, be, cnt, nv: (jnp.minimum(i, nv[0] - 1), 0)
    wmap = lambda i, be, cnt, nv: (layer, be[i], 0, 0)
    grid_spec = pltpu.PrefetchScalarGridSpec(
        num_scalar_prefetch=3,
        grid=(n_blocks,),
        in_specs=[
            pl.BlockSpec((BM * TOK_TILE_ROWS, LANES), blk),
            pl.BlockSpec((None, 1, D_MODEL, D_EXPERT), wmap),
            pl.BlockSpec((None, 1, D_MODEL, D_EXPERT), wmap),
            pl.BlockSpec((None, 1, D_EXPERT, D_MODEL), wmap),
        ],
        out_specs=pl.BlockSpec((BM, D_MODEL), lambda i, *_: (i, 0)),
    )
    return pl.pallas_call(
        _expert_kernel,
        grid_spec=grid_spec,
        out_shape=jax.ShapeDtypeStruct((n_blocks * BM, D_MODEL), _F32),
        compiler_params=pltpu.CompilerParams(dimension_semantics=("arbitrary",), vmem_limit_bytes=VMEM_LIMIT),
        name="experts",
    )(block_e, block_cnt, n_valid, xs.reshape(n_blocks * BM * TOK_TILE_ROWS, LANES), w1, w3, w2)


def _slot_tables(t, rec, counts_f):
    n_assign = t * TOP_K
    n_blocks = n_assign // BM + N_EXPERTS
    counts = counts_f[0, :N_EXPERTS].astype(jnp.int32)
    padded = ((counts + BM - 1) // BM) * BM
    pend = jnp.cumsum(padded)
    pstart = pend - padded
    starts = jnp.arange(n_blocks, dtype=jnp.int32) * BM
    block_e = jnp.minimum(jnp.sum(pend[None, :] <= starts[:, None], axis=1), N_EXPERTS - 1).astype(jnp.int32)
    seg_end = (pstart + counts)[block_e]
    block_cnt = jnp.where(starts < pend[-1], jnp.clip(seg_end - starts, 0, BM), 0).astype(jnp.int32)
    n_valid = jnp.maximum(pend[-1] // BM, 1).astype(jnp.int32).reshape(1)
    block_e = jnp.where(starts < pend[-1], block_e, block_e[n_valid[0] - 1])
    expert = rec[:, REC_EXPERT:REC_EXPERT + TOP_K].astype(jnp.int32)
    rank = rec[:, REC_RANK:REC_RANK + TOP_K].astype(jnp.int32)
    dest = (pstart[expert] + rank).reshape(t // TM_OUT, 1, TOP_K * TM_OUT)
    tail = pend[-1] + jnp.arange(N_EXPERTS, dtype=jnp.int32) * BM
    pad_start = jnp.concatenate([pstart + counts, tail])
    pad_rows = jnp.concatenate([padded - counts, jnp.where(tail < n_blocks * BM, BM, 0)])
    return n_blocks, block_e, block_cnt, n_valid, pad_start, pad_rows.astype(jnp.int32), dest


def _final_kernel(x_ref, rec_ref, y_hbm, dest_ref, dest_next_ref, g_ref, o_ref, ybuf, ysem):
    x = _combined_input(x_ref, rec_ref, y_hbm, dest_ref, dest_next_ref, ybuf, ysem)
    o_ref[...] = x * _rms_scale(x) * g_ref[...]


def _final_call(t, x, moe, g):
    tm = TM_OUT
    n_tiles = t // tm
    rec, ys, dest = moe
    in_specs, scratch = _combine_specs(tm, n_tiles)
    return pl.pallas_call(
        _final_kernel,
        grid=(n_tiles,),
        in_specs=in_specs + [_resident(g.shape)],
        out_specs=pl.BlockSpec((tm, D_MODEL), lambda i: (i, 0)),
        out_shape=jax.ShapeDtypeStruct((t, D_MODEL), _F32),
        scratch_shapes=scratch,
        compiler_params=pltpu.CompilerParams(dimension_semantics=("arbitrary",), vmem_limit_bytes=VMEM_LIMIT),
        name="final_norm",
    )(x, rec, ys, dest, dest, g)


def kernel(x, norm_mix, w_in, pool_w, pool_scale, sgu_ln_g, sgu_ln_b, sgu_w, sgu_b, conv_dw_w, conv_dw_b,
           conv_ln_g, conv_ln_b, conv_pw, w_out, norm_ffn, router_g_w, router_g_b, router_e_w, router_e_b,
           exp_w1, exp_w3, exp_w2, norm_final):
    b, s, d = x.shape
    depth = w_in.shape[0]
    assert d == D_MODEL and s % TM_MIX == 0 and (b * s) % TM_OUT == 0 and TM_MIX % CHUNK == 0 and TM_MIX == TM_OUT
    t = b * s
    xf = x.reshape(t, d)
    moe = None
    for l in range(depth):
        row2 = lambda v: v[l].reshape(1, -1)
        sbias = jnp.repeat(sgu_b[l].T, SGU_HEAD_DIM, axis=1)
        outs = _mixer_call(
            t, s, xf, moe, row2(norm_mix), w_in[l].astype(_BF16), pool_w[l].astype(_BF16),
            row2(pool_scale), row2(sgu_ln_g), row2(sgu_ln_b), sgu_w[l], sbias, conv_dw_w[l], row2(conv_dw_b),
            row2(conv_ln_g), row2(conv_ln_b), conv_pw[l].astype(_BF16))
        if moe is not None:
            ycat, xf = outs
        else:
            (ycat,) = outs
        pad = ROUTER_COLS - N_GROUPS - N_EXPERTS
        wr = jnp.concatenate([router_g_w[l], router_e_w[l], jnp.zeros((d, pad), _F32)], axis=1).astype(_BF16)
        br = jnp.concatenate([router_g_b[l], router_e_b[l], jnp.zeros((pad,), _F32)]).reshape(1, ROUTER_COLS)
        xf, hp, rec, counts = _out_call(t, ycat, xf, w_out[l].astype(_BF16), row2(norm_ffn), wr, br)
        n_blocks, block_e, block_cnt, n_valid, pad_start, pad_rows, dest = _slot_tables(t, rec, counts)
        xs = _dispatch_call(t, n_blocks * BM, pad_start, pad_rows, dest, hp)
        ys = _expert_call(l, n_blocks, block_e, block_cnt, n_valid, xs, exp_w1, exp_w3, exp_w2)
        moe = (rec, ys, dest)
    out = _final_call(t, xf, moe, norm_final.reshape(1, -1))
    return out.reshape(b, s, d)
```

```python
import functools

import jax
import jax.numpy as jnp
from jax import lax
from jax.experimental import pallas as pl
from jax.experimental.pallas import tpu as pltpu

D_MODEL = 2048
D_A = D_MODEL // 4
D_B = 3 * D_MODEL // 8
D_C = D_MODEL - D_A - D_B
POOL_WINDOWS = (2, 4, 8, 16)
POOL_GROUP_DIM = D_A // len(POOL_WINDOWS)
CHUNK = 128
SGU_HEAD_DIM = 128
SGU_HEADS = D_B // SGU_HEAD_DIM
CONV_WIDTH = 31
N_IN = D_A + 2 * D_B + 2 * D_C
N_GROUPS = 4
E_PER_GROUP = 8
N_EXPERTS = N_GROUPS * E_PER_GROUP
TOP_K = 2
D_EXPERT = D_MODEL // 4
EPS = 1e-6

LANES = 128
SUBLANES = 8
POOL_HALO = 16
CONV_HALO = 32
CONV_ROWS = 64
CONV_PITCH = 2
TM_MIX = 256
TM_OUT = 256
BM = 256
TOK_TILE_ROWS = D_MODEL // 2 // LANES
ROUTER_COLS = LANES
REC_EXPERT = 0
REC_WEIGHT = TOP_K
REC_RANK = 2 * TOP_K
VMEM_LIMIT = 56 * 1024 * 1024

_F32 = jnp.float32
_BF16 = jnp.bfloat16


def _resident(shape):
    nd = len(shape)
    return pl.BlockSpec(shape, lambda *_: (0,) * nd, pipeline_mode=pl.Buffered(1))


def _rms_scale(x):
    return lax.rsqrt(jnp.mean(x * x, axis=-1, keepdims=True) + EPS)


def _layer_norm(x, g, b):
    mu = jnp.mean(x, axis=-1, keepdims=True)
    xc = x - mu
    return xc * lax.rsqrt(jnp.mean(xc * xc, axis=-1, keepdims=True) + EPS) * g + b


def _gelu_tanh(x):
    return 0.5 * x * (1.0 + jnp.tanh(0.7978845608028654 * (x + 0.044715 * (x * x * x))))


def _sigmoid(x):
    return 1.0 / (1.0 + jnp.exp(-x))


def _start_expert_rows(y_hbm, dest_ref, ybuf, sem, slot):
    tm = ybuf.shape[2]
    for j in range(tm):
        for k in range(TOP_K):
            pltpu.make_async_copy(y_hbm.at[pl.ds(dest_ref[0, 0, TOP_K * j + k], 1)],
                                  ybuf.at[slot, k, pl.ds(j, 1)], sem.at[slot]).start()


def _wait_expert_rows(ybuf, sem, slot):
    pltpu.make_async_copy(ybuf.at[slot], ybuf.at[slot], sem.at[slot]).wait()


def _combined_input(x_ref, rec_ref, y_hbm, dest_ref, dest_next_ref, ybuf, sem):
    step = pl.program_id(0)
    slot = lax.rem(step, 2)

    @pl.when(step == 0)
    def _():
        _start_expert_rows(y_hbm, dest_ref, ybuf, sem, 0)

    for s in range(2):
        @pl.when((step < pl.num_programs(0) - 1) & (slot == s))
        def _(s=s):
            _start_expert_rows(y_hbm, dest_next_ref, ybuf, sem, 1 - s)

    _wait_expert_rows(ybuf, sem, slot)
    rec = rec_ref[...]
    moe = (rec[:, REC_WEIGHT:REC_WEIGHT + 1] * ybuf[slot, 0]
           + rec[:, REC_WEIGHT + 1:REC_WEIGHT + 2] * ybuf[slot, 1])
    return x_ref[...] + moe


def _combine_specs(tm, n_tiles):
    row = lambda i: (i, 0)
    in_specs = [
        pl.BlockSpec((tm, D_MODEL), row),
        pl.BlockSpec((tm, ROUTER_COLS), row),
        pl.BlockSpec(memory_space=pl.ANY),
        pl.BlockSpec((1, 1, TOP_K * tm), lambda i: (i, 0, 0), memory_space=pltpu.SMEM),
        pl.BlockSpec((1, 1, TOP_K * tm), lambda i: (jnp.minimum(i + 1, n_tiles - 1), 0, 0), memory_space=pltpu.SMEM),
    ]
    scratch = [pltpu.VMEM((2, TOP_K, tm, D_MODEL), _F32), pltpu.SemaphoreType.DMA((2,))]
    return in_specs, scratch


def _mixer_kernel(combine, tiles_per_seq, *refs):
    if combine:
        x_ref, rec_ref, y_hbm, dest_ref, dest_next_ref = refs[:5]
        refs = refs[5:]
    else:
        x_ref = refs[0]
        refs = refs[1:]
    (g_ref, win_ref, poolw_ref, pscale_ref, sg_ref, sb_ref, sw_ref, sbias_ref,
     dww_ref, dwb_ref, cg_ref, cb_ref, pw_ref) = refs[:13]
    refs = refs[13:]
    if combine:
        ycat_ref, xo_ref = refs[:2]
        h_s, xa_s, z_s, cv_s, ybuf, ysem = refs[2:]
    else:
        ycat_ref = refs[0]
        h_s, xa_s, z_s, cv_s = refs[1:]
    tm = ycat_ref.shape[0]

    tile = pl.program_id(0) % tiles_per_seq
    first = tile == 0

    if combine:
        x = _combined_input(x_ref, rec_ref, y_hbm, dest_ref, dest_next_ref, ybuf, ysem)
        xo_ref[...] = x
    else:
        x = x_ref[...]
    n_ct = D_C // LANES
    halo_src = pl.ds(CONV_PITCH * tm, CONV_HALO, stride=CONV_PITCH)
    halo_dst = pl.ds(0, CONV_HALO, stride=CONV_PITCH)

    @pl.when(first)
    def _():
        xa_s[0:POOL_HALO, :] = jnp.zeros((POOL_HALO, D_A), _F32)
        for c in range(n_ct):
            z_s[c, halo_dst, :] = jnp.zeros((CONV_HALO, LANES), _F32)

    @pl.when(jnp.logical_not(first))
    def _():
        xa_s[0:POOL_HALO, :] = xa_s[tm:tm + POOL_HALO, :]
        for c in range(n_ct):
            z_s[c, halo_dst, :] = z_s[c, halo_src, :]

    h_s[...] = (x * _rms_scale(x) * g_ref[...]).astype(_BF16)

    pc = jnp.dot(h_s[...], win_ref[:, D_A + 2 * D_B:N_IN], preferred_element_type=_F32)
    zc = pc[:, :D_C] * _sigmoid(pc[:, D_C:])
    for c in range(n_ct):
        z_s[c, pl.ds(CONV_PITCH * CONV_HALO, tm, stride=CONV_PITCH), :] = zc[:, c * LANES:(c + 1) * LANES]
    for r0 in range(0, tm, CONV_ROWS):
        for c in range(n_ct):
            lanes = slice(c * LANES, (c + 1) * LANES)
            acc = jnp.zeros((CONV_ROWS, LANES), _F32)
            for k in range(CONV_WIDTH):
                off = r0 + CONV_HALO - (CONV_WIDTH - 1) + k
                acc = acc + dww_ref[k:k + 1, lanes] * z_s[c, pl.ds(CONV_PITCH * off, CONV_ROWS, stride=CONV_PITCH), :]
            cv_s[r0:r0 + CONV_ROWS, lanes] = acc + dwb_ref[:, lanes]

    xa_s[POOL_HALO:POOL_HALO + tm, :] = jnp.dot(h_s[...], win_ref[:, 0:D_A], preferred_element_type=_F32)
    pos = tile * tm + lax.broadcasted_iota(jnp.int32, (tm, 1), 0)
    for gi, w in enumerate(POOL_WINDOWS):
        cols = slice(gi * POOL_GROUP_DIM, (gi + 1) * POOL_GROUP_DIM)
        xg = xa_s[POOL_HALO:POOL_HALO + tm, cols]
        acc = xg
        for k in range(1, w):
            acc = acc + xa_s[POOL_HALO - k:POOL_HALO - k + tm, cols]
        cnt = jnp.minimum(pos + 1, w).astype(_F32)
        d = acc / cnt - xg
        yg = jnp.dot(d.astype(_BF16), poolw_ref[gi], preferred_element_type=_F32) * pscale_ref[:, cols]
        ycat_ref[:, cols] = yg.astype(_BF16)

    pb = jnp.dot(h_s[...], win_ref[:, D_A:D_A + 2 * D_B], preferred_element_type=_F32)
    zb = _gelu_tanh(pb)
    u = zb[:, :D_B]
    vb = _layer_norm(zb[:, D_B:], sg_ref[...], sb_ref[...]).astype(_BF16)
    r_i = lax.broadcasted_iota(jnp.int32, (CHUNK, CHUNK), 0)
    c_i = lax.broadcasted_iota(jnp.int32, (CHUNK, CHUNK), 1)
    causal = r_i >= c_i
    for hd in range(SGU_HEADS):
        hc = slice(hd * SGU_HEAD_DIM, (hd + 1) * SGU_HEAD_DIM)
        wm = jnp.where(causal, sw_ref[hd], 0.0).astype(_BF16)
        for c in range(tm // CHUNK):
            rows = slice(c * CHUNK, (c + 1) * CHUNK)
            gate = jnp.dot(wm, vb[rows, hc], preferred_element_type=_F32) + sbias_ref[:, hc]
            ycat_ref[rows, D_A + hd * SGU_HEAD_DIM:D_A + (hd + 1) * SGU_HEAD_DIM] = (u[rows, hc] * gate).astype(_BF16)

    zn = _layer_norm(cv_s[...], cg_ref[...], cb_ref[...])
    zs = (zn * _sigmoid(zn)).astype(_BF16)
    ycat_ref[:, D_A + D_B:D_MODEL] = jnp.dot(zs, pw_ref[...], preferred_element_type=_F32).astype(_BF16)


def _mixer_call(t, seq, x, moe, g, win, poolw, pscale, sg, sb, sw, sbias, dww, dwb, cg, cb, pw):
    tm = TM_MIX
    n_tiles = t // tm
    row = lambda i: (i, 0)
    combine = moe is not None
    scratch = [
        pltpu.VMEM((tm, D_MODEL), _BF16),
        pltpu.VMEM((POOL_HALO + tm, D_A), _F32),
        pltpu.VMEM((D_C // LANES, CONV_PITCH * (CONV_HALO + tm), LANES), _F32),
        pltpu.VMEM((tm, D_C), _F32),
    ]
    if combine:
        rec, ys, dest = moe
        in_specs, extra = _combine_specs(tm, n_tiles)
        scratch += extra
        args = [x, rec, ys, dest, dest]
    else:
        in_specs = [pl.BlockSpec((tm, D_MODEL), row)]
        args = [x]
    weights = [g, win, poolw, pscale, sg, sb, sw, sbias, dww, dwb, cg, cb, pw]
    in_specs += [_resident(w.shape) for w in weights]
    args += weights
    out_shape = [jax.ShapeDtypeStruct((t, D_MODEL), _BF16)]
    out_specs = [pl.BlockSpec((tm, D_MODEL), row)]
    if combine:
        out_shape.append(jax.ShapeDtypeStruct((t, D_MODEL), _F32))
        out_specs.append(pl.BlockSpec((tm, D_MODEL), row))
    return pl.pallas_call(
        functools.partial(_mixer_kernel, combine, seq // tm),
        grid=(n_tiles,),
        in_specs=in_specs,
        out_specs=out_specs,
        out_shape=out_shape,
        scratch_shapes=scratch,
        compiler_params=pltpu.CompilerParams(dimension_semantics=("arbitrary",), vmem_limit_bytes=VMEM_LIMIT),
        name="mixer_combine" if combine else "mixer",
    )(*args)


def _pack_bf16_pairs(hb):
    u = lax.bitcast_convert_type(hb.astype(_F32), jnp.uint32)
    c = hb.shape[1] // 2
    return (u[:, c:] & jnp.uint32(0xFFFF0000)) | (u[:, :c] >> 16)


def _unpack_bf16_pairs(p):
    lo = lax.bitcast_convert_type(p << 16, _F32).astype(_BF16)
    hi = lax.bitcast_convert_type(p & jnp.uint32(0xFFFF0000), _F32).astype(_BF16)
    return lo, hi


def _route_rows(logits, seen):
    rows = logits.shape[0]
    lane = lax.broadcasted_iota(jnp.int32, logits.shape, 1)
    neg = jnp.float32(-jnp.inf)
    big = jnp.int32(ROUTER_COLS)

    def first_max(v):
        m = jnp.max(v, axis=-1, keepdims=True)
        return m, jnp.min(jnp.where(v == m, lane, big), axis=-1, keepdims=True)

    lg = jnp.where(lane < N_GROUPS, logits, neg)
    mg, g_idx = first_max(lg)
    g_w = 1.0 / jnp.sum(jnp.exp(lg - mg), axis=-1, keepdims=True)
    e_lo = N_GROUPS + g_idx * E_PER_GROUP
    le = jnp.where((lane >= e_lo) & (lane < e_lo + E_PER_GROUP), logits, neg)
    m1, i1 = first_max(le)
    m2, i2 = first_max(jnp.where(lane == i1, neg, le))
    p2 = jnp.exp(m2 - m1)
    w1 = g_w / (1.0 + p2)
    w2 = g_w * p2 / (1.0 + p2)
    e1 = i1 - N_GROUPS
    e2 = i2 - N_GROUPS

    hit1 = lane == e1
    hit2 = lane == e2
    onehot = jnp.where(hit1, 1.0, 0.0) + jnp.where(hit2, 1.0, 0.0)
    r_i = lax.broadcasted_iota(jnp.int32, (rows, rows), 0)
    c_i = lax.broadcasted_iota(jnp.int32, (rows, rows), 1)
    earlier = jnp.where(r_i > c_i, 1.0, 0.0).astype(_BF16)
    before = seen + jnp.dot(earlier, onehot.astype(_BF16), preferred_element_type=_F32)
    rank1 = jnp.sum(jnp.where(hit1, before, 0.0), axis=-1, keepdims=True)
    rank2 = jnp.sum(jnp.where(hit2, before, 0.0), axis=-1, keepdims=True)
    seen = seen + jnp.sum(onehot, axis=0, keepdims=True)

    rec = jnp.zeros(logits.shape, _F32)
    for k, v in ((REC_EXPERT, e1.astype(_F32)), (REC_EXPERT + 1, e2.astype(_F32)), (REC_WEIGHT, w1),
                 (REC_WEIGHT + 1, w2), (REC_RANK, rank1), (REC_RANK + 1, rank2)):
        rec = jnp.where(lane == k, v, rec)
    return rec, seen


def _out_kernel(ycat_ref, x_ref, wout_ref, g_ref, wr_ref, br_ref, x1_ref, hp_ref, rec_ref, cnt_ref, seen_s):
    @pl.when(pl.program_id(0) == 0)
    def _():
        seen_s[...] = jnp.zeros(seen_s.shape, _F32)

    x1 = x_ref[...] + jnp.dot(ycat_ref[...], wout_ref[...], preferred_element_type=_F32)
    x1_ref[...] = x1
    hb = (x1 * _rms_scale(x1) * g_ref[...]).astype(_BF16)
    logits = jnp.dot(hb, wr_ref[...], preferred_element_type=_F32) + br_ref[...]
    rec, seen = _route_rows(logits, seen_s[...])
    rec_ref[...] = rec
    seen_s[...] = seen
    cnt_ref[...] = jnp.broadcast_to(seen, cnt_ref.shape)
    packed = _pack_bf16_pairs(hb)
    tm = packed.shape[0]
    for s in range(TOK_TILE_ROWS):
        hp_ref[pl.ds(s, tm, stride=TOK_TILE_ROWS), :] = packed[:, s * LANES:(s + 1) * LANES]


def _out_call(t, ycat, x, wout, g, wr, br):
    tm = TM_OUT
    row = lambda i: (i, 0)
    return pl.pallas_call(
        _out_kernel,
        grid=(t // tm,),
        in_specs=[pl.BlockSpec((tm, D_MODEL), row), pl.BlockSpec((tm, D_MODEL), row),
                  _resident(wout.shape), _resident(g.shape), _resident(wr.shape), _resident(br.shape)],
        out_specs=[pl.BlockSpec((tm, D_MODEL), row), pl.BlockSpec((tm * TOK_TILE_ROWS, LANES), row),
                   pl.BlockSpec((tm, ROUTER_COLS), row), pl.BlockSpec((SUBLANES, ROUTER_COLS), lambda i: (0, 0))],
        out_shape=[jax.ShapeDtypeStruct((t, D_MODEL), _F32),
                   jax.ShapeDtypeStruct((t * TOK_TILE_ROWS, LANES), jnp.uint32),
                   jax.ShapeDtypeStruct((t, ROUTER_COLS), _F32),
                   jax.ShapeDtypeStruct((SUBLANES, ROUTER_COLS), _F32)],
        scratch_shapes=[pltpu.VMEM((1, ROUTER_COLS), _F32)],
        compiler_params=pltpu.CompilerParams(dimension_semantics=("arbitrary",), vmem_limit_bytes=VMEM_LIMIT),
        name="out_proj",
    )(ycat, x, wout, g, wr, br)


def _dispatch_kernel(pad_start_ref, pad_rows_ref, dest_ref, hp_hbm, xs_hbm, buf, zero_s, lsem, sem, zsem):
    step = pl.program_id(0)
    last = pl.num_programs(0) - 1
    tm = dest_ref.shape[2] // TOP_K
    slot = lax.rem(step, 2)

    def stage(tile, to_slot):
        return pltpu.make_async_copy(hp_hbm.at[pl.ds(tile * tm, tm)], buf.at[to_slot], lsem.at[to_slot])

    def wait_tile():
        view = xs_hbm.at[pl.ds(0, TOP_K * tm)]
        pltpu.make_async_copy(view, view, sem).wait()

    @pl.when(step == 0)
    def _():
        stage(0, 0).start()
        zero_s[...] = jnp.zeros(zero_s.shape, zero_s.dtype)
        for e in range(pad_rows_ref.shape[0]):
            n = pad_rows_ref[e]

            @pl.when(n > 0)
            def _():
                pltpu.make_async_copy(zero_s.at[pl.ds(0, n)], xs_hbm.at[pl.ds(pad_start_ref[e], n)], zsem).start()

        for e in range(pad_rows_ref.shape[0]):
            n = pad_rows_ref[e]

            @pl.when(n > 0)
            def _():
                pltpu.make_async_copy(zero_s.at[pl.ds(0, n)], xs_hbm.at[pl.ds(pad_start_ref[e], n)], zsem).wait()

    stage(step, slot).wait()

    @pl.when(step > 0)
    def _():
        wait_tile()

    @pl.when(step < last)
    def _():
        stage(step + 1, 1 - slot).start()

    for j in range(tm):
        for k in range(TOP_K):
            pltpu.make_async_copy(buf.at[slot, j], xs_hbm.at[dest_ref[0, 0, TOP_K * j + k]], sem).start()

    @pl.when(step == last)
    def _():
        wait_tile()


def _dispatch_call(t, p_len, pad_start, pad_rows, dest, hp):
    tm = TM_OUT
    grid_spec = pltpu.PrefetchScalarGridSpec(
        num_scalar_prefetch=2,
        grid=(t // tm,),
        in_specs=[pl.BlockSpec((1, 1, TOP_K * tm), lambda i, *_: (i, 0, 0), memory_space=pltpu.SMEM),
                  pl.BlockSpec(memory_space=pl.ANY)],
        out_specs=pl.BlockSpec(memory_space=pl.ANY),
        scratch_shapes=[pltpu.VMEM((2, tm, TOK_TILE_ROWS, LANES), jnp.uint32),
                        pltpu.VMEM((BM, TOK_TILE_ROWS, LANES), jnp.uint32),
                        pltpu.SemaphoreType.DMA((2,)), pltpu.SemaphoreType.DMA(()), pltpu.SemaphoreType.DMA(())],
    )
    return pl.pallas_call(
        _dispatch_kernel,
        grid_spec=grid_spec,
        out_shape=jax.ShapeDtypeStruct((p_len, TOK_TILE_ROWS, LANES), jnp.uint32),
        compiler_params=pltpu.CompilerParams(dimension_semantics=("arbitrary",), vmem_limit_bytes=VMEM_LIMIT),
        name="dispatch",
    )(pad_start, pad_rows, dest, hp.reshape(t, TOK_TILE_ROWS, LANES))


def _expert_kernel(be_ref, cnt_ref, nv_ref, xs_ref, w1_ref, w3_ref, w2_ref, y_ref):
    del be_ref, nv_ref
    half = D_MODEL // 2
    rows = cnt_ref[pl.program_id(0)]

    @pl.when(rows == 0)
    def _():
        y_ref[...] = jnp.zeros(y_ref.shape, y_ref.dtype)

    @pl.when(rows > 0)
    def _():
        packed = jnp.concatenate([xs_ref[pl.ds(s, BM, stride=TOK_TILE_ROWS), :] for s in range(TOK_TILE_ROWS)], axis=1)
        lo, hi = _unpack_bf16_pairs(packed)
        a1 = (jnp.dot(lo, w1_ref[0, 0:half, :].astype(_BF16), preferred_element_type=_F32)
              + jnp.dot(hi, w1_ref[0, half:D_MODEL, :].astype(_BF16), preferred_element_type=_F32))
        a3 = (jnp.dot(lo, w3_ref[0, 0:half, :].astype(_BF16), preferred_element_type=_F32)
              + jnp.dot(hi, w3_ref[0, half:D_MODEL, :].astype(_BF16), preferred_element_type=_F32))
        hid = (a1 * _sigmoid(a1) * a3).astype(_BF16)
        y_ref[...] = jnp.dot(hid, w2_ref[0].astype(_BF16), preferred_element_type=_F32)


def _expert_call(layer, n_blocks, block_e, block_cnt, n_valid, xs, w1, w3, w2):
    blk = lambda i, be, cnt, nv: (jnp.minimum(i, nv[0] - 1), 0)
    wmap = lambda i, be, cnt, nv: (layer, be[i], 0, 0)
    grid_spec = pltpu.PrefetchScalarGridSpec(
        num_scalar_prefetch=3,
        grid=(n_blocks,),
        in_specs=[
            pl.BlockSpec((BM * TOK_TILE_ROWS, LANES), blk),
            pl.BlockSpec((None, 1, D_MODEL, D_EXPERT), wmap),
            pl.BlockSpec((None, 1, D_MODEL, D_EXPERT), wmap),
            pl.BlockSpec((None, 1, D_EXPERT, D_MODEL), wmap),
        ],
        out_specs=pl.BlockSpec((BM, D_MODEL), lambda i, *_: (i, 0)),
    )
    return pl.pallas_call(
        _expert_kernel,
        grid_spec=grid_spec,
        out_shape=jax.ShapeDtypeStruct((n_blocks * BM, D_MODEL), _F32),
        compiler_params=pltpu.CompilerParams(dimension_semantics=("arbitrary",), vmem_limit_bytes=VMEM_LIMIT),
        name="experts",
    )(block_e, block_cnt, n_valid, xs.reshape(n_blocks * BM * TOK_TILE_ROWS, LANES), w1, w3, w2)


def _slot_tables(t, rec, counts_f):
    n_assign = t * TOP_K
    n_blocks = n_assign // BM + N_EXPERTS
    counts = counts_f[0, :N_EXPERTS].astype(jnp.int32)
    padded = ((counts + BM - 1) // BM) * BM
    pend = jnp.cumsum(padded)
    pstart = pend - padded
    starts = jnp.arange(n_blocks, dtype=jnp.int32) * BM
    block_e = jnp.minimum(jnp.sum(pend[None, :] <= starts[:, None], axis=1), N_EXPERTS - 1).astype(jnp.int32)
    seg_end = (pstart + counts)[block_e]
    block_cnt = jnp.where(starts < pend[-1], jnp.clip(seg_end - starts, 0, BM), 0).astype(jnp.int32)
    n_valid = jnp.maximum(pend[-1] // BM, 1).astype(jnp.int32).reshape(1)
    block_e = jnp.where(starts < pend[-1], block_e, block_e[n_valid[0] - 1])
    expert = rec[:, REC_EXPERT:REC_EXPERT + TOP_K].astype(jnp.int32)
    rank = rec[:, REC_RANK:REC_RANK + TOP_K].astype(jnp.int32)
    dest = (pstart[expert] + rank).reshape(t // TM_OUT, 1, TOP_K * TM_OUT)
    tail = pend[-1] + jnp.arange(N_EXPERTS, dtype=jnp.int32) * BM
    pad_start = jnp.concatenate([pstart + counts, tail])
    pad_rows = jnp.concatenate([padded - counts, jnp.where(tail < n_blocks * BM, BM, 0)])
    return n_blocks, block_e, block_cnt, n_valid, pad_start, pad_rows.astype(jnp.int32), dest


def _final_kernel(x_ref, rec_ref, y_hbm, dest_ref, dest_next_ref, g_ref, o_ref, ybuf, ysem):
    x = _combined_input(x_ref, rec_ref, y_hbm, dest_ref, dest_next_ref, ybuf, ysem)
    o_ref[...] = x * _rms_scale(x) * g_ref[...]


def _final_call(t, x, moe, g):
    tm = TM_OUT
    n_tiles = t // tm
    rec, ys, dest = moe
    in_specs, scratch = _combine_specs(tm, n_tiles)
    return pl.pallas_call(
        _final_kernel,
        grid=(n_tiles,),
        in_specs=in_specs + [_resident(g.shape)],
        out_specs=pl.BlockSpec((tm, D_MODEL), lambda i: (i, 0)),
        out_shape=jax.ShapeDtypeStruct((t, D_MODEL), _F32),
        scratch_shapes=scratch,
        compiler_params=pltpu.CompilerParams(dimension_semantics=("arbitrary",), vmem_limit_bytes=VMEM_LIMIT),
        name="final_norm",
    )(x, rec, ys, dest, dest, g)


def kernel(x, norm_mix, w_in, pool_w, pool_scale, sgu_ln_g, sgu_ln_b, sgu_w, sgu_b, conv_dw_w, conv_dw_b,
           conv_ln_g, conv_ln_b, conv_pw, w_out, norm_ffn, router_g_w, router_g_b, router_e_w, router_e_b,
           exp_w1, exp_w3, exp_w2, norm_final):
    b, s, d = x.shape
    depth = w_in.shape[0]
    assert d == D_MODEL and s % TM_MIX == 0 and (b * s) % TM_OUT == 0 and TM_MIX % CHUNK == 0 and TM_MIX == TM_OUT
    t = b * s
    xf = x.reshape(t, d)
    moe = None
    for l in range(depth):
        row2 = lambda v: v[l].reshape(1, -1)
        sbias = jnp.repeat(sgu_b[l].T, SGU_HEAD_DIM, axis=1)
        outs = _mixer_call(
            t, s, xf, moe, row2(norm_mix), w_in[l].astype(_BF16), pool_w[l].astype(_BF16),
            row2(pool_scale), row2(sgu_ln_g), row2(sgu_ln_b), sgu_w[l], sbias, conv_dw_w[l], row2(conv_dw_b),
            row2(conv_ln_g), row2(conv_ln_b), conv_pw[l].astype(_BF16))
        if moe is not None:
            ycat, xf = outs
        else:
            (ycat,) = outs
        pad = ROUTER_COLS - N_GROUPS - N_EXPERTS
        wr = jnp.concatenate([router_g_w[l], router_e_w[l], jnp.zeros((d, pad), _F32)], axis=1).astype(_BF16)
        br = jnp.concatenate([router_g_b[l], router_e_b[l], jnp.zeros((pad,), _F32)]).reshape(1, ROUTER_COLS)
        xf, hp, rec, counts = _out_call(t, ycat, xf, w_out[l].astype(_BF16), row2(norm_ffn), wr, br)
        n_blocks, block_e, block_cnt, n_valid, pad_start, pad_rows, dest = _slot_tables(t, rec, counts)
        xs = _dispatch_call(t, n_blocks * BM, pad_start, pad_rows, dest, hp)
        ys = _expert_call(l, n_blocks, block_e, block_cnt, n_valid, xs, exp_w1, exp_w3, exp_w2)
        moe = (rec, ys, dest)
    out = _final_call(t, xf, moe, norm_final.reshape(1, -1))
    return out.reshape(b, s, d)
```

```python
import functools

import jax
import jax.numpy as jnp
from jax import lax
from jax.experimental import pallas as pl
from jax.experimental.pallas import tpu as pltpu

D_MODEL = 2048
D_A = D_MODEL // 4
D_B = 3 * D_MODEL // 8
D_C = D_MODEL - D_A - D_B
POOL_WINDOWS = (2, 4, 8, 16)
POOL_GROUP_DIM = D_A // len(POOL_WINDOWS)
CHUNK = 128
SGU_HEAD_DIM = 128
SGU_HEADS = D_B // SGU_HEAD_DIM
CONV_WIDTH = 31
N_IN = D_A + 2 * D_B + 2 * D_C
N_GROUPS = 4
E_PER_GROUP = 8
N_EXPERTS = N_GROUPS * E_PER_GROUP
TOP_K = 2
D_EXPERT = D_MODEL // 4
EPS = 1e-6

LANES = 128
SUBLANES = 8
POOL_HALO = 16
CONV_HALO = 32
CONV_ROWS = 64
CONV_PITCH = 2
TM_MIX = 256
TM_OUT = 256
BM = 256
TOK_TILE_ROWS = D_MODEL // 2 // LANES
ROUTER_COLS = LANES
REC_EXPERT = 0
REC_WEIGHT = TOP_K
REC_RANK = 2 * TOP_K
VMEM_LIMIT = 56 * 1024 * 1024

_F32 = jnp.float32
_BF16 = jnp.bfloat16


def _resident(shape):
    nd = len(shape)
    return pl.BlockSpec(shape, lambda *_: (0,) * nd, pipeline_mode=pl.Buffered(1))


def _rms_scale(x):
    return lax.rsqrt(jnp.mean(x * x, axis=-1, keepdims=True) + EPS)


def _layer_norm(x, g, b):
    mu = jnp.mean(x, axis=-1, keepdims=True)
    xc = x - mu
    return xc * lax.rsqrt(jnp.mean(xc * xc, axis=-1, keepdims=True) + EPS) * g + b


def _gelu_tanh(x):
    return 0.5 * x * (1.0 + jnp.tanh(0.7978845608028654 * (x + 0.044715 * (x * x * x))))


def _sigmoid(x):
    return 1.0 / (1.0 + jnp.exp(-x))


def _start_expert_rows(y_hbm, dest_ref, ybuf, sem, slot):
    tm = ybuf.shape[2]
    for j in range(tm):
        for k in range(TOP_K):
            pltpu.make_async_copy(y_hbm.at[pl.ds(dest_ref[0, 0, TOP_K * j + k], 1)],
                                  ybuf.at[slot, k, pl.ds(j, 1)], sem.at[slot]).start()


def _wait_expert_rows(ybuf, sem, slot):
    pltpu.make_async_copy(ybuf.at[slot], ybuf.at[slot], sem.at[slot]).wait()


def _combined_input(x_ref, rec_ref, y_hbm, dest_ref, dest_next_ref, ybuf, sem):
    step = pl.program_id(0)
    slot = lax.rem(step, 2)

    @pl.when(step == 0)
    def _():
        _start_expert_rows(y_hbm, dest_ref, ybuf, sem, 0)

    for s in range(2):
        @pl.when((step < pl.num_programs(0) - 1) & (slot == s))
        def _(s=s):
            _start_expert_rows(y_hbm, dest_next_ref, ybuf, sem, 1 - s)

    _wait_expert_rows(ybuf, sem, slot)
    rec = rec_ref[...]
    moe = (rec[:, REC_WEIGHT:REC_WEIGHT + 1] * ybuf[slot, 0]
           + rec[:, REC_WEIGHT + 1:REC_WEIGHT + 2] * ybuf[slot, 1])
    return x_ref[...] + moe


def _combine_specs(tm, n_tiles):
    row = lambda i: (i, 0)
    in_specs = [
        pl.BlockSpec((tm, D_MODEL), row),
        pl.BlockSpec((tm, ROUTER_COLS), row),
        pl.BlockSpec(memory_space=pl.ANY),
        pl.BlockSpec((1, 1, TOP_K * tm), lambda i: (i, 0, 0), memory_space=pltpu.SMEM),
        pl.BlockSpec((1, 1, TOP_K * tm), lambda i: (jnp.minimum(i + 1, n_tiles - 1), 0, 0), memory_space=pltpu.SMEM),
    ]
    scratch = [pltpu.VMEM((2, TOP_K, tm, D_MODEL), _F32), pltpu.SemaphoreType.DMA((2,))]
    return in_specs, scratch


def _mixer_kernel(combine, tiles_per_seq, *refs):
    if combine:
        x_ref, rec_ref, y_hbm, dest_ref, dest_next_ref = refs[:5]
        refs = refs[5:]
    else:
        x_ref = refs[0]
        refs = refs[1:]
    (g_ref, win_ref, poolw_ref, pscale_ref, sg_ref, sb_ref, sw_ref, sbias_ref,
     dww_ref, dwb_ref, cg_ref, cb_ref, pw_ref) = refs[:13]
    refs = refs[13:]
    if combine:
        ycat_ref, xo_ref = refs[:2]
        h_s, xa_s, z_s, cv_s, ybuf, ysem = refs[2:]
    else:
        ycat_ref = refs[0]
        h_s, xa_s, z_s, cv_s = refs[1:]
    tm = ycat_ref.shape[0]

    tile = pl.program_id(0) % tiles_per_seq
    first = tile == 0

    if combine:
        x = _combined_input(x_ref, rec_ref, y_hbm, dest_ref, dest_next_ref, ybuf, ysem)
        xo_ref[...] = x
    else:
        x = x_ref[...]
    n_ct = D_C // LANES
    halo_src = pl.ds(CONV_PITCH * tm, CONV_HALO, stride=CONV_PITCH)
    halo_dst = pl.ds(0, CONV_HALO, stride=CONV_PITCH)

    @pl.when(first)
    def _():
        xa_s[0:POOL_HALO, :] = jnp.zeros((POOL_HALO, D_A), _F32)
        for c in range(n_ct):
            z_s[c, halo_dst, :] = jnp.zeros((CONV_HALO, LANES), _F32)

    @pl.when(jnp.logical_not(first))
    def _():
        xa_s[0:POOL_HALO, :] = xa_s[tm:tm + POOL_HALO, :]
        for c in range(n_ct):
            z_s[c, halo_dst, :] = z_s[c, halo_src, :]

    h_s[...] = (x * _rms_scale(x) * g_ref[...]).astype(_BF16)

    pc = jnp.dot(h_s[...], win_ref[:, D_A + 2 * D_B:N_IN], preferred_element_type=_F32)
    zc = pc[:, :D_C] * _sigmoid(pc[:, D_C:])
    for c in range(n_ct):
        z_s[c, pl.ds(CONV_PITCH * CONV_HALO, tm, stride=CONV_PITCH), :] = zc[:, c * LANES:(c + 1) * LANES]
    for r0 in range(0, tm, CONV_ROWS):
        for c in range(n_ct):
            lanes = slice(c * LANES, (c + 1) * LANES)
            acc = jnp.zeros((CONV_ROWS, LANES), _F32)
            for k in range(CONV_WIDTH):
                off = r0 + CONV_HALO - (CONV_WIDTH - 1) + k
                acc = acc + dww_ref[k:k + 1, lanes] * z_s[c, pl.ds(CONV_PITCH * off, CONV_ROWS, stride=CONV_PITCH), :]
            cv_s[r0:r0 + CONV_ROWS, lanes] = acc + dwb_ref[:, lanes]

    xa_s[POOL_HALO:POOL_HALO + tm, :] = jnp.dot(h_s[...], win_ref[:, 0:D_A], preferred_element_type=_F32)
    pos = tile * tm + lax.broadcasted_iota(jnp.int32, (tm, 1), 0)
    for gi, w in enumerate(POOL_WINDOWS):
        cols = slice(gi * POOL_GROUP_DIM, (gi + 1) * POOL_GROUP_DIM)
        xg = xa_s[POOL_HALO:POOL_HALO + tm, cols]
        acc = xg
        for k in range(1, w):
            acc = acc + xa_s[POOL_HALO - k:POOL_HALO - k + tm, cols]
        cnt = jnp.minimum(pos + 1, w).astype(_F32)
        d = acc / cnt - xg
        yg = jnp.dot(d.astype(_BF16), poolw_ref[gi], preferred_element_type=_F32) * pscale_ref[:, cols]
        ycat_ref[:, cols] = yg.astype(_BF16)

    pb = jnp.dot(h_s[...], win_ref[:, D_A:D_A + 2 * D_B], preferred_element_type=_F32)
    zb = _gelu_tanh(pb)
    u = zb[:, :D_B]
    vb = _layer_norm(zb[:, D_B:], sg_ref[...], sb_ref[...]).astype(_BF16)
    r_i = lax.broadcasted_iota(jnp.int32, (CHUNK, CHUNK), 0)
    c_i = lax.broadcasted_iota(jnp.int32, (CHUNK, CHUNK), 1)
    causal = r_i >= c_i
    for hd in range(SGU_HEADS):
        hc = slice(hd * SGU_HEAD_DIM, (hd + 1) * SGU_HEAD_DIM)
        wm = jnp.where(causal, sw_ref[hd], 0.0).astype(_BF16)
        for c in range(tm // CHUNK):
            rows = slice(c * CHUNK, (c + 1) * CHUNK)
            gate = jnp.dot(wm, vb[rows, hc], preferred_element_type=_F32) + sbias_ref[:, hc]
            ycat_ref[rows, D_A + hd * SGU_HEAD_DIM:D_A + (hd + 1) * SGU_HEAD_DIM] = (u[rows, hc] * gate).astype(_BF16)

    zn = _layer_norm(cv_s[...], cg_ref[...], cb_ref[...])
    zs = (zn * _sigmoid(zn)).astype(_BF16)
    ycat_ref[:, D_A + D_B:D_MODEL] = jnp.dot(zs, pw_ref[...], preferred_element_type=_F32).astype(_BF16)


def _mixer_call(t, seq, x, moe, g, win, poolw, pscale, sg, sb, sw, sbias, dww, dwb, cg, cb, pw):
    tm = TM_MIX
    n_tiles = t // tm
    row = lambda i: (i, 0)
    combine = moe is not None
    scratch = [
        pltpu.VMEM((tm, D_MODEL), _BF16),
        pltpu.VMEM((POOL_HALO + tm, D_A), _F32),
        pltpu.VMEM((D_C // LANES, CONV_PITCH * (CONV_HALO + tm), LANES), _F32),
        pltpu.VMEM((tm, D_C), _F32),
    ]
    if combine:
        rec, ys, dest = moe
        in_specs, extra = _combine_specs(tm, n_tiles)
        scratch += extra
        args = [x, rec, ys, dest, dest]
    else:
        in_specs = [pl.BlockSpec((tm, D_MODEL), row)]
        args = [x]
    weights = [g, win, poolw, pscale, sg, sb, sw, sbias, dww, dwb, cg, cb, pw]
    in_specs += [_resident(w.shape) for w in weights]
    args += weights
    out_shape = [jax.ShapeDtypeStruct((t, D_MODEL), _BF16)]
    out_specs = [pl.BlockSpec((tm, D_MODEL), row)]
    if combine:
        out_shape.append(jax.ShapeDtypeStruct((t, D_MODEL), _F32))
        out_specs.append(pl.BlockSpec((tm, D_MODEL), row))
    return pl.pallas_call(
        functools.partial(_mixer_kernel, combine, seq // tm),
        grid=(n_tiles,),
        in_specs=in_specs,
        out_specs=out_specs,
        out_shape=out_shape,
        scratch_shapes=scratch,
        compiler_params=pltpu.CompilerParams(dimension_semantics=("arbitrary",), vmem_limit_bytes=VMEM_LIMIT),
        name="mixer_combine" if combine else "mixer",
    )(*args)


def _pack_bf16_pairs(hb):
    u = lax.bitcast_convert_type(hb.astype(_F32), jnp.uint32)
    c = hb.shape[1] // 2
    return (u[:, c:] & jnp.uint32(0xFFFF0000)) | (u[:, :c] >> 16)


def _unpack_bf16_pairs(p):
    lo = lax.bitcast_convert_type(p << 16, _F32).astype(_BF16)
    hi = lax.bitcast_convert_type(p & jnp.uint32(0xFFFF0000), _F32).astype(_BF16)
    return lo, hi


def _route_rows(logits, seen):
    rows = logits.shape[0]
    lane = lax.broadcasted_iota(jnp.int32, logits.shape, 1)
    neg = jnp.float32(-jnp.inf)
    big = jnp.int32(ROUTER_COLS)

    def first_max(v):
        m = jnp.max(v, axis=-1, keepdims=True)
        return m, jnp.min(jnp.where(v == m, lane, big), axis=-1, keepdims=True)

    lg = jnp.where(lane < N_GROUPS, logits, neg)
    mg, g_idx = first_max(lg)
    g_w = 1.0 / jnp.sum(jnp.exp(lg - mg), axis=-1, keepdims=True)
    e_lo = N_GROUPS + g_idx * E_PER_GROUP
    le = jnp.where((lane >= e_lo) & (lane < e_lo + E_PER_GROUP), logits, neg)
    m1, i1 = first_max(le)
    m2, i2 = first_max(jnp.where(lane == i1, neg, le))
    p2 = jnp.exp(m2 - m1)
    w1 = g_w / (1.0 + p2)
    w2 = g_w * p2 / (1.0 + p2)
    e1 = i1 - N_GROUPS
    e2 = i2 - N_GROUPS

    hit1 = lane == e1
    hit2 = lane == e2
    onehot = jnp.where(hit1, 1.0, 0.0) + jnp.where(hit2, 1.0, 0.0)
    r_i = lax.broadcasted_iota(jnp.int32, (rows, rows), 0)
    c_i = lax.broadcasted_iota(jnp.int32, (rows, rows), 1)
    earlier = jnp.where(r_i > c_i, 1.0, 0.0).astype(_BF16)
    before = seen + jnp.dot(earlier, onehot.astype(_BF16), preferred_element_type=_F32)
    rank1 = jnp.sum(jnp.where(hit1, before, 0.0), axis=-1, keepdims=True)
    rank2 = jnp.sum(jnp.where(hit2, before, 0.0), axis=-1, keepdims=True)
    seen = seen + jnp.sum(onehot, axis=0, keepdims=True)

    rec = jnp.zeros(logits.shape, _F32)
    for k, v in ((REC_EXPERT, e1.astype(_F32)), (REC_EXPERT + 1, e2.astype(_F32)), (REC_WEIGHT, w1),
                 (REC_WEIGHT + 1, w2), (REC_RANK, rank1), (REC_RANK + 1, rank2)):
        rec = jnp.where(lane == k, v, rec)
    return rec, seen


def _out_kernel(ycat_ref, x_ref, wout_ref, g_ref, wr_ref, br_ref, x1_ref, hp_ref, rec_ref, cnt_ref, lg_s, seen_s):
    step = pl.program_id(0)
    slot = lax.rem(step, 2)

    @pl.when(step == 0)
    def _():
        lg_s[...] = jnp.zeros(lg_s.shape, _F32)
        seen_s[...] = jnp.zeros(seen_s.shape, _F32)

    prev_logits = lg_s[1 - slot]
    prev_seen = seen_s[...]

    x1 = x_ref[...] + jnp.dot(ycat_ref[...], wout_ref[...], preferred_element_type=_F32)
    x1_ref[...] = x1
    hb = (x1 * _rms_scale(x1) * g_ref[...]).astype(_BF16)
    lg_s[slot] = jnp.dot(hb, wr_ref[...], preferred_element_type=_F32) + br_ref[...]

    rec, seen = _route_rows(prev_logits, prev_seen)
    rec_ref[...] = rec
    seen = jnp.where(step == 0, 0.0, seen)
    seen_s[...] = seen
    cnt_ref[...] = jnp.broadcast_to(seen, cnt_ref.shape)
    packed = _pack_bf16_pairs(hb)
    tm = packed.shape[0]
    for s in range(TOK_TILE_ROWS):
        hp_ref[pl.ds(s, tm, stride=TOK_TILE_ROWS), :] = packed[:, s * LANES:(s + 1) * LANES]


def _out_call(t, ycat, x, wout, g, wr, br):
    tm = TM_OUT
    n_tiles = t // tm
    row = lambda i: (jnp.minimum(i, n_tiles - 1), 0)
    prev = lambda i: (jnp.maximum(i - 1, 0), 0)
    return pl.pallas_call(
        _out_kernel,
        grid=(n_tiles + 1,),
        in_specs=[pl.BlockSpec((tm, D_MODEL), row), pl.BlockSpec((tm, D_MODEL), row),
                  _resident(wout.shape), _resident(g.shape), _resident(wr.shape), _resident(br.shape)],
        out_specs=[pl.BlockSpec((tm, D_MODEL), row), pl.BlockSpec((tm * TOK_TILE_ROWS, LANES), row),
                   pl.BlockSpec((tm, ROUTER_COLS), prev), pl.BlockSpec((SUBLANES, ROUTER_COLS), lambda i: (0, 0))],
        out_shape=[jax.ShapeDtypeStruct((t, D_MODEL), _F32),
                   jax.ShapeDtypeStruct((t * TOK_TILE_ROWS, LANES), jnp.uint32),
                   jax.ShapeDtypeStruct((t, ROUTER_COLS), _F32),
                   jax.ShapeDtypeStruct((SUBLANES, ROUTER_COLS), _F32)],
        scratch_shapes=[pltpu.VMEM((2, tm, ROUTER_COLS), _F32), pltpu.VMEM((1, ROUTER_COLS), _F32)],
        compiler_params=pltpu.CompilerParams(dimension_semantics=("arbitrary",), vmem_limit_bytes=VMEM_LIMIT),
        name="out_proj",
    )(ycat, x, wout, g, wr, br)


def _dispatch_kernel(pad_start_ref, pad_rows_ref, dest_ref, hp_hbm, xs_hbm, buf, zero_s, lsem, sem, zsem):
    step = pl.program_id(0)
    last = pl.num_programs(0) - 1
    tm = dest_ref.shape[2] // TOP_K
    slot = lax.rem(step, 2)

    def stage(tile, to_slot):
        return pltpu.make_async_copy(hp_hbm.at[pl.ds(tile * tm, tm)], buf.at[to_slot], lsem.at[to_slot])

    def wait_tile():
        view = xs_hbm.at[pl.ds(0, TOP_K * tm)]
        pltpu.make_async_copy(view, view, sem).wait()

    @pl.when(step == 0)
    def _():
        stage(0, 0).start()
        zero_s[...] = jnp.zeros(zero_s.shape, zero_s.dtype)
        for e in range(pad_rows_ref.shape[0]):
            n = pad_rows_ref[e]

            @pl.when(n > 0)
            def _():
                pltpu.make_async_copy(zero_s.at[pl.ds(0, n)], xs_hbm.at[pl.ds(pad_start_ref[e], n)], zsem).start()

        for e in range(pad_rows_ref.shape[0]):
            n = pad_rows_ref[e]

            @pl.when(n > 0)
            def _():
                pltpu.make_async_copy(zero_s.at[pl.ds(0, n)], xs_hbm.at[pl.ds(pad_start_ref[e], n)], zsem).wait()

    stage(step, slot).wait()

    @pl.when(step > 0)
    def _():
        wait_tile()

    @pl.when(step < last)
    def _():
        stage(step + 1, 1 - slot).start()

    for j in range(tm):
        for k in range(TOP_K):
            pltpu.make_async_copy(buf.at[slot, j], xs_hbm.at[dest_ref[0, 0, TOP_K * j + k]], sem).start(priority=k % 2)

    @pl.when(step == last)
    def _():
        wait_tile()


def _dispatch_call(t, p_len, pad_start, pad_rows, dest, hp):
    tm = TM_OUT
    grid_spec = pltpu.PrefetchScalarGridSpec(
        num_scalar_prefetch=2,
        grid=(t // tm,),
        in_specs=[pl.BlockSpec((1, 1, TOP_K * tm), lambda i, *_: (i, 0, 0), memory_space=pltpu.SMEM),
                  pl.BlockSpec(memory_space=pl.ANY)],
        out_specs=pl.BlockSpec(memory_space=pl.ANY),
        scratch_shapes=[pltpu.VMEM((2, tm, TOK_TILE_ROWS, LANES), jnp.uint32),
                        pltpu.VMEM((BM, TOK_TILE_ROWS, LANES), jnp.uint32),
                        pltpu.SemaphoreType.DMA((2,)), pltpu.SemaphoreType.DMA(()), pltpu.SemaphoreType.DMA(())],
    )
    return pl.pallas_call(
        _dispatch_kernel,
        grid_spec=grid_spec,
        out_shape=jax.ShapeDtypeStruct((p_len, TOK_TILE_ROWS, LANES), jnp.uint32),
        compiler_params=pltpu.CompilerParams(dimension_semantics=("arbitrary",), vmem_limit_bytes=VMEM_LIMIT),
        name="dispatch",
    )(pad_start, pad_rows, dest, hp.reshape(t, TOK_TILE_ROWS, LANES))


def _expert_kernel(be_ref, cnt_ref, nv_ref, xs_ref, w1_ref, w3_ref, w2_ref, y_ref):
    del be_ref, nv_ref
    half = D_MODEL // 2
    rows = cnt_ref[pl.program_id(0)]

    @pl.when(rows == 0)
    def _():
        y_ref[...] = jnp.zeros(y_ref.shape, y_ref.dtype)

    @pl.when(rows > 0)
    def _():
        packed = jnp.concatenate([xs_ref[pl.ds(s, BM, stride=TOK_TILE_ROWS), :] for s in range(TOK_TILE_ROWS)], axis=1)
        lo, hi = _unpack_bf16_pairs(packed)
        a1 = (jnp.dot(lo, w1_ref[0, 0:half, :].astype(_BF16), preferred_element_type=_F32)
              + jnp.dot(hi, w1_ref[0, half:D_MODEL, :].astype(_BF16), preferred_element_type=_F32))
        a3 = (jnp.dot(lo, w3_ref[0, 0:half, :].astype(_BF16), preferred_element_type=_F32)
              + jnp.dot(hi, w3_ref[0, half:D_MODEL, :].astype(_BF16), preferred_element_type=_F32))
        hid = (a1 * _sigmoid(a1) * a3).astype(_BF16)
        y_ref[...] = jnp.dot(hid, w2_ref[0].astype(_BF16), preferred_element_type=_F32)


def _expert_call(layer, n_blocks, block_e, block_cnt, n_valid, xs, w1, w3, w2):
    blk = lambda i, be, cnt, nv: (jnp.minimum(i, nv[0] - 1), 0)
    wmap = lambda i, be, cnt, nv: (layer, be[i], 0, 0)
    grid_spec = pltpu.PrefetchScalarGridSpec(
        num_scalar_prefetch=3,
        grid=(n_blocks,),
        in_specs=[
            pl.BlockSpec((BM * TOK_TILE_ROWS, LANES), blk),
            pl.BlockSpec((None, 1, D_MODEL, D_EXPERT), wmap),
            pl.BlockSpec((None, 1, D_MODEL, D_EXPERT), wmap),
            pl.BlockSpec((None, 1, D_EXPERT, D_MODEL), wmap),
        ],
        out_specs=pl.BlockSpec((BM, D_MODEL), lambda i, *_: (i, 0)),
    )
    return pl.pallas_call(
        _expert_kernel,
        grid_spec=grid_spec,
        out_shape=jax.ShapeDtypeStruct((n_blocks * BM, D_MODEL), _F32),
        compiler_params=pltpu.CompilerParams(dimension_semantics=("arbitrary",), vmem_limit_bytes=VMEM_LIMIT),
        name="experts",
    )(block_e, block_cnt, n_valid, xs.reshape(n_blocks * BM * TOK_TILE_ROWS, LANES), w1, w3, w2)


def _slot_tables(t, rec, counts_f):
    n_assign = t * TOP_K
    n_blocks = n_assign // BM + N_EXPERTS
    counts = counts_f[0, :N_EXPERTS].astype(jnp.int32)
    padded = ((counts + BM - 1) // BM) * BM
    pend = jnp.cumsum(padded)
    pstart = pend - padded
    starts = jnp.arange(n_blocks, dtype=jnp.int32) * BM
    block_e = jnp.minimum(jnp.sum(pend[None, :] <= starts[:, None], axis=1), N_EXPERTS - 1).astype(jnp.int32)
    seg_end = (pstart + counts)[block_e]
    block_cnt = jnp.where(starts < pend[-1], jnp.clip(seg_end - starts, 0, BM), 0).astype(jnp.int32)
    n_valid = jnp.maximum(pend[-1] // BM, 1).astype(jnp.int32).reshape(1)
    block_e = jnp.where(starts < pend[-1], block_e, block_e[n_valid[0] - 1])
    expert = rec[:, REC_EXPERT:REC_EXPERT + TOP_K].astype(jnp.int32)
    rank = rec[:, REC_RANK:REC_RANK + TOP_K].astype(jnp.int32)
    dest = (pstart[expert] + rank).reshape(t // TM_OUT, 1, TOP_K * TM_OUT)
    tail = pend[-1] + jnp.arange(N_EXPERTS, dtype=jnp.int32) * BM
    pad_start = jnp.concatenate([pstart + counts, tail])
    pad_rows = jnp.concatenate([padded - counts, jnp.where(tail < n_blocks * BM, BM, 0)])
    return n_blocks, block_e, block_cnt, n_valid, pad_start, pad_rows.astype(jnp.int32), dest


def _final_kernel(x_ref, rec_ref, y_hbm, dest_ref, dest_next_ref, g_ref, o_ref, ybuf, ysem):
    x = _combined_input(x_ref, rec_ref, y_hbm, dest_ref, dest_next_ref, ybuf, ysem)
    o_ref[...] = x * _rms_scale(x) * g_ref[...]


def _final_call(t, x, moe, g):
    tm = TM_OUT
    n_tiles = t // tm
    rec, ys, dest = moe
    in_specs, scratch = _combine_specs(tm, n_tiles)
    return pl.pallas_call(
        _final_kernel,
        grid=(n_tiles,),
        in_specs=in_specs + [_resident(g.shape)],
        out_specs=pl.BlockSpec((tm, D_MODEL), lambda i: (i, 0)),
        out_shape=jax.ShapeDtypeStruct((t, D_MODEL), _F32),
        scratch_shapes=scratch,
        compiler_params=pltpu.CompilerParams(dimension_semantics=("arbitrary",), vmem_limit_bytes=VMEM_LIMIT),
        name="final_norm",
    )(x, rec, ys, dest, dest, g)


def kernel(x, norm_mix, w_in, pool_w, pool_scale, sgu_ln_g, sgu_ln_b, sgu_w, sgu_b, conv_dw_w, conv_dw_b,
           conv_ln_g, conv_ln_b, conv_pw, w_out, norm_ffn, router_g_w, router_g_b, router_e_w, router_e_b,
           exp_w1, exp_w3, exp_w2, norm_final):
    b, s, d = x.shape
    depth = w_in.shape[0]
    assert d == D_MODEL and s % TM_MIX == 0 and (b * s) % TM_OUT == 0 and TM_MIX % CHUNK == 0 and TM_MIX == TM_OUT
    t = b * s
    xf = x.reshape(t, d)
    moe = None
    for l in range(depth):
        row2 = lambda v: v[l].reshape(1, -1)
        sbias = jnp.repeat(sgu_b[l].T, SGU_HEAD_DIM, axis=1)
        outs = _mixer_call(
            t, s, xf, moe, row2(norm_mix), w_in[l].astype(_BF16), pool_w[l].astype(_BF16),
            row2(pool_scale), row2(sgu_ln_g), row2(sgu_ln_b), sgu_w[l], sbias, conv_dw_w[l], row2(conv_dw_b),
            row2(conv_ln_g), row2(conv_ln_b), conv_pw[l].astype(_BF16))
        if moe is not None:
            ycat, xf = outs
        else:
            (ycat,) = outs
        pad = ROUTER_COLS - N_GROUPS - N_EXPERTS
        wr = jnp.concatenate([router_g_w[l], router_e_w[l], jnp.zeros((d, pad), _F32)], axis=1).astype(_BF16)
        br = jnp.concatenate([router_g_b[l], router_e_b[l], jnp.zeros((pad,), _F32)]).reshape(1, ROUTER_COLS)
        xf, hp, rec, counts = _out_call(t, ycat, xf, w_out[l].astype(_BF16), row2(norm_ffn), wr, br)
        n_blocks, block_e, block_cnt, n_valid, pad_start, pad_rows, dest = _slot_tables(t, rec, counts)
        xs = _dispatch_call(t, n_blocks * BM, pad_start, pad_rows, dest, hp)
        ys = _expert_call(l, n_blocks, block_e, block_cnt, n_valid, xs, exp_w1, exp_w3, exp_w2)
        moe = (rec, ys, dest)
    out = _final_call(t, xf, moe, norm_final.reshape(1, -1))
    return out.reshape(b, s, d)
```

```python
import functools

import jax
import jax.numpy as jnp
from jax import lax
from jax.experimental import pallas as pl
from jax.experimental.pallas import tpu as pltpu

D_MODEL = 2048
D_A = D_MODEL // 4
D_B = 3 * D_MODEL // 8
D_C = D_MODEL - D_A - D_B
POOL_WINDOWS = (2, 4, 8, 16)
POOL_GROUP_DIM = D_A // len(POOL_WINDOWS)
CHUNK = 128
SGU_HEAD_DIM = 128
SGU_HEADS = D_B // SGU_HEAD_DIM
CONV_WIDTH = 31
N_IN = D_A + 2 * D_B + 2 * D_C
N_GROUPS = 4
E_PER_GROUP = 8
N_EXPERTS = N_GROUPS * E_PER_GROUP
TOP_K = 2
D_EXPERT = D_MODEL // 4
EPS = 1e-6

LANES = 128
SUBLANES = 8
POOL_HALO = 16
CONV_HALO = 32
CONV_ROWS = 64
CONV_PITCH = 2
TM_MIX = 256
TM_OUT = 256
BM = 256
TOK_TILE_ROWS = D_MODEL // 2 // LANES
ROUTER_COLS = LANES
REC_EXPERT = 0
REC_WEIGHT = TOP_K
REC_RANK = 2 * TOP_K
VMEM_LIMIT = 56 * 1024 * 1024

_F32 = jnp.float32
_BF16 = jnp.bfloat16


def _resident(shape):
    nd = len(shape)
    return pl.BlockSpec(shape, lambda *_: (0,) * nd, pipeline_mode=pl.Buffered(1))


def _rms_scale(x):
    return lax.rsqrt(jnp.mean(x * x, axis=-1, keepdims=True) + EPS)


def _layer_norm(x, g, b):
    mu = jnp.mean(x, axis=-1, keepdims=True)
    xc = x - mu
    return xc * lax.rsqrt(jnp.mean(xc * xc, axis=-1, keepdims=True) + EPS) * g + b


def _gelu_tanh(x):
    return 0.5 * x * (1.0 + jnp.tanh(0.7978845608028654 * (x + 0.044715 * (x * x * x))))


def _sigmoid(x):
    return 1.0 / (1.0 + jnp.exp(-x))


def _pack_bf16_pairs(hb):
    u = lax.bitcast_convert_type(hb.astype(_F32), jnp.uint32)
    c = hb.shape[1] // 2
    return (u[:, c:] & jnp.uint32(0xFFFF0000)) | (u[:, :c] >> 16)


def _unpack_pairs_f32(p):
    return (lax.bitcast_convert_type(p << 16, _F32), lax.bitcast_convert_type(p & jnp.uint32(0xFFFF0000), _F32))


def _unpack_bf16_pairs(p):
    lo, hi = _unpack_pairs_f32(p)
    return lo.astype(_BF16), hi.astype(_BF16)


def _store_token_tiles(ref, packed):
    rows = packed.shape[0]
    for s in range(TOK_TILE_ROWS):
        ref[pl.ds(s, rows, stride=TOK_TILE_ROWS), :] = packed[:, s * LANES:(s + 1) * LANES]


def _load_token_tiles(ref, rows):
    return jnp.concatenate([ref[pl.ds(s, rows, stride=TOK_TILE_ROWS), :] for s in range(TOK_TILE_ROWS)], axis=1)


def _start_expert_rows(y_hbm, dest_ref, ybuf, sem, slot):
    tm = ybuf.shape[2] // TOK_TILE_ROWS
    for j in range(tm):
        for k in range(TOP_K):
            pltpu.make_async_copy(y_hbm.at[dest_ref[0, 0, TOP_K * j + k]],
                                  ybuf.at[slot, k, pl.ds(j * TOK_TILE_ROWS, TOK_TILE_ROWS)], sem.at[slot]).start()


def _wait_expert_rows(ybuf, sem, slot):
    pltpu.make_async_copy(ybuf.at[slot], ybuf.at[slot], sem.at[slot]).wait()


def _combined_input(x_ref, rec_ref, y_hbm, dest_ref, dest_next_ref, ybuf, sem):
    step = pl.program_id(0)
    slot = lax.rem(step, 2)

    @pl.when(step == 0)
    def _():
        _start_expert_rows(y_hbm, dest_ref, ybuf, sem, 0)

    for s in range(2):
        @pl.when((step < pl.num_programs(0) - 1) & (slot == s))
        def _(s=s):
            _start_expert_rows(y_hbm, dest_next_ref, ybuf, sem, 1 - s)

    _wait_expert_rows(ybuf, sem, slot)
    rec = rec_ref[...]
    tm = x_ref.shape[0]
    moe = None
    for k in range(TOP_K):
        yk = jnp.concatenate(_unpack_pairs_f32(_load_token_tiles(ybuf.at[slot, k], tm)), axis=1)
        term = rec[:, REC_WEIGHT + k:REC_WEIGHT + k + 1] * yk
        moe = term if moe is None else moe + term
    return x_ref[...] + moe


def _combine_specs(tm, n_tiles):
    row = lambda i: (i, 0)
    in_specs = [
        pl.BlockSpec((tm, D_MODEL), row),
        pl.BlockSpec((tm, ROUTER_COLS), row),
        pl.BlockSpec(memory_space=pl.ANY),
        pl.BlockSpec((1, 1, TOP_K * tm), lambda i: (i, 0, 0), memory_space=pltpu.SMEM),
        pl.BlockSpec((1, 1, TOP_K * tm), lambda i: (jnp.minimum(i + 1, n_tiles - 1), 0, 0), memory_space=pltpu.SMEM),
    ]
    scratch = [pltpu.VMEM((2, TOP_K, tm * TOK_TILE_ROWS, LANES), jnp.uint32), pltpu.SemaphoreType.DMA((2,))]
    return in_specs, scratch


def _mixer_kernel(combine, tiles_per_seq, *refs):
    if combine:
        x_ref, rec_ref, y_hbm, dest_ref, dest_next_ref = refs[:5]
        refs = refs[5:]
    else:
        x_ref = refs[0]
        refs = refs[1:]
    (g_ref, win_ref, poolw_ref, pscale_ref, sg_ref, sb_ref, sw_ref, sbias_ref,
     dww_ref, dwb_ref, cg_ref, cb_ref, pw_ref) = refs[:13]
    refs = refs[13:]
    if combine:
        ycat_ref, xo_ref = refs[:2]
        h_s, xa_s, z_s, cv_s, ybuf, ysem = refs[2:]
    else:
        ycat_ref = refs[0]
        h_s, xa_s, z_s, cv_s = refs[1:]
    tm = ycat_ref.shape[0]

    tile = pl.program_id(0) % tiles_per_seq
    first = tile == 0

    if combine:
        x = _combined_input(x_ref, rec_ref, y_hbm, dest_ref, dest_next_ref, ybuf, ysem)
        xo_ref[...] = x
    else:
        x = x_ref[...]
    n_ct = D_C // LANES
    halo_src = pl.ds(CONV_PITCH * tm, CONV_HALO, stride=CONV_PITCH)
    halo_dst = pl.ds(0, CONV_HALO, stride=CONV_PITCH)

    @pl.when(first)
    def _():
        xa_s[0:POOL_HALO, :] = jnp.zeros((POOL_HALO, D_A), _F32)
        for c in range(n_ct):
            z_s[c, halo_dst, :] = jnp.zeros((CONV_HALO, LANES), _F32)

    @pl.when(jnp.logical_not(first))
    def _():
        xa_s[0:POOL_HALO, :] = xa_s[tm:tm + POOL_HALO, :]
        for c in range(n_ct):
            z_s[c, halo_dst, :] = z_s[c, halo_src, :]

    h_s[...] = (x * _rms_scale(x) * g_ref[...]).astype(_BF16)

    pc = jnp.dot(h_s[...], win_ref[:, D_A + 2 * D_B:N_IN], preferred_element_type=_F32)
    zc = pc[:, :D_C] * _sigmoid(pc[:, D_C:])
    for c in range(n_ct):
        z_s[c, pl.ds(CONV_PITCH * CONV_HALO, tm, stride=CONV_PITCH), :] = zc[:, c * LANES:(c + 1) * LANES]
    for r0 in range(0, tm, CONV_ROWS):
        for c in range(n_ct):
            lanes = slice(c * LANES, (c + 1) * LANES)
            acc = jnp.zeros((CONV_ROWS, LANES), _F32)
            for k in range(CONV_WIDTH):
                off = r0 + CONV_HALO - (CONV_WIDTH - 1) + k
                acc = acc + dww_ref[k:k + 1, lanes] * z_s[c, pl.ds(CONV_PITCH * off, CONV_ROWS, stride=CONV_PITCH), :]
            cv_s[r0:r0 + CONV_ROWS, lanes] = acc + dwb_ref[:, lanes]

    xa_s[POOL_HALO:POOL_HALO + tm, :] = jnp.dot(h_s[...], win_ref[:, 0:D_A], preferred_element_type=_F32)
    pos = tile * tm + lax.broadcasted_iota(jnp.int32, (tm, 1), 0)
    for gi, w in enumerate(POOL_WINDOWS):
        cols = slice(gi * POOL_GROUP_DIM, (gi + 1) * POOL_GROUP_DIM)
        xg = xa_s[POOL_HALO:POOL_HALO + tm, cols]
        acc = xg
        for k in range(1, w):
            acc = acc + xa_s[POOL_HALO - k:POOL_HALO - k + tm, cols]
        cnt = jnp.minimum(pos + 1, w).astype(_F32)
        d = acc / cnt - xg
        yg = jnp.dot(d.astype(_BF16), poolw_ref[gi], preferred_element_type=_F32) * pscale_ref[:, cols]
        ycat_ref[:, cols] = yg.astype(_BF16)

    pb = jnp.dot(h_s[...], win_ref[:, D_A:D_A + 2 * D_B], preferred_element_type=_F32)
    zb = _gelu_tanh(pb)
    u = zb[:, :D_B]
    vb = _layer_norm(zb[:, D_B:], sg_ref[...], sb_ref[...]).astype(_BF16)
    r_i = lax.broadcasted_iota(jnp.int32, (CHUNK, CHUNK), 0)
    c_i = lax.broadcasted_iota(jnp.int32, (CHUNK, CHUNK), 1)
    causal = r_i >= c_i
    for hd in range(SGU_HEADS):
        hc = slice(hd * SGU_HEAD_DIM, (hd + 1) * SGU_HEAD_DIM)
        wm = jnp.where(causal, sw_ref[hd], 0.0).astype(_BF16)
        for c in range(tm // CHUNK):
            rows = slice(c * CHUNK, (c + 1) * CHUNK)
            gate = jnp.dot(wm, vb[rows, hc], preferred_element_type=_F32) + sbias_ref[:, hc]
            ycat_ref[rows, D_A + hd * SGU_HEAD_DIM:D_A + (hd + 1) * SGU_HEAD_DIM] = (u[rows, hc] * gate).astype(_BF16)

    zn = _layer_norm(cv_s[...], cg_ref[...], cb_ref[...])
    zs = (zn * _sigmoid(zn)).astype(_BF16)
    ycat_ref[:, D_A + D_B:D_MODEL] = jnp.dot(zs, pw_ref[...], preferred_element_type=_F32).astype(_BF16)


def _mixer_call(t, seq, x, moe, g, win, poolw, pscale, sg, sb, sw, sbias, dww, dwb, cg, cb, pw):
    tm = TM_MIX
    n_tiles = t // tm
    row = lambda i: (i, 0)
    combine = moe is not None
    scratch = [
        pltpu.VMEM((tm, D_MODEL), _BF16),
        pltpu.VMEM((POOL_HALO + tm, D_A), _F32),
        pltpu.VMEM((D_C // LANES, CONV_PITCH * (CONV_HALO + tm), LANES), _F32),
        pltpu.VMEM((tm, D_C), _F32),
    ]
    if combine:
        rec, ys, dest = moe
        in_specs, extra = _combine_specs(tm, n_tiles)
        scratch += extra
        args = [x, rec, ys, dest, dest]
    else:
        in_specs = [pl.BlockSpec((tm, D_MODEL), row)]
        args = [x]
    weights = [g, win, poolw, pscale, sg, sb, sw, sbias, dww, dwb, cg, cb, pw]
    in_specs += [_resident(w.shape) for w in weights]
    args += weights
    out_shape = [jax.ShapeDtypeStruct((t, D_MODEL), _BF16)]
    out_specs = [pl.BlockSpec((tm, D_MODEL), row)]
    if combine:
        out_shape.append(jax.ShapeDtypeStruct((t, D_MODEL), _F32))
        out_specs.append(pl.BlockSpec((tm, D_MODEL), row))
    return pl.pallas_call(
        functools.partial(_mixer_kernel, combine, seq // tm),
        grid=(n_tiles,),
        in_specs=in_specs,
        out_specs=out_specs,
        out_shape=out_shape,
        scratch_shapes=scratch,
        compiler_params=pltpu.CompilerParams(dimension_semantics=("arbitrary",), vmem_limit_bytes=VMEM_LIMIT),
        name="mixer_combine" if combine else "mixer",
    )(*args)


def _route_rows(logits, seen):
    rows = logits.shape[0]
    lane = lax.broadcasted_iota(jnp.int32, logits.shape, 1)
    neg = jnp.float32(-jnp.inf)
    big = jnp.int32(ROUTER_COLS)

    def first_max(v):
        m = jnp.max(v, axis=-1, keepdims=True)
        return m, jnp.min(jnp.where(v == m, lane, big), axis=-1, keepdims=True)

    lg = jnp.where(lane < N_GROUPS, logits, neg)
    mg, g_idx = first_max(lg)
    g_w = 1.0 / jnp.sum(jnp.exp(lg - mg), axis=-1, keepdims=True)
    e_lo = N_GROUPS + g_idx * E_PER_GROUP
    le = jnp.where((lane >= e_lo) & (lane < e_lo + E_PER_GROUP), logits, neg)
    m1, i1 = first_max(le)
    m2, i2 = first_max(jnp.where(lane == i1, neg, le))
    p2 = jnp.exp(m2 - m1)
    w1 = g_w / (1.0 + p2)
    w2 = g_w * p2 / (1.0 + p2)
    e1 = i1 - N_GROUPS
    e2 = i2 - N_GROUPS

    hit1 = lane == e1
    hit2 = lane == e2
    onehot = jnp.where(hit1, 1.0, 0.0) + jnp.where(hit2, 1.0, 0.0)
    r_i = lax.broadcasted_iota(jnp.int32, (rows, rows), 0)
    c_i = lax.broadcasted_iota(jnp.int32, (rows, rows), 1)
    earlier = jnp.where(r_i > c_i, 1.0, 0.0).astype(_BF16)
    before = seen + jnp.dot(earlier, onehot.astype(_BF16), preferred_element_type=_F32)
    rank1 = jnp.sum(jnp.where(hit1, before, 0.0), axis=-1, keepdims=True)
    rank2 = jnp.sum(jnp.where(hit2, before, 0.0), axis=-1, keepdims=True)
    seen = seen + jnp.sum(onehot, axis=0, keepdims=True)

    rec = jnp.zeros(logits.shape, _F32)
    for k, v in ((REC_EXPERT, e1.astype(_F32)), (REC_EXPERT + 1, e2.astype(_F32)), (REC_WEIGHT, w1),
                 (REC_WEIGHT + 1, w2), (REC_RANK, rank1), (REC_RANK + 1, rank2)):
        rec = jnp.where(lane == k, v, rec)
    return rec, seen


def _out_kernel(ycat_ref, x_ref, wout_ref, g_ref, wr_ref, br_ref, x1_ref, hp_ref, rec_ref, cnt_ref, lg_s, seen_s):
    step = pl.program_id(0)
    slot = lax.rem(step, 2)

    @pl.when(step == 0)
    def _():
        lg_s[...] = jnp.zeros(lg_s.shape, _F32)
        seen_s[...] = jnp.zeros(seen_s.shape, _F32)

    prev_logits = lg_s[1 - slot]
    prev_seen = seen_s[...]

    x1 = x_ref[...] + jnp.dot(ycat_ref[...], wout_ref[...], preferred_element_type=_F32)
    x1_ref[...] = x1
    hb = (x1 * _rms_scale(x1) * g_ref[...]).astype(_BF16)
    lg_s[slot] = jnp.dot(hb, wr_ref[...], preferred_element_type=_F32) + br_ref[...]

    rec, seen = _route_rows(prev_logits, prev_seen)
    rec_ref[...] = rec
    seen = jnp.where(step == 0, 0.0, seen)
    seen_s[...] = seen
    cnt_ref[...] = jnp.broadcast_to(seen, cnt_ref.shape)
    _store_token_tiles(hp_ref, _pack_bf16_pairs(hb))


def _out_call(t, ycat, x, wout, g, wr, br):
    tm = TM_OUT
    n_tiles = t // tm
    row = lambda i: (jnp.minimum(i, n_tiles - 1), 0)
    prev = lambda i: (jnp.maximum(i - 1, 0), 0)
    return pl.pallas_call(
        _out_kernel,
        grid=(n_tiles + 1,),
        in_specs=[pl.BlockSpec((tm, D_MODEL), row), pl.BlockSpec((tm, D_MODEL), row),
                  _resident(wout.shape), _resident(g.shape), _resident(wr.shape), _resident(br.shape)],
        out_specs=[pl.BlockSpec((tm, D_MODEL), row), pl.BlockSpec((tm * TOK_TILE_ROWS, LANES), row),
                   pl.BlockSpec((tm, ROUTER_COLS), prev), pl.BlockSpec((SUBLANES, ROUTER_COLS), lambda i: (0, 0))],
        out_shape=[jax.ShapeDtypeStruct((t, D_MODEL), _F32),
                   jax.ShapeDtypeStruct((t * TOK_TILE_ROWS, LANES), jnp.uint32),
                   jax.ShapeDtypeStruct((t, ROUTER_COLS), _F32),
                   jax.ShapeDtypeStruct((SUBLANES, ROUTER_COLS), _F32)],
        scratch_shapes=[pltpu.VMEM((2, tm, ROUTER_COLS), _F32), pltpu.VMEM((1, ROUTER_COLS), _F32)],
        compiler_params=pltpu.CompilerParams(dimension_semantics=("arbitrary",), vmem_limit_bytes=VMEM_LIMIT),
        name="out_proj",
    )(ycat, x, wout, g, wr, br)


def _dispatch_kernel(pad_start_ref, pad_rows_ref, dest_ref, hp_hbm, xs_hbm, buf, zero_s, lsem, sem, zsem):
    step = pl.program_id(0)
    last = pl.num_programs(0) - 1
    tm = dest_ref.shape[2] // TOP_K
    slot = lax.rem(step, 2)

    def stage(tile, to_slot):
        return pltpu.make_async_copy(hp_hbm.at[pl.ds(tile * tm, tm)], buf.at[to_slot], lsem.at[to_slot])

    def wait_tile():
        view = xs_hbm.at[pl.ds(0, TOP_K * tm)]
        pltpu.make_async_copy(view, view, sem).wait()

    @pl.when(step == 0)
    def _():
        stage(0, 0).start()
        zero_s[...] = jnp.zeros(zero_s.shape, zero_s.dtype)
        for e in range(pad_rows_ref.shape[0]):
            n = pad_rows_ref[e]

            @pl.when(n > 0)
            def _():
                pltpu.make_async_copy(zero_s.at[pl.ds(0, n)], xs_hbm.at[pl.ds(pad_start_ref[e], n)], zsem).start()

        for e in range(pad_rows_ref.shape[0]):
            n = pad_rows_ref[e]

            @pl.when(n > 0)
            def _():
                pltpu.make_async_copy(zero_s.at[pl.ds(0, n)], xs_hbm.at[pl.ds(pad_start_ref[e], n)], zsem).wait()

    stage(step, slot).wait()

    @pl.when(step > 0)
    def _():
        wait_tile()

    @pl.when(step < last)
    def _():
        stage(step + 1, 1 - slot).start()

    for j in range(tm):
        for k in range(TOP_K):
            pltpu.make_async_copy(buf.at[slot, j], xs_hbm.at[dest_ref[0, 0, TOP_K * j + k]], sem).start(priority=k % 2)

    @pl.when(step == last)
    def _():
        wait_tile()


def _dispatch_call(t, p_len, pad_start, pad_rows, dest, hp):
    tm = TM_OUT
    grid_spec = pltpu.PrefetchScalarGridSpec(
        num_scalar_prefetch=2,
        grid=(t // tm,),
        in_specs=[pl.BlockSpec((1, 1, TOP_K * tm), lambda i, *_: (i, 0, 0), memory_space=pltpu.SMEM),
                  pl.BlockSpec(memory_space=pl.ANY)],
        out_specs=pl.BlockSpec(memory_space=pl.ANY),
        scratch_shapes=[pltpu.VMEM((2, tm, TOK_TILE_ROWS, LANES), jnp.uint32),
                        pltpu.VMEM((BM, TOK_TILE_ROWS, LANES), jnp.uint32),
                        pltpu.SemaphoreType.DMA((2,)), pltpu.SemaphoreType.DMA(()), pltpu.SemaphoreType.DMA(())],
    )
    return pl.pallas_call(
        _dispatch_kernel,
        grid_spec=grid_spec,
        out_shape=jax.ShapeDtypeStruct((p_len, TOK_TILE_ROWS, LANES), jnp.uint32),
        compiler_params=pltpu.CompilerParams(dimension_semantics=("arbitrary",), vmem_limit_bytes=VMEM_LIMIT),
        name="dispatch",
    )(pad_start, pad_rows, dest, hp.reshape(t, TOK_TILE_ROWS, LANES))


def _expert_kernel(be_ref, cnt_ref, nv_ref, xs_ref, w1_ref, w3_ref, w2_ref, y_ref):
    del be_ref, nv_ref
    half = D_MODEL // 2
    rows = cnt_ref[pl.program_id(0)]

    @pl.when(rows == 0)
    def _():
        y_ref[...] = jnp.zeros(y_ref.shape, y_ref.dtype)

    @pl.when(rows > 0)
    def _():
        lo, hi = _unpack_bf16_pairs(_load_token_tiles(xs_ref, BM))
        a1 = (jnp.dot(lo, w1_ref[0, 0:half, :].astype(_BF16), preferred_element_type=_F32)
              + jnp.dot(hi, w1_ref[0, half:D_MODEL, :].astype(_BF16), preferred_element_type=_F32))
        a3 = (jnp.dot(lo, w3_ref[0, 0:half, :].astype(_BF16), preferred_element_type=_F32)
              + jnp.dot(hi, w3_ref[0, half:D_MODEL, :].astype(_BF16), preferred_element_type=_F32))
        hid = (a1 * _sigmoid(a1) * a3).astype(_BF16)
        y = jnp.dot(hid, w2_ref[0].astype(_BF16), preferred_element_type=_F32)
        _store_token_tiles(y_ref, _pack_bf16_pairs(y.astype(_BF16)))


def _expert_call(layer, n_blocks, block_e, block_cnt, n_valid, xs, w1, w3, w2):
    blk = lambda i, be, cnt, nv: (jnp.minimum(i, nv[0] - 1), 0)
    wmap = lambda i, be, cnt, nv: (layer, be[i], 0, 0)
    grid_spec = pltpu.PrefetchScalarGridSpec(
        num_scalar_prefetch=3,
        grid=(n_blocks,),
        in_specs=[
            pl.BlockSpec((BM * TOK_TILE_ROWS, LANES), blk),
            pl.BlockSpec((None, 1, D_MODEL, D_EXPERT), wmap),
            pl.BlockSpec((None, 1, D_MODEL, D_EXPERT), wmap),
            pl.BlockSpec((None, 1, D_EXPERT, D_MODEL), wmap),
        ],
        out_specs=pl.BlockSpec((BM * TOK_TILE_ROWS, LANES), lambda i, *_: (i, 0)),
    )
    ys = pl.pallas_call(
        _expert_kernel,
        grid_spec=grid_spec,
        out_shape=jax.ShapeDtypeStruct((n_blocks * BM * TOK_TILE_ROWS, LANES), jnp.uint32),
        compiler_params=pltpu.CompilerParams(dimension_semantics=("arbitrary",), vmem_limit_bytes=VMEM_LIMIT),
        name="experts",
    )(block_e, block_cnt, n_valid, xs.reshape(n_blocks * BM * TOK_TILE_ROWS, LANES), w1, w3, w2)
    return ys.reshape(n_blocks * BM, TOK_TILE_ROWS, LANES)


def _slot_tables(t, rec, counts_f):
    n_assign = t * TOP_K
    n_blocks = n_assign // BM + N_EXPERTS
    counts = counts_f[0, :N_EXPERTS].astype(jnp.int32)
    padded = ((counts + BM - 1) // BM) * BM
    pend = jnp.cumsum(padded)
    pstart = pend - padded
    starts = jnp.arange(n_blocks, dtype=jnp.int32) * BM
    block_e = jnp.minimum(jnp.sum(pend[None, :] <= starts[:, None], axis=1), N_EXPERTS - 1).astype(jnp.int32)
    seg_end = (pstart + counts)[block_e]
    block_cnt = jnp.where(starts < pend[-1], jnp.clip(seg_end - starts, 0, BM), 0).astype(jnp.int32)
    n_valid = jnp.maximum(pend[-1] // BM, 1).astype(jnp.int32).reshape(1)
    block_e = jnp.where(starts < pend[-1], block_e, block_e[n_valid[0] - 1])
    expert = rec[:, REC_EXPERT:REC_EXPERT + TOP_K].astype(jnp.int32)
    rank = rec[:, REC_RANK:REC_RANK + TOP_K].astype(jnp.int32)
    dest = (pstart[expert] + rank).reshape(t // TM_OUT, 1, TOP_K * TM_OUT)
    tail = pend[-1] + jnp.arange(N_EXPERTS, dtype=jnp.int32) * BM
    pad_start = jnp.concatenate([pstart + counts, tail])
    pad_rows = jnp.concatenate([padded - counts, jnp.where(tail < n_blocks * BM, BM, 0)])
    return n_blocks, block_e, block_cnt, n_valid, pad_start, pad_rows.astype(jnp.int32), dest


def _final_kernel(x_ref, rec_ref, y_hbm, dest_ref, dest_next_ref, g_ref, o_ref, ybuf, ysem):
    x = _combined_input(x_ref, rec_ref, y_hbm, dest_ref, dest_next_ref, ybuf, ysem)
    o_ref[...] = x * _rms_scale(x) * g_ref[...]


def _final_call(t, x, moe, g):
    tm = TM_OUT
    n_tiles = t // tm
    rec, ys, dest = moe
    in_specs, scratch = _combine_specs(tm, n_tiles)
    return pl.pallas_call(
        _final_kernel,
        grid=(n_tiles,),
        in_specs=in_specs + [_resident(g.shape)],
        out_specs=pl.BlockSpec((tm, D_MODEL), lambda i: (i, 0)),
        out_shape=jax.ShapeDtypeStruct((t, D_MODEL), _F32),
        scratch_shapes=scratch,
        compiler_params=pltpu.CompilerParams(dimension_semantics=("arbitrary",), vmem_limit_bytes=VMEM_LIMIT),
        name="final_norm",
    )(x, rec, ys, dest, dest, g)


def kernel(x, norm_mix, w_in, pool_w, pool_scale, sgu_ln_g, sgu_ln_b, sgu_w, sgu_b, conv_dw_w, conv_dw_b,
           conv_ln_g, conv_ln_b, conv_pw, w_out, norm_ffn, router_g_w, router_g_b, router_e_w, router_e_b,
           exp_w1, exp_w3, exp_w2, norm_final):
    b, s, d = x.shape
    depth = w_in.shape[0]
    assert d == D_MODEL and s % TM_MIX == 0 and (b * s) % TM_OUT == 0 and TM_MIX % CHUNK == 0 and TM_MIX == TM_OUT
    t = b * s
    xf = x.reshape(t, d)
    moe = None
    for l in range(depth):
        row2 = lambda v: v[l].reshape(1, -1)
        sbias = jnp.repeat(sgu_b[l].T, SGU_HEAD_DIM, axis=1)
        outs = _mixer_call(
            t, s, xf, moe, row2(norm_mix), w_in[l].astype(_BF16), pool_w[l].astype(_BF16),
            row2(pool_scale), row2(sgu_ln_g), row2(sgu_ln_b), sgu_w[l], sbias, conv_dw_w[l], row2(conv_dw_b),
            row2(conv_ln_g), row2(conv_ln_b), conv_pw[l].astype(_BF16))
        if moe is not None:
            ycat, xf = outs
        else:
            (ycat,) = outs
        pad = ROUTER_COLS - N_GROUPS - N_EXPERTS
        wr = jnp.concatenate([router_g_w[l], router_e_w[l], jnp.zeros((d, pad), _F32)], axis=1).astype(_BF16)
        br = jnp.concatenate([router_g_b[l], router_e_b[l], jnp.zeros((pad,), _F32)]).reshape(1, ROUTER_COLS)
        xf, hp, rec, counts = _out_call(t, ycat, xf, w_out[l].astype(_BF16), row2(norm_ffn), wr, br)
        n_blocks, block_e, block_cnt, n_valid, pad_start, pad_rows, dest = _slot_tables(t, rec, counts)
        xs = _dispatch_call(t, n_blocks * BM, pad_start, pad_rows, dest, hp)
        ys = _expert_call(l, n_blocks, block_e, block_cnt, n_valid, xs, exp_w1, exp_w3, exp_w2)
        moe = (rec, ys, dest)
    out = _final_call(t, xf, moe, norm_final.reshape(1, -1))
    return out.reshape(b, s, d)
```

```python
import functools

import jax
import jax.numpy as jnp
from jax import lax
from jax.experimental import pallas as pl
from jax.experimental.pallas import tpu as pltpu

D_MODEL = 2048
D_A = D_MODEL // 4
D_B = 3 * D_MODEL // 8
D_C = D_MODEL - D_A - D_B
POOL_WINDOWS = (2, 4, 8, 16)
POOL_GROUP_DIM = D_A // len(POOL_WINDOWS)
CHUNK = 128
SGU_HEAD_DIM = 128
SGU_HEADS = D_B // SGU_HEAD_DIM
CONV_WIDTH = 31
N_IN = D_A + 2 * D_B + 2 * D_C
N_GROUPS = 4
E_PER_GROUP = 8
N_EXPERTS = N_GROUPS * E_PER_GROUP
TOP_K = 2
D_EXPERT = D_MODEL // 4
EPS = 1e-6

LANES = 128
SUBLANES = 8
POOL_HALO = 16
CONV_HALO = 32
CONV_ROWS = 64
CONV_PITCH = 2
TM_MIX = 256
TM_OUT = 256
BM = 256
TOK_TILE_ROWS = D_MODEL // 2 // LANES
ROUTER_COLS = LANES
REC_EXPERT = 0
REC_WEIGHT = TOP_K
REC_RANK = 2 * TOP_K
VMEM_LIMIT = 56 * 1024 * 1024

_F32 = jnp.float32
_BF16 = jnp.bfloat16


def _resident(shape):
    nd = len(shape)
    return pl.BlockSpec(shape, lambda *_: (0,) * nd, pipeline_mode=pl.Buffered(1))


def _rms_scale(x):
    return lax.rsqrt(jnp.mean(x * x, axis=-1, keepdims=True) + EPS)


def _layer_norm(x, g, b):
    mu = jnp.mean(x, axis=-1, keepdims=True)
    xc = x - mu
    return xc * lax.rsqrt(jnp.mean(xc * xc, axis=-1, keepdims=True) + EPS) * g + b


def _gelu_tanh(x):
    return 0.5 * x * (1.0 + jnp.tanh(0.7978845608028654 * (x + 0.044715 * (x * x * x))))


def _sigmoid(x):
    return 1.0 / (1.0 + jnp.exp(-x))


def _pack_bf16_pairs(hb):
    u = lax.bitcast_convert_type(hb.astype(_F32), jnp.uint32)
    c = hb.shape[1] // 2
    return (u[:, c:] & jnp.uint32(0xFFFF0000)) | (u[:, :c] >> 16)


def _unpack_pairs_f32(p):
    return (lax.bitcast_convert_type(p << 16, _F32), lax.bitcast_convert_type(p & jnp.uint32(0xFFFF0000), _F32))


def _unpack_bf16_pairs(p):
    lo, hi = _unpack_pairs_f32(p)
    return lo.astype(_BF16), hi.astype(_BF16)


def _store_token_tiles(ref, packed):
    rows = packed.shape[0]
    for s in range(TOK_TILE_ROWS):
        ref[pl.ds(s, rows, stride=TOK_TILE_ROWS), :] = packed[:, s * LANES:(s + 1) * LANES]


def _load_token_tiles(ref, rows):
    return jnp.concatenate([ref[pl.ds(s, rows, stride=TOK_TILE_ROWS), :] for s in range(TOK_TILE_ROWS)], axis=1)


def _start_expert_rows(y_hbm, dest_ref, ybuf, sem, slot):
    tm = ybuf.shape[2] // TOK_TILE_ROWS
    for j in range(tm):
        for k in range(TOP_K):
            pltpu.make_async_copy(y_hbm.at[dest_ref[0, 0, TOP_K * j + k]],
                                  ybuf.at[slot, k, pl.ds(j * TOK_TILE_ROWS, TOK_TILE_ROWS)], sem.at[slot]).start()


def _wait_expert_rows(ybuf, sem, slot):
    pltpu.make_async_copy(ybuf.at[slot], ybuf.at[slot], sem.at[slot]).wait()


def _combined_input(x_ref, rec_ref, y_hbm, dest_ref, dest_next_ref, ybuf, sem):
    step = pl.program_id(0)
    slot = lax.rem(step, 2)

    @pl.when(step == 0)
    def _():
        _start_expert_rows(y_hbm, dest_ref, ybuf, sem, 0)

    for s in range(2):
        @pl.when((step < pl.num_programs(0) - 1) & (slot == s))
        def _(s=s):
            _start_expert_rows(y_hbm, dest_next_ref, ybuf, sem, 1 - s)

    _wait_expert_rows(ybuf, sem, slot)
    rec = rec_ref[...]
    tm = x_ref.shape[0]
    moe = None
    for k in range(TOP_K):
        yk = jnp.concatenate(_unpack_pairs_f32(_load_token_tiles(ybuf.at[slot, k], tm)), axis=1)
        term = rec[:, REC_WEIGHT + k:REC_WEIGHT + k + 1] * yk
        moe = term if moe is None else moe + term
    return x_ref[...] + moe


def _combine_specs(tm, n_tiles):
    row = lambda i: (i, 0)
    in_specs = [
        pl.BlockSpec((tm, D_MODEL), row),
        pl.BlockSpec((tm, ROUTER_COLS), row),
        pl.BlockSpec(memory_space=pl.ANY),
        pl.BlockSpec((1, 1, TOP_K * tm), lambda i: (i, 0, 0), memory_space=pltpu.SMEM),
        pl.BlockSpec((1, 1, TOP_K * tm), lambda i: (jnp.minimum(i + 1, n_tiles - 1), 0, 0), memory_space=pltpu.SMEM),
    ]
    scratch = [pltpu.VMEM((2, TOP_K, tm * TOK_TILE_ROWS, LANES), jnp.uint32), pltpu.SemaphoreType.DMA((2,))]
    return in_specs, scratch


def _mixer_kernel(combine, tiles_per_seq, *refs):
    if combine:
        x_ref, rec_ref, y_hbm, dest_ref, dest_next_ref = refs[:5]
        refs = refs[5:]
    else:
        x_ref = refs[0]
        refs = refs[1:]
    (g_ref, win_ref, poolw_ref, pscale_ref, sg_ref, sb_ref, sw_ref, sbias_ref,
     dww_ref, dwb_ref, cg_ref, cb_ref, pw_ref) = refs[:13]
    refs = refs[13:]
    if combine:
        ycat_ref, xo_ref = refs[:2]
        h_s, xa_s, z_s, cv_s, ybuf, ysem = refs[2:]
    else:
        ycat_ref = refs[0]
        h_s, xa_s, z_s, cv_s = refs[1:]
    tm = ycat_ref.shape[0]

    tile = pl.program_id(0) % tiles_per_seq
    first = tile == 0

    if combine:
        x = _combined_input(x_ref, rec_ref, y_hbm, dest_ref, dest_next_ref, ybuf, ysem)
        xo_ref[...] = x
    else:
        x = x_ref[...]
    n_ct = D_C // LANES
    halo_src = pl.ds(CONV_PITCH * tm, CONV_HALO, stride=CONV_PITCH)
    halo_dst = pl.ds(0, CONV_HALO, stride=CONV_PITCH)

    @pl.when(first)
    def _():
        xa_s[0:POOL_HALO, :] = jnp.zeros((POOL_HALO, D_A), _F32)
        for c in range(n_ct):
            z_s[c, halo_dst, :] = jnp.zeros((CONV_HALO, LANES), _F32)

    @pl.when(jnp.logical_not(first))
    def _():
        xa_s[0:POOL_HALO, :] = xa_s[tm:tm + POOL_HALO, :]
        for c in range(n_ct):
            z_s[c, halo_dst, :] = z_s[c, halo_src, :]

    h_s[...] = (x * _rms_scale(x) * g_ref[...]).astype(_BF16)

    pc = jnp.dot(h_s[...], win_ref[:, D_A + 2 * D_B:N_IN], preferred_element_type=_F32)
    zc = pc[:, :D_C] * _sigmoid(pc[:, D_C:])
    for c in range(n_ct):
        z_s[c, pl.ds(CONV_PITCH * CONV_HALO, tm, stride=CONV_PITCH), :] = zc[:, c * LANES:(c + 1) * LANES]
    for r0 in range(0, tm, CONV_ROWS):
        for c in range(n_ct):
            lanes = slice(c * LANES, (c + 1) * LANES)
            acc = jnp.zeros((CONV_ROWS, LANES), _F32)
            for k in range(CONV_WIDTH):
                off = r0 + CONV_HALO - (CONV_WIDTH - 1) + k
                acc = acc + dww_ref[k:k + 1, lanes] * z_s[c, pl.ds(CONV_PITCH * off, CONV_ROWS, stride=CONV_PITCH), :]
            cv_s[r0:r0 + CONV_ROWS, lanes] = acc + dwb_ref[:, lanes]

    xa_s[POOL_HALO:POOL_HALO + tm, :] = jnp.dot(h_s[...], win_ref[:, 0:D_A], preferred_element_type=_F32)
    pos = tile * tm + lax.broadcasted_iota(jnp.int32, (tm, 1), 0)
    for gi, w in enumerate(POOL_WINDOWS):
        cols = slice(gi * POOL_GROUP_DIM, (gi + 1) * POOL_GROUP_DIM)
        xg = xa_s[POOL_HALO:POOL_HALO + tm, cols]
        acc = xg
        for k in range(1, w):
            acc = acc + xa_s[POOL_HALO - k:POOL_HALO - k + tm, cols]
        cnt = jnp.minimum(pos + 1, w).astype(_F32)
        d = acc / cnt - xg
        yg = jnp.dot(d.astype(_BF16), poolw_ref[gi], preferred_element_type=_F32) * pscale_ref[:, cols]
        ycat_ref[:, cols] = yg.astype(_BF16)

    pb = jnp.dot(h_s[...], win_ref[:, D_A:D_A + 2 * D_B], preferred_element_type=_F32)
    zb = _gelu_tanh(pb)
    u = zb[:, :D_B]
    vb = _layer_norm(zb[:, D_B:], sg_ref[...], sb_ref[...]).astype(_BF16)
    r_i = lax.broadcasted_iota(jnp.int32, (CHUNK, CHUNK), 0)
    c_i = lax.broadcasted_iota(jnp.int32, (CHUNK, CHUNK), 1)
    causal = r_i >= c_i
    for hd in range(SGU_HEADS):
        hc = slice(hd * SGU_HEAD_DIM, (hd + 1) * SGU_HEAD_DIM)
        wm = jnp.where(causal, sw_ref[hd], 0.0).astype(_BF16)
        for c in range(tm // CHUNK):
            rows = slice(c * CHUNK, (c + 1) * CHUNK)
            gate = jnp.dot(wm, vb[rows, hc], preferred_element_type=_F32) + sbias_ref[:, hc]
            ycat_ref[rows, D_A + hd * SGU_HEAD_DIM:D_A + (hd + 1) * SGU_HEAD_DIM] = (u[rows, hc] * gate).astype(_BF16)

    zn = _layer_norm(cv_s[...], cg_ref[...], cb_ref[...])
    zs = (zn * _sigmoid(zn)).astype(_BF16)
    ycat_ref[:, D_A + D_B:D_MODEL] = jnp.dot(zs, pw_ref[...], preferred_element_type=_F32).astype(_BF16)


def _mixer_call(t, seq, x, moe, g, win, poolw, pscale, sg, sb, sw, sbias, dww, dwb, cg, cb, pw):
    tm = TM_MIX
    n_tiles = t // tm
    row = lambda i: (i, 0)
    combine = moe is not None
    scratch = [
        pltpu.VMEM((tm, D_MODEL), _BF16),
        pltpu.VMEM((POOL_HALO + tm, D_A), _F32),
        pltpu.VMEM((D_C // LANES, CONV_PITCH * (CONV_HALO + tm), LANES), _F32),
        pltpu.VMEM((tm, D_C), _F32),
    ]
    if combine:
        rec, ys, dest = moe
        in_specs, extra = _combine_specs(tm, n_tiles)
        scratch += extra
        args = [x, rec, ys, dest, dest]
    else:
        in_specs = [pl.BlockSpec((tm, D_MODEL), row)]
        args = [x]
    weights = [g, win, poolw, pscale, sg, sb, sw, sbias, dww, dwb, cg, cb, pw]
    in_specs += [_resident(w.shape) for w in weights]
    args += weights
    out_shape = [jax.ShapeDtypeStruct((t, D_MODEL), _BF16)]
    out_specs = [pl.BlockSpec((tm, D_MODEL), row)]
    if combine:
        out_shape.append(jax.ShapeDtypeStruct((t, D_MODEL), _F32))
        out_specs.append(pl.BlockSpec((tm, D_MODEL), row))
    return pl.pallas_call(
        functools.partial(_mixer_kernel, combine, seq // tm),
        grid=(n_tiles,),
        in_specs=in_specs,
        out_specs=out_specs,
        out_shape=out_shape,
        scratch_shapes=scratch,
        compiler_params=pltpu.CompilerParams(dimension_semantics=("arbitrary",), vmem_limit_bytes=VMEM_LIMIT),
        name="mixer_combine" if combine else "mixer",
    )(*args)


def _route_rows(logits, seen):
    rows = logits.shape[0]
    lane = lax.broadcasted_iota(jnp.int32, logits.shape, 1)
    neg = jnp.float32(-jnp.inf)
    big = jnp.int32(ROUTER_COLS)

    def first_max(v):
        m = jnp.max(v, axis=-1, keepdims=True)
        return m, jnp.min(jnp.where(v == m, lane, big), axis=-1, keepdims=True)

    lg = jnp.where(lane < N_GROUPS, logits, neg)
    mg, g_idx = first_max(lg)
    g_w = 1.0 / jnp.sum(jnp.exp(lg - mg), axis=-1, keepdims=True)
    e_lo = N_GROUPS + g_idx * E_PER_GROUP
    le = jnp.where((lane >= e_lo) & (lane < e_lo + E_PER_GROUP), logits, neg)
    m1, i1 = first_max(le)
    m2, i2 = first_max(jnp.where(lane == i1, neg, le))
    p2 = jnp.exp(m2 - m1)
    w1 = g_w / (1.0 + p2)
    w2 = g_w * p2 / (1.0 + p2)
    e1 = i1 - N_GROUPS
    e2 = i2 - N_GROUPS

    hit1 = lane == e1
    hit2 = lane == e2
    onehot = jnp.where(hit1, 1.0, 0.0) + jnp.where(hit2, 1.0, 0.0)
    r_i = lax.broadcasted_iota(jnp.int32, (rows, rows), 0)
    c_i = lax.broadcasted_iota(jnp.int32, (rows, rows), 1)
    earlier = jnp.where(r_i > c_i, 1.0, 0.0).astype(_BF16)
    before = seen + jnp.dot(earlier, onehot.astype(_BF16), preferred_element_type=_F32)
    rank1 = jnp.sum(jnp.where(hit1, before, 0.0), axis=-1, keepdims=True)
    rank2 = jnp.sum(jnp.where(hit2, before, 0.0), axis=-1, keepdims=True)
    seen = seen + jnp.sum(onehot, axis=0, keepdims=True)

    rec = jnp.zeros(logits.shape, _F32)
    for k, v in ((REC_EXPERT, e1.astype(_F32)), (REC_EXPERT + 1, e2.astype(_F32)), (REC_WEIGHT, w1),
                 (REC_WEIGHT + 1, w2), (REC_RANK, rank1), (REC_RANK + 1, rank2)):
        rec = jnp.where(lane == k, v, rec)
    return rec, seen


def _out_kernel(ycat_ref, x_ref, wout_ref, g_ref, wr_ref, br_ref, x1_ref, hp_ref, rec_ref, cnt_ref, lg_s, seen_s):
    step = pl.program_id(0)
    slot = lax.rem(step, 2)

    @pl.when(step == 0)
    def _():
        lg_s[...] = jnp.zeros(lg_s.shape, _F32)
        seen_s[...] = jnp.zeros(seen_s.shape, _F32)

    prev_logits = lg_s[1 - slot]
    prev_seen = seen_s[...]

    x1 = x_ref[...] + jnp.dot(ycat_ref[...], wout_ref[...], preferred_element_type=_F32)
    x1_ref[...] = x1
    hb = (x1 * _rms_scale(x1) * g_ref[...]).astype(_BF16)
    lg_s[slot] = jnp.dot(hb, wr_ref[...], preferred_element_type=_F32) + br_ref[...]

    rec, seen = _route_rows(prev_logits, prev_seen)
    rec_ref[...] = rec
    seen = jnp.where(step == 0, 0.0, seen)
    seen_s[...] = seen
    cnt_ref[...] = jnp.broadcast_to(seen, cnt_ref.shape)
    _store_token_tiles(hp_ref, _pack_bf16_pairs(hb))


def _out_call(t, ycat, x, wout, g, wr, br):
    tm = TM_OUT
    n_tiles = t // tm
    row = lambda i: (jnp.minimum(i, n_tiles - 1), 0)
    prev = lambda i: (jnp.maximum(i - 1, 0), 0)
    return pl.pallas_call(
        _out_kernel,
        grid=(n_tiles + 1,),
        in_specs=[pl.BlockSpec((tm, D_MODEL), row), pl.BlockSpec((tm, D_MODEL), row),
                  _resident(wout.shape), _resident(g.shape), _resident(wr.shape), _resident(br.shape)],
        out_specs=[pl.BlockSpec((tm, D_MODEL), row), pl.BlockSpec((tm * TOK_TILE_ROWS, LANES), row),
                   pl.BlockSpec((tm, ROUTER_COLS), prev), pl.BlockSpec((SUBLANES, ROUTER_COLS), lambda i: (0, 0))],
        out_shape=[jax.ShapeDtypeStruct((t, D_MODEL), _F32),
                   jax.ShapeDtypeStruct((t * TOK_TILE_ROWS, LANES), jnp.uint32),
                   jax.ShapeDtypeStruct((t, ROUTER_COLS), _F32),
                   jax.ShapeDtypeStruct((SUBLANES, ROUTER_COLS), _F32)],
        scratch_shapes=[pltpu.VMEM((2, tm, ROUTER_COLS), _F32), pltpu.VMEM((1, ROUTER_COLS), _F32)],
        compiler_params=pltpu.CompilerParams(dimension_semantics=("arbitrary",), vmem_limit_bytes=VMEM_LIMIT),
        name="out_proj",
    )(ycat, x, wout, g, wr, br)


def _dispatch_kernel(pstart_ref, pad_start_ref, pad_rows_ref, expert_ref, rank_ref, hp_hbm, xs_hbm, dest_ref,
                     buf, zero_s, lsem, sem, zsem):
    step = pl.program_id(0)
    last = pl.num_programs(0) - 1
    tm = expert_ref.shape[2] // TOP_K
    slot = lax.rem(step, 2)

    def stage(tile, to_slot):
        return pltpu.make_async_copy(hp_hbm.at[pl.ds(tile * tm, tm)], buf.at[to_slot], lsem.at[to_slot])

    def wait_tile():
        view = xs_hbm.at[pl.ds(0, TOP_K * tm)]
        pltpu.make_async_copy(view, view, sem).wait()

    @pl.when(step == 0)
    def _():
        stage(0, 0).start()
        zero_s[...] = jnp.zeros(zero_s.shape, zero_s.dtype)
        for e in range(pad_rows_ref.shape[0]):
            n = pad_rows_ref[e]

            @pl.when(n > 0)
            def _():
                pltpu.make_async_copy(zero_s.at[pl.ds(0, n)], xs_hbm.at[pl.ds(pad_start_ref[e], n)], zsem).start()

        for e in range(pad_rows_ref.shape[0]):
            n = pad_rows_ref[e]

            @pl.when(n > 0)
            def _():
                pltpu.make_async_copy(zero_s.at[pl.ds(0, n)], xs_hbm.at[pl.ds(pad_start_ref[e], n)], zsem).wait()

    stage(step, slot).wait()

    @pl.when(step > 0)
    def _():
        wait_tile()

    @pl.when(step < last)
    def _():
        stage(step + 1, 1 - slot).start()

    for j in range(tm):
        for k in range(TOP_K):
            a = TOP_K * j + k
            row = pstart_ref[expert_ref[0, 0, a]] + rank_ref[0, 0, a]
            dest_ref[0, 0, a] = row
            pltpu.make_async_copy(buf.at[slot, j], xs_hbm.at[row], sem).start(priority=k % 2)

    @pl.when(step == last)
    def _():
        wait_tile()


def _dispatch_call(t, p_len, pstart, pad_start, pad_rows, expert, rank, hp):
    tm = TM_OUT
    table = pl.BlockSpec((1, 1, TOP_K * tm), lambda i, *_: (i, 0, 0), memory_space=pltpu.SMEM)
    grid_spec = pltpu.PrefetchScalarGridSpec(
        num_scalar_prefetch=3,
        grid=(t // tm,),
        in_specs=[table, table, pl.BlockSpec(memory_space=pl.ANY)],
        out_specs=[pl.BlockSpec(memory_space=pl.ANY), table],
        scratch_shapes=[pltpu.VMEM((2, tm, TOK_TILE_ROWS, LANES), jnp.uint32),
                        pltpu.VMEM((BM, TOK_TILE_ROWS, LANES), jnp.uint32),
                        pltpu.SemaphoreType.DMA((2,)), pltpu.SemaphoreType.DMA(()), pltpu.SemaphoreType.DMA(())],
    )
    return pl.pallas_call(
        _dispatch_kernel,
        grid_spec=grid_spec,
        out_shape=[jax.ShapeDtypeStruct((p_len, TOK_TILE_ROWS, LANES), jnp.uint32),
                   jax.ShapeDtypeStruct(expert.shape, jnp.int32)],
        compiler_params=pltpu.CompilerParams(dimension_semantics=("arbitrary",), vmem_limit_bytes=VMEM_LIMIT),
        name="dispatch",
    )(pstart, pad_start, pad_rows, expert, rank, hp.reshape(t, TOK_TILE_ROWS, LANES))


def _expert_kernel(layer, be_ref, cnt_ref, nv_ref, first_ref, par_ref, nxt_ref, xs_ref, w1_hbm, w3_hbm, w2_hbm,
                   y_ref, w1_s, w3_s, w2_s, wsem):
    del nv_ref
    half = D_MODEL // 2
    step = pl.program_id(0)
    rows = cnt_ref[step]
    slot = par_ref[step]

    def weight_copies(e, to_slot):
        return (pltpu.make_async_copy(w1_hbm.at[layer, e], w1_s.at[to_slot], wsem.at[to_slot]),
                pltpu.make_async_copy(w3_hbm.at[layer, e], w3_s.at[to_slot], wsem.at[to_slot]),
                pltpu.make_async_copy(w2_hbm.at[layer, e], w2_s.at[to_slot], wsem.at[to_slot]))

    @pl.when(step == 0)
    def _():
        for cp in weight_copies(be_ref[0], slot):
            cp.start()

    @pl.when(first_ref[step] == 1)
    def _():
        for cp in weight_copies(be_ref[step], slot):
            cp.wait()

        @pl.when(nxt_ref[step] >= 0)
        def _():
            for cp in weight_copies(nxt_ref[step], 1 - slot):
                cp.start()

    @pl.when(rows == 0)
    def _():
        y_ref[...] = jnp.zeros(y_ref.shape, y_ref.dtype)

    @pl.when(rows > 0)
    def _():
        lo, hi = _unpack_bf16_pairs(_load_token_tiles(xs_ref, BM))
        a1 = (jnp.dot(lo, w1_s[slot, 0:half, :].astype(_BF16), preferred_element_type=_F32)
              + jnp.dot(hi, w1_s[slot, half:D_MODEL, :].astype(_BF16), preferred_element_type=_F32))
        a3 = (jnp.dot(lo, w3_s[slot, 0:half, :].astype(_BF16), preferred_element_type=_F32)
              + jnp.dot(hi, w3_s[slot, half:D_MODEL, :].astype(_BF16), preferred_element_type=_F32))
        hid = (a1 * _sigmoid(a1) * a3).astype(_BF16)
        y = jnp.dot(hid, w2_s[slot].astype(_BF16), preferred_element_type=_F32)
        _store_token_tiles(y_ref, _pack_bf16_pairs(y.astype(_BF16)))


def _expert_call(layer, n_blocks, plan, xs, w1, w3, w2):
    blk = lambda i, be, cnt, nv, *_: (jnp.minimum(i, nv[0] - 1), 0)
    grid_spec = pltpu.PrefetchScalarGridSpec(
        num_scalar_prefetch=len(plan),
        grid=(n_blocks,),
        in_specs=[
            pl.BlockSpec((BM * TOK_TILE_ROWS, LANES), blk),
            pl.BlockSpec(memory_space=pl.ANY),
            pl.BlockSpec(memory_space=pl.ANY),
            pl.BlockSpec(memory_space=pl.ANY),
        ],
        out_specs=pl.BlockSpec((BM * TOK_TILE_ROWS, LANES), lambda i, *_: (i, 0)),
        scratch_shapes=[
            pltpu.VMEM((2, D_MODEL, D_EXPERT), w1.dtype),
            pltpu.VMEM((2, D_MODEL, D_EXPERT), w3.dtype),
            pltpu.VMEM((2, D_EXPERT, D_MODEL), w2.dtype),
            pltpu.SemaphoreType.DMA((2,)),
        ],
    )
    ys = pl.pallas_call(
        functools.partial(_expert_kernel, layer),
        grid_spec=grid_spec,
        out_shape=jax.ShapeDtypeStruct((n_blocks * BM * TOK_TILE_ROWS, LANES), jnp.uint32),
        compiler_params=pltpu.CompilerParams(dimension_semantics=("arbitrary",), vmem_limit_bytes=VMEM_LIMIT),
        name="experts",
    )(*plan, xs.reshape(n_blocks * BM * TOK_TILE_ROWS, LANES), w1, w3, w2)
    return ys.reshape(n_blocks * BM, TOK_TILE_ROWS, LANES)


def _slot_tables(t, rec, counts_f):
    n_assign = t * TOP_K
    n_blocks = n_assign // BM + N_EXPERTS
    counts = counts_f[0, :N_EXPERTS].astype(jnp.int32)
    padded = ((counts + BM - 1) // BM) * BM
    pend = jnp.cumsum(padded)
    pstart = pend - padded
    starts = jnp.arange(n_blocks, dtype=jnp.int32) * BM
    block_e = jnp.minimum(jnp.sum(pend[None, :] <= starts[:, None], axis=1), N_EXPERTS - 1).astype(jnp.int32)
    seg_end = (pstart + counts)[block_e]
    block_cnt = jnp.where(starts < pend[-1], jnp.clip(seg_end - starts, 0, BM), 0).astype(jnp.int32)
    n_valid = jnp.maximum(pend[-1] // BM, 1).astype(jnp.int32).reshape(1)
    real = starts < pend[-1]
    block_e = jnp.where(real, block_e, block_e[n_valid[0] - 1])
    first = (real & (block_e != jnp.concatenate([jnp.full((1,), -1, jnp.int32), block_e[:-1]]))).astype(jnp.int32)
    parity = jnp.maximum(jnp.cumsum(first) - 1, 0) % 2
    e_ids = jnp.arange(N_EXPERTS, dtype=jnp.int32)
    later = jnp.where((counts[None, :] > 0) & (e_ids[None, :] > e_ids[:, None]), e_ids[None, :], N_EXPERTS)
    next_e = jnp.min(later, axis=1)
    next_e = jnp.where(next_e < N_EXPERTS, next_e, -1)[block_e]
    plan = (block_e, block_cnt, n_valid, first, parity.astype(jnp.int32), next_e.astype(jnp.int32))
    shape = (t // TM_OUT, 1, TOP_K * TM_OUT)
    expert = rec[:, REC_EXPERT:REC_EXPERT + TOP_K].astype(jnp.int32).reshape(shape)
    rank = rec[:, REC_RANK:REC_RANK + TOP_K].astype(jnp.int32).reshape(shape)
    tail = pend[-1] + jnp.arange(N_EXPERTS, dtype=jnp.int32) * BM
    pad_start = jnp.concatenate([pstart + counts, tail])
    pad_rows = jnp.concatenate([padded - counts, jnp.where(tail < n_blocks * BM, BM, 0)])
    return n_blocks, plan, pstart.astype(jnp.int32), pad_start, pad_rows.astype(jnp.int32), expert, rank


def _final_kernel(x_ref, rec_ref, y_hbm, dest_ref, dest_next_ref, g_ref, o_ref, ybuf, ysem):
    x = _combined_input(x_ref, rec_ref, y_hbm, dest_ref, dest_next_ref, ybuf, ysem)
    o_ref[...] = x * _rms_scale(x) * g_ref[...]


def _final_call(t, x, moe, g):
    tm = TM_OUT
    n_tiles = t // tm
    rec, ys, dest = moe
    in_specs, scratch = _combine_specs(tm, n_tiles)
    return pl.pallas_call(
        _final_kernel,
        grid=(n_tiles,),
        in_specs=in_specs + [_resident(g.shape)],
        out_specs=pl.BlockSpec((tm, D_MODEL), lambda i: (i, 0)),
        out_shape=jax.ShapeDtypeStruct((t, D_MODEL), _F32),
        scratch_shapes=scratch,
        compiler_params=pltpu.CompilerParams(dimension_semantics=("arbitrary",), vmem_limit_bytes=VMEM_LIMIT),
        name="final_norm",
    )(x, rec, ys, dest, dest, g)


def kernel(x, norm_mix, w_in, pool_w, pool_scale, sgu_ln_g, sgu_ln_b, sgu_w, sgu_b, conv_dw_w, conv_dw_b,
           conv_ln_g, conv_ln_b, conv_pw, w_out, norm_ffn, router_g_w, router_g_b, router_e_w, router_e_b,
           exp_w1, exp_w3, exp_w2, norm_final):
    b, s, d = x.shape
    depth = w_in.shape[0]
    assert d == D_MODEL and s % TM_MIX == 0 and (b * s) % TM_OUT == 0 and TM_MIX % CHUNK == 0 and TM_MIX == TM_OUT
    t = b * s
    xf = x.reshape(t, d)
    moe = None
    for l in range(depth):
        row2 = lambda v: v[l].reshape(1, -1)
        sbias = jnp.repeat(sgu_b[l].T, SGU_HEAD_DIM, axis=1)
        outs = _mixer_call(
            t, s, xf, moe, row2(norm_mix), w_in[l].astype(_BF16), pool_w[l].astype(_BF16),
            row2(pool_scale), row2(sgu_ln_g), row2(sgu_ln_b), sgu_w[l], sbias, conv_dw_w[l], row2(conv_dw_b),
            row2(conv_ln_g), row2(conv_ln_b), conv_pw[l].astype(_BF16))
        if moe is not None:
            ycat, xf = outs
        else:
            (ycat,) = outs
        pad = ROUTER_COLS - N_GROUPS - N_EXPERTS
        wr = jnp.concatenate([router_g_w[l], router_e_w[l], jnp.zeros((d, pad), _F32)], axis=1).astype(_BF16)
        br = jnp.concatenate([router_g_b[l], router_e_b[l], jnp.zeros((pad,), _F32)]).reshape(1, ROUTER_COLS)
        xf, hp, rec, counts = _out_call(t, ycat, xf, w_out[l].astype(_BF16), row2(norm_ffn), wr, br)
        n_blocks, plan, pstart, pad_start, pad_rows, expert, rank = _slot_tables(t, rec, counts)
        xs, dest = _dispatch_call(t, n_blocks * BM, pstart, pad_start, pad_rows, expert, rank, hp)
        ys = _expert_call(l, n_blocks, plan, xs, exp_w1, exp_w3, exp_w2)
        moe = (rec, ys, dest)
    out = _final_call(t, xf, moe, norm_final.reshape(1, -1))
    return out.reshape(b, s, d)
```

```python
import functools

import jax
import jax.numpy as jnp
from jax import lax
from jax.experimental import pallas as pl
from jax.experimental.pallas import tpu as pltpu

D_MODEL = 2048
D_A = D_MODEL // 4
D_B = 3 * D_MODEL // 8
D_C = D_MODEL - D_A - D_B
POOL_WINDOWS = (2, 4, 8, 16)
POOL_GROUP_DIM = D_A // len(POOL_WINDOWS)
CHUNK = 128
SGU_HEAD_DIM = 128
SGU_HEADS = D_B // SGU_HEAD_DIM
CONV_WIDTH = 31
N_IN = D_A + 2 * D_B + 2 * D_C
N_GROUPS = 4
E_PER_GROUP = 8
N_EXPERTS = N_GROUPS * E_PER_GROUP
TOP_K = 2
D_EXPERT = D_MODEL // 4
EPS = 1e-6

LANES = 128
SUBLANES = 8
POOL_HALO = 16
CONV_HALO = 32
CONV_ROWS = 64
CONV_PITCH = 2
TM_MIX = 256
TM_OUT = 256
TM_PROJ = 512
BM = 256
TOK_TILE_ROWS = D_MODEL // 2 // LANES
ROUTER_COLS = LANES
REC_EXPERT = 0
REC_WEIGHT = TOP_K
REC_RANK = 2 * TOP_K
VMEM_LIMIT = 56 * 1024 * 1024

_F32 = jnp.float32
_BF16 = jnp.bfloat16


def _resident(shape):
    nd = len(shape)
    return pl.BlockSpec(shape, lambda *_: (0,) * nd, pipeline_mode=pl.Buffered(1))


def _rms_scale(x):
    return lax.rsqrt(jnp.mean(x * x, axis=-1, keepdims=True) + EPS)


def _layer_norm(x, g, b):
    mu = jnp.mean(x, axis=-1, keepdims=True)
    xc = x - mu
    return xc * lax.rsqrt(jnp.mean(xc * xc, axis=-1, keepdims=True) + EPS) * g + b


def _gelu_tanh(x):
    return 0.5 * x * (1.0 + jnp.tanh(0.7978845608028654 * (x + 0.044715 * (x * x * x))))


def _sigmoid(x):
    return 1.0 / (1.0 + jnp.exp(-x))


def _pack_bf16_pairs(hb):
    u = lax.bitcast_convert_type(hb.astype(_F32), jnp.uint32)
    c = hb.shape[1] // 2
    return (u[:, c:] & jnp.uint32(0xFFFF0000)) | (u[:, :c] >> 16)


def _unpack_pairs_f32(p):
    return (lax.bitcast_convert_type(p << 16, _F32), lax.bitcast_convert_type(p & jnp.uint32(0xFFFF0000), _F32))


def _unpack_bf16_pairs(p):
    lo, hi = _unpack_pairs_f32(p)
    return lo.astype(_BF16), hi.astype(_BF16)


def _store_token_tiles(ref, packed):
    rows = packed.shape[0]
    for s in range(TOK_TILE_ROWS):
        ref[pl.ds(s, rows, stride=TOK_TILE_ROWS), :] = packed[:, s * LANES:(s + 1) * LANES]


def _load_token_tiles(ref, rows):
    return jnp.concatenate([ref[pl.ds(s, rows, stride=TOK_TILE_ROWS), :] for s in range(TOK_TILE_ROWS)], axis=1)


def _start_expert_rows(y_hbm, dest_ref, ybuf, sem, slot):
    tm = ybuf.shape[2] // TOK_TILE_ROWS
    for j in range(tm):
        for k in range(TOP_K):
            pltpu.make_async_copy(y_hbm.at[dest_ref[0, 0, TOP_K * j + k]],
                                  ybuf.at[slot, k, pl.ds(j * TOK_TILE_ROWS, TOK_TILE_ROWS)], sem.at[slot]).start()


def _wait_expert_rows(ybuf, sem, slot):
    pltpu.make_async_copy(ybuf.at[slot], ybuf.at[slot], sem.at[slot]).wait()


def _fetch_expert_rows(y_hbm, dest_ref, dest_next_ref, ybuf, sem):
    step = pl.program_id(0)
    slot = lax.rem(step, 2)

    @pl.when(step == 0)
    def _():
        _start_expert_rows(y_hbm, dest_ref, ybuf, sem, 0)

    for s in range(2):
        @pl.when((step < pl.num_programs(0) - 1) & (slot == s))
        def _(s=s):
            _start_expert_rows(y_hbm, dest_next_ref, ybuf, sem, 1 - s)

    _wait_expert_rows(ybuf, sem, slot)


def _combine(x_ref, rec_ref, ybuf, slot):
    rec = rec_ref[...]
    tm = x_ref.shape[0]
    moe = None
    for k in range(TOP_K):
        yk = jnp.concatenate(_unpack_pairs_f32(_load_token_tiles(ybuf.at[slot, k], tm)), axis=1)
        term = rec[:, REC_WEIGHT + k:REC_WEIGHT + k + 1] * yk
        moe = term if moe is None else moe + term
    return x_ref[...] + moe


def _combined_input(x_ref, rec_ref, y_hbm, dest_ref, dest_next_ref, ybuf, sem):
    _fetch_expert_rows(y_hbm, dest_ref, dest_next_ref, ybuf, sem)
    return _combine(x_ref, rec_ref, ybuf, lax.rem(pl.program_id(0), 2))


def _combine_specs(tm, n_tiles):
    row = lambda i: (jnp.minimum(i, n_tiles - 1), 0)
    in_specs = [
        pl.BlockSpec((tm, D_MODEL), row),
        pl.BlockSpec((tm, ROUTER_COLS), row),
        pl.BlockSpec(memory_space=pl.ANY),
        pl.BlockSpec((1, 1, TOP_K * tm), lambda i: (jnp.minimum(i, n_tiles - 1), 0, 0), memory_space=pltpu.SMEM),
        pl.BlockSpec((1, 1, TOP_K * tm), lambda i: (jnp.minimum(i + 1, n_tiles - 1), 0, 0), memory_space=pltpu.SMEM),
    ]
    scratch = [pltpu.VMEM((2, TOP_K, tm * TOK_TILE_ROWS, LANES), jnp.uint32), pltpu.SemaphoreType.DMA((2,))]
    return in_specs, scratch


def _mixer_kernel(combine, tiles_per_seq, *refs):
    if combine:
        x_ref, rec_ref, y_hbm, dest_ref, dest_next_ref = refs[:5]
        refs = refs[5:]
    else:
        x_ref = refs[0]
        refs = refs[1:]
    (g_ref, win_ref, poolw_ref, pscale_ref, sg_ref, sb_ref, sw_ref, sbias_ref,
     dww_ref, dwb_ref, cg_ref, cb_ref, pw_ref) = refs[:13]
    refs = refs[13:]
    if combine:
        ycat_ref, xo_ref = refs[:2]
        h_s, xa_s, z_s, cv_s, ybuf, ysem = refs[2:]
    else:
        ycat_ref = refs[0]
        h_s, xa_s, z_s, cv_s = refs[1:]
    tm = ycat_ref.shape[0]

    tile = pl.program_id(0) % tiles_per_seq
    first = tile == 0

    if combine:
        x = _combined_input(x_ref, rec_ref, y_hbm, dest_ref, dest_next_ref, ybuf, ysem)
        xo_ref[...] = x
    else:
        x = x_ref[...]
    n_ct = D_C // LANES
    halo_src = pl.ds(CONV_PITCH * tm, CONV_HALO, stride=CONV_PITCH)
    halo_dst = pl.ds(0, CONV_HALO, stride=CONV_PITCH)

    @pl.when(first)
    def _():
        xa_s[0:POOL_HALO, :] = jnp.zeros((POOL_HALO, D_A), _F32)
        for c in range(n_ct):
            z_s[c, halo_dst, :] = jnp.zeros((CONV_HALO, LANES), _F32)

    @pl.when(jnp.logical_not(first))
    def _():
        xa_s[0:POOL_HALO, :] = xa_s[tm:tm + POOL_HALO, :]
        for c in range(n_ct):
            z_s[c, halo_dst, :] = z_s[c, halo_src, :]

    h_s[...] = (x * _rms_scale(x) * g_ref[...]).astype(_BF16)

    pc = jnp.dot(h_s[...], win_ref[:, D_A + 2 * D_B:N_IN], preferred_element_type=_F32)
    zc = pc[:, :D_C] * _sigmoid(pc[:, D_C:])
    for c in range(n_ct):
        z_s[c, pl.ds(CONV_PITCH * CONV_HALO, tm, stride=CONV_PITCH), :] = zc[:, c * LANES:(c + 1) * LANES]
    for r0 in range(0, tm, CONV_ROWS):
        for c in range(n_ct):
            lanes = slice(c * LANES, (c + 1) * LANES)
            acc = jnp.zeros((CONV_ROWS, LANES), _F32)
            for k in range(CONV_WIDTH):
                off = r0 + CONV_HALO - (CONV_WIDTH - 1) + k
                acc = acc + dww_ref[k:k + 1, lanes] * z_s[c, pl.ds(CONV_PITCH * off, CONV_ROWS, stride=CONV_PITCH), :]
            cv_s[r0:r0 + CONV_ROWS, lanes] = acc + dwb_ref[:, lanes]

    xa_s[POOL_HALO:POOL_HALO + tm, :] = jnp.dot(h_s[...], win_ref[:, 0:D_A], preferred_element_type=_F32)
    pos = tile * tm + lax.broadcasted_iota(jnp.int32, (tm, 1), 0)
    for gi, w in enumerate(POOL_WINDOWS):
        cols = slice(gi * POOL_GROUP_DIM, (gi + 1) * POOL_GROUP_DIM)
        xg = xa_s[POOL_HALO:POOL_HALO + tm, cols]
        acc = xg
        for k in range(1, w):
            acc = acc + xa_s[POOL_HALO - k:POOL_HALO - k + tm, cols]
        cnt = jnp.minimum(pos + 1, w).astype(_F32)
        d = acc / cnt - xg
        yg = jnp.dot(d.astype(_BF16), poolw_ref[gi], preferred_element_type=_F32) * pscale_ref[:, cols]
        ycat_ref[:, cols] = yg.astype(_BF16)

    pb = jnp.dot(h_s[...], win_ref[:, D_A:D_A + 2 * D_B], preferred_element_type=_F32)
    zb = _gelu_tanh(pb)
    u = zb[:, :D_B]
    vb = _layer_norm(zb[:, D_B:], sg_ref[...], sb_ref[...]).astype(_BF16)
    r_i = lax.broadcasted_iota(jnp.int32, (CHUNK, CHUNK), 0)
    c_i = lax.broadcasted_iota(jnp.int32, (CHUNK, CHUNK), 1)
    causal = r_i >= c_i
    for hd in range(SGU_HEADS):
        hc = slice(hd * SGU_HEAD_DIM, (hd + 1) * SGU_HEAD_DIM)
        wm = jnp.where(causal, sw_ref[hd], 0.0).astype(_BF16)
        for c in range(tm // CHUNK):
            rows = slice(c * CHUNK, (c + 1) * CHUNK)
            gate = jnp.dot(wm, vb[rows, hc], preferred_element_type=_F32) + sbias_ref[:, hc]
            ycat_ref[rows, D_A + hd * SGU_HEAD_DIM:D_A + (hd + 1) * SGU_HEAD_DIM] = (u[rows, hc] * gate).astype(_BF16)

    zn = _layer_norm(cv_s[...], cg_ref[...], cb_ref[...])
    zs = (zn * _sigmoid(zn)).astype(_BF16)
    ycat_ref[:, D_A + D_B:D_MODEL] = jnp.dot(zs, pw_ref[...], preferred_element_type=_F32).astype(_BF16)


def _mixer_call(t, seq, x, moe, g, win, poolw, pscale, sg, sb, sw, sbias, dww, dwb, cg, cb, pw):
    tm = TM_MIX
    n_tiles = t // tm
    row = lambda i: (i, 0)
    combine = moe is not None
    scratch = [
        pltpu.VMEM((tm, D_MODEL), _BF16),
        pltpu.VMEM((POOL_HALO + tm, D_A), _F32),
        pltpu.VMEM((D_C // LANES, CONV_PITCH * (CONV_HALO + tm), LANES), _F32),
        pltpu.VMEM((tm, D_C), _F32),
    ]
    if combine:
        rec, ys, dest = moe
        in_specs, extra = _combine_specs(tm, n_tiles)
        scratch += extra
        args = [x, rec, ys, dest, dest]
    else:
        in_specs = [pl.BlockSpec((tm, D_MODEL), row)]
        args = [x]
    weights = [g, win, poolw, pscale, sg, sb, sw, sbias, dww, dwb, cg, cb, pw]
    in_specs += [_resident(w.shape) for w in weights]
    args += weights
    out_shape = [jax.ShapeDtypeStruct((t, D_MODEL), _BF16)]
    out_specs = [pl.BlockSpec((tm, D_MODEL), row)]
    if combine:
        out_shape.append(jax.ShapeDtypeStruct((t, D_MODEL), _F32))
        out_specs.append(pl.BlockSpec((tm, D_MODEL), row))
    return pl.pallas_call(
        functools.partial(_mixer_kernel, combine, seq // tm),
        grid=(n_tiles,),
        in_specs=in_specs,
        out_specs=out_specs,
        out_shape=out_shape,
        scratch_shapes=scratch,
        compiler_params=pltpu.CompilerParams(dimension_semantics=("arbitrary",), vmem_limit_bytes=VMEM_LIMIT),
        name="mixer_combine" if combine else "mixer",
    )(*args)


def _route_rows(logits, seen):
    rows = logits.shape[0]
    lane = lax.broadcasted_iota(jnp.int32, logits.shape, 1)
    neg = jnp.float32(-jnp.inf)
    big = jnp.int32(ROUTER_COLS)

    def first_max(v):
        m = jnp.max(v, axis=-1, keepdims=True)
        return m, jnp.min(jnp.where(v == m, lane, big), axis=-1, keepdims=True)

    lg = jnp.where(lane < N_GROUPS, logits, neg)
    mg, g_idx = first_max(lg)
    g_w = 1.0 / jnp.sum(jnp.exp(lg - mg), axis=-1, keepdims=True)
    e_lo = N_GROUPS + g_idx * E_PER_GROUP
    le = jnp.where((lane >= e_lo) & (lane < e_lo + E_PER_GROUP), logits, neg)
    m1, i1 = first_max(le)
    m2, i2 = first_max(jnp.where(lane == i1, neg, le))
    p2 = jnp.exp(m2 - m1)
    w1 = g_w / (1.0 + p2)
    w2 = g_w * p2 / (1.0 + p2)
    e1 = i1 - N_GROUPS
    e2 = i2 - N_GROUPS

    hit1 = lane == e1
    hit2 = lane == e2
    onehot = jnp.where(hit1, 1.0, 0.0) + jnp.where(hit2, 1.0, 0.0)
    r_i = lax.broadcasted_iota(jnp.int32, (rows, rows), 0)
    c_i = lax.broadcasted_iota(jnp.int32, (rows, rows), 1)
    earlier = jnp.where(r_i > c_i, 1.0, 0.0).astype(_BF16)
    before = seen + jnp.dot(earlier, onehot.astype(_BF16), preferred_element_type=_F32)
    rank1 = jnp.sum(jnp.where(hit1, before, 0.0), axis=-1, keepdims=True)
    rank2 = jnp.sum(jnp.where(hit2, before, 0.0), axis=-1, keepdims=True)
    seen = seen + jnp.sum(onehot, axis=0, keepdims=True)

    rec = jnp.zeros(logits.shape, _F32)
    for k, v in ((REC_EXPERT, e1.astype(_F32)), (REC_EXPERT + 1, e2.astype(_F32)), (REC_WEIGHT, w1),
                 (REC_WEIGHT + 1, w2), (REC_RANK, rank1), (REC_RANK + 1, rank2)):
        rec = jnp.where(lane == k, v, rec)
    return rec, seen


def _out_kernel(ycat_ref, x_ref, wout_ref, g_ref, wr_ref, br_ref, x1_ref, hp_ref, rec_ref, cnt_ref, lg_s, seen_s):
    step = pl.program_id(0)
    slot = lax.rem(step, 2)

    @pl.when(step == 0)
    def _():
        lg_s[...] = jnp.zeros(lg_s.shape, _F32)
        seen_s[...] = jnp.zeros(seen_s.shape, _F32)

    prev_logits = lg_s[1 - slot]
    prev_seen = seen_s[...]

    x1 = x_ref[...] + jnp.dot(ycat_ref[...], wout_ref[...], preferred_element_type=_F32)
    x1_ref[...] = x1
    hb = (x1 * _rms_scale(x1) * g_ref[...]).astype(_BF16)
    lg_s[slot] = jnp.dot(hb, wr_ref[...], preferred_element_type=_F32) + br_ref[...]

    rec, seen = _route_rows(prev_logits, prev_seen)
    rec_ref[...] = rec
    seen = jnp.where(step == 0, 0.0, seen)
    seen_s[...] = seen
    cnt_ref[...] = jnp.broadcast_to(seen, cnt_ref.shape)
    _store_token_tiles(hp_ref, _pack_bf16_pairs(hb))


def _out_call(t, ycat, x, wout, g, wr, br):
    tm = TM_PROJ
    n_tiles = t // tm
    row = lambda i: (jnp.minimum(i, n_tiles - 1), 0)
    prev = lambda i: (jnp.maximum(i - 1, 0), 0)
    return pl.pallas_call(
        _out_kernel,
        grid=(n_tiles + 1,),
        in_specs=[pl.BlockSpec((tm, D_MODEL), row), pl.BlockSpec((tm, D_MODEL), row),
                  _resident(wout.shape), _resident(g.shape), _resident(wr.shape), _resident(br.shape)],
        out_specs=[pl.BlockSpec((tm, D_MODEL), row), pl.BlockSpec((tm * TOK_TILE_ROWS, LANES), row),
                   pl.BlockSpec((tm, ROUTER_COLS), prev), pl.BlockSpec((SUBLANES, ROUTER_COLS), lambda i: (0, 0))],
        out_shape=[jax.ShapeDtypeStruct((t, D_MODEL), _F32),
                   jax.ShapeDtypeStruct((t * TOK_TILE_ROWS, LANES), jnp.uint32),
                   jax.ShapeDtypeStruct((t, ROUTER_COLS), _F32),
                   jax.ShapeDtypeStruct((SUBLANES, ROUTER_COLS), _F32)],
        scratch_shapes=[pltpu.VMEM((2, tm, ROUTER_COLS), _F32), pltpu.VMEM((1, ROUTER_COLS), _F32)],
        compiler_params=pltpu.CompilerParams(dimension_semantics=("arbitrary",), vmem_limit_bytes=VMEM_LIMIT),
        name="out_proj",
    )(ycat, x, wout, g, wr, br)


def _dispatch_kernel(pstart_ref, pad_start_ref, pad_rows_ref, expert_ref, rank_ref, hp_hbm, xs_hbm, dest_ref,
                     buf, zero_s, lsem, sem, zsem):
    step = pl.program_id(0)
    last = pl.num_programs(0) - 1
    tm = expert_ref.shape[2] // TOP_K
    slot = lax.rem(step, 2)

    def stage(tile, to_slot):
        return pltpu.make_async_copy(hp_hbm.at[pl.ds(tile * tm, tm)], buf.at[to_slot], lsem.at[to_slot])

    def wait_tile():
        view = xs_hbm.at[pl.ds(0, TOP_K * tm)]
        pltpu.make_async_copy(view, view, sem).wait()

    @pl.when(step == 0)
    def _():
        stage(0, 0).start()
        zero_s[...] = jnp.zeros(zero_s.shape, zero_s.dtype)
        for e in range(pad_rows_ref.shape[0]):
            n = pad_rows_ref[e]

            @pl.when(n > 0)
            def _():
                pltpu.make_async_copy(zero_s.at[pl.ds(0, n)], xs_hbm.at[pl.ds(pad_start_ref[e], n)], zsem).start()

        for e in range(pad_rows_ref.shape[0]):
            n = pad_rows_ref[e]

            @pl.when(n > 0)
            def _():
                pltpu.make_async_copy(zero_s.at[pl.ds(0, n)], xs_hbm.at[pl.ds(pad_start_ref[e], n)], zsem).wait()

    stage(step, slot).wait()

    @pl.when(step > 0)
    def _():
        wait_tile()

    @pl.when(step < last)
    def _():
        stage(step + 1, 1 - slot).start()

    for j in range(tm):
        for k in range(TOP_K):
            a = TOP_K * j + k
            row = pstart_ref[expert_ref[0, 0, a]] + rank_ref[0, 0, a]
            dest_ref[0, 0, a] = row
            pltpu.make_async_copy(buf.at[slot, j], xs_hbm.at[row], sem).start(priority=k % 2)

    @pl.when(step == last)
    def _():
        wait_tile()


def _dispatch_call(t, p_len, pstart, pad_start, pad_rows, expert, rank, hp):
    tm = TM_OUT
    table = pl.BlockSpec((1, 1, TOP_K * tm), lambda i, *_: (i, 0, 0), memory_space=pltpu.SMEM)
    grid_spec = pltpu.PrefetchScalarGridSpec(
        num_scalar_prefetch=3,
        grid=(t // tm,),
        in_specs=[table, table, pl.BlockSpec(memory_space=pl.ANY)],
        out_specs=[pl.BlockSpec(memory_space=pl.ANY), table],
        scratch_shapes=[pltpu.VMEM((2, tm, TOK_TILE_ROWS, LANES), jnp.uint32),
                        pltpu.VMEM((BM, TOK_TILE_ROWS, LANES), jnp.uint32),
                        pltpu.SemaphoreType.DMA((2,)), pltpu.SemaphoreType.DMA(()), pltpu.SemaphoreType.DMA(())],
    )
    return pl.pallas_call(
        _dispatch_kernel,
        grid_spec=grid_spec,
        out_shape=[jax.ShapeDtypeStruct((p_len, TOK_TILE_ROWS, LANES), jnp.uint32),
                   jax.ShapeDtypeStruct(expert.shape, jnp.int32)],
        compiler_params=pltpu.CompilerParams(dimension_semantics=("arbitrary",), vmem_limit_bytes=VMEM_LIMIT),
        name="dispatch",
    )(pstart, pad_start, pad_rows, expert, rank, hp.reshape(t, TOK_TILE_ROWS, LANES))


def _expert_kernel(layer, be_ref, cnt_ref, nv_ref, first_ref, par_ref, nxt_ref, xs_ref, w1_hbm, w3_hbm, w2_hbm,
                   y_ref, w1_s, w3_s, w2_s, wsem):
    del nv_ref
    half = D_MODEL // 2
    step = pl.program_id(0)
    rows = cnt_ref[step]
    slot = par_ref[step]

    def weight_copies(e, to_slot):
        return (pltpu.make_async_copy(w1_hbm.at[layer, e], w1_s.at[to_slot], wsem.at[to_slot]),
                pltpu.make_async_copy(w3_hbm.at[layer, e], w3_s.at[to_slot], wsem.at[to_slot]),
                pltpu.make_async_copy(w2_hbm.at[layer, e], w2_s.at[to_slot], wsem.at[to_slot]))

    @pl.when(step == 0)
    def _():
        for cp in weight_copies(be_ref[0], slot):
            cp.start()

    @pl.when(first_ref[step] == 1)
    def _():
        for cp in weight_copies(be_ref[step], slot):
            cp.wait()

        @pl.when(nxt_ref[step] >= 0)
        def _():
            for cp in weight_copies(nxt_ref[step], 1 - slot):
                cp.start()

    @pl.when(rows == 0)
    def _():
        y_ref[...] = jnp.zeros(y_ref.shape, y_ref.dtype)

    @pl.when(rows > 0)
    def _():
        lo, hi = _unpack_bf16_pairs(_load_token_tiles(xs_ref, BM))
        a1 = (jnp.dot(lo, w1_s[slot, 0:half, :].astype(_BF16), preferred_element_type=_F32)
              + jnp.dot(hi, w1_s[slot, half:D_MODEL, :].astype(_BF16), preferred_element_type=_F32))
        a3 = (jnp.dot(lo, w3_s[slot, 0:half, :].astype(_BF16), preferred_element_type=_F32)
              + jnp.dot(hi, w3_s[slot, half:D_MODEL, :].astype(_BF16), preferred_element_type=_F32))
        hid = (a1 * _sigmoid(a1) * a3).astype(_BF16)
        y = jnp.dot(hid, w2_s[slot].astype(_BF16), preferred_element_type=_F32)
        _store_token_tiles(y_ref, _pack_bf16_pairs(y.astype(_BF16)))


def _expert_call(layer, n_blocks, plan, xs, w1, w3, w2):
    blk = lambda i, be, cnt, nv, *_: (jnp.minimum(i, nv[0] - 1), 0)
    grid_spec = pltpu.PrefetchScalarGridSpec(
        num_scalar_prefetch=len(plan),
        grid=(n_blocks,),
        in_specs=[
            pl.BlockSpec((BM * TOK_TILE_ROWS, LANES), blk),
            pl.BlockSpec(memory_space=pl.ANY),
            pl.BlockSpec(memory_space=pl.ANY),
            pl.BlockSpec(memory_space=pl.ANY),
        ],
        out_specs=pl.BlockSpec((BM * TOK_TILE_ROWS, LANES), lambda i, *_: (i, 0)),
        scratch_shapes=[
            pltpu.VMEM((2, D_MODEL, D_EXPERT), w1.dtype),
            pltpu.VMEM((2, D_MODEL, D_EXPERT), w3.dtype),
            pltpu.VMEM((2, D_EXPERT, D_MODEL), w2.dtype),
            pltpu.SemaphoreType.DMA((2,)),
        ],
    )
    ys = pl.pallas_call(
        functools.partial(_expert_kernel, layer),
        grid_spec=grid_spec,
        out_shape=jax.ShapeDtypeStruct((n_blocks * BM * TOK_TILE_ROWS, LANES), jnp.uint32),
        compiler_params=pltpu.CompilerParams(dimension_semantics=("arbitrary",), vmem_limit_bytes=VMEM_LIMIT),
        name="experts",
    )(*plan, xs.reshape(n_blocks * BM * TOK_TILE_ROWS, LANES), w1, w3, w2)
    return ys.reshape(n_blocks * BM, TOK_TILE_ROWS, LANES)


def _slot_tables(t, rec, counts_f):
    n_assign = t * TOP_K
    n_blocks = n_assign // BM + N_EXPERTS
    counts = counts_f[0, :N_EXPERTS].astype(jnp.int32)
    padded = ((counts + BM - 1) // BM) * BM
    pend = jnp.cumsum(padded)
    pstart = pend - padded
    starts = jnp.arange(n_blocks, dtype=jnp.int32) * BM
    block_e = jnp.minimum(jnp.sum(pend[None, :] <= starts[:, None], axis=1), N_EXPERTS - 1).astype(jnp.int32)
    seg_end = (pstart + counts)[block_e]
    block_cnt = jnp.where(starts < pend[-1], jnp.clip(seg_end - starts, 0, BM), 0).astype(jnp.int32)
    n_valid = jnp.maximum(pend[-1] // BM, 1).astype(jnp.int32).reshape(1)
    real = starts < pend[-1]
    block_e = jnp.where(real, block_e, block_e[n_valid[0] - 1])
    first = (real & (block_e != jnp.concatenate([jnp.full((1,), -1, jnp.int32), block_e[:-1]]))).astype(jnp.int32)
    parity = jnp.maximum(jnp.cumsum(first) - 1, 0) % 2
    e_ids = jnp.arange(N_EXPERTS, dtype=jnp.int32)
    later = jnp.where((counts[None, :] > 0) & (e_ids[None, :] > e_ids[:, None]), e_ids[None, :], N_EXPERTS)
    next_e = jnp.min(later, axis=1)
    next_e = jnp.where(next_e < N_EXPERTS, next_e, -1)[block_e]
    plan = (block_e, block_cnt, n_valid, first, parity.astype(jnp.int32), next_e.astype(jnp.int32))
    shape = (t // TM_OUT, 1, TOP_K * TM_OUT)
    expert = rec[:, REC_EXPERT:REC_EXPERT + TOP_K].astype(jnp.int32).reshape(shape)
    rank = rec[:, REC_RANK:REC_RANK + TOP_K].astype(jnp.int32).reshape(shape)
    tail = pend[-1] + jnp.arange(N_EXPERTS, dtype=jnp.int32) * BM
    pad_start = jnp.concatenate([pstart + counts, tail])
    pad_rows = jnp.concatenate([padded - counts, jnp.where(tail < n_blocks * BM, BM, 0)])
    return n_blocks, plan, pstart.astype(jnp.int32), pad_start, pad_rows.astype(jnp.int32), expert, rank


def _final_kernel(x_ref, rec_ref, y_hbm, dest_ref, dest_next_ref, g_ref, o_ref, ybuf, ysem):
    x = _combined_input(x_ref, rec_ref, y_hbm, dest_ref, dest_next_ref, ybuf, ysem)
    o_ref[...] = x * _rms_scale(x) * g_ref[...]


def _final_call(t, x, moe, g):
    tm = TM_OUT
    n_tiles = t // tm
    rec, ys, dest = moe
    in_specs, scratch = _combine_specs(tm, n_tiles)
    return pl.pallas_call(
        _final_kernel,
        grid=(n_tiles,),
        in_specs=in_specs + [_resident(g.shape)],
        out_specs=pl.BlockSpec((tm, D_MODEL), lambda i: (i, 0)),
        out_shape=jax.ShapeDtypeStruct((t, D_MODEL), _F32),
        scratch_shapes=scratch,
        compiler_params=pltpu.CompilerParams(dimension_semantics=("arbitrary",), vmem_limit_bytes=VMEM_LIMIT),
        name="final_norm",
    )(x, rec, ys, dest, dest, g)


def kernel(x, norm_mix, w_in, pool_w, pool_scale, sgu_ln_g, sgu_ln_b, sgu_w, sgu_b, conv_dw_w, conv_dw_b,
           conv_ln_g, conv_ln_b, conv_pw, w_out, norm_ffn, router_g_w, router_g_b, router_e_w, router_e_b,
           exp_w1, exp_w3, exp_w2, norm_final):
    b, s, d = x.shape
    depth = w_in.shape[0]
    assert d == D_MODEL and s % TM_MIX == 0 and TM_MIX % CHUNK == 0 and TM_MIX == TM_OUT and (b * s) % TM_PROJ == 0
    t = b * s
    xf = x.reshape(t, d)
    moe = None
    for l in range(depth):
        row2 = lambda v: v[l].reshape(1, -1)
        sbias = jnp.repeat(sgu_b[l].T, SGU_HEAD_DIM, axis=1)
        outs = _mixer_call(
            t, s, xf, moe, row2(norm_mix), w_in[l].astype(_BF16), pool_w[l].astype(_BF16),
            row2(pool_scale), row2(sgu_ln_g), row2(sgu_ln_b), sgu_w[l], sbias, conv_dw_w[l], row2(conv_dw_b),
            row2(conv_ln_g), row2(conv_ln_b), conv_pw[l].astype(_BF16))
        if moe is not None:
            ycat, xf = outs
        else:
            (ycat,) = outs
        pad = ROUTER_COLS - N_GROUPS - N_EXPERTS
        wr = jnp.concatenate([router_g_w[l], router_e_w[l], jnp.zeros((d, pad), _F32)], axis=1).astype(_BF16)
        br = jnp.concatenate([router_g_b[l], router_e_b[l], jnp.zeros((pad,), _F32)]).reshape(1, ROUTER_COLS)
        xf, hp, rec, counts = _out_call(t, ycat, xf, w_out[l].astype(_BF16), row2(norm_ffn), wr, br)
        n_blocks, plan, pstart, pad_start, pad_rows, expert, rank = _slot_tables(t, rec, counts)
        xs, dest = _dispatch_call(t, n_blocks * BM, pstart, pad_start, pad_rows, expert, rank, hp)
        ys = _expert_call(l, n_blocks, plan, xs, exp_w1, exp_w3, exp_w2)
        moe = (rec, ys, dest)
    out = _final_call(t, xf, moe, norm_final.reshape(1, -1))
    return out.reshape(b, s, d)
```

```python
import functools

import jax
import jax.numpy as jnp
from jax import lax
from jax.experimental import pallas as pl
from jax.experimental.pallas import tpu as pltpu

D_MODEL = 2048
D_A = D_MODEL // 4
D_B = 3 * D_MODEL // 8
D_C = D_MODEL - D_A - D_B
POOL_WINDOWS = (2, 4, 8, 16)
POOL_GROUP_DIM = D_A // len(POOL_WINDOWS)
CHUNK = 128
SGU_HEAD_DIM = 128
SGU_HEADS = D_B // SGU_HEAD_DIM
CONV_WIDTH = 31
N_IN = D_A + 2 * D_B + 2 * D_C
N_GROUPS = 4
E_PER_GROUP = 8
N_EXPERTS = N_GROUPS * E_PER_GROUP
TOP_K = 2
D_EXPERT = D_MODEL // 4
EPS = 1e-6

LANES = 128
SUBLANES = 8
POOL_HALO = 16
CONV_HALO = 32
CONV_ROWS = 64
CONV_PITCH = 2
TM_MIX = 256
TM_OUT = 256
TM_PROJ = 512
TM_DEST = 1024
BM = 256
TOK_TILE_ROWS = D_MODEL // 2 // LANES
ROUTER_COLS = LANES
REC_EXPERT = 0
REC_WEIGHT = TOP_K
REC_RANK = 2 * TOP_K
VMEM_LIMIT = 56 * 1024 * 1024

_F32 = jnp.float32
_BF16 = jnp.bfloat16


def _resident(shape):
    nd = len(shape)
    return pl.BlockSpec(shape, lambda *_: (0,) * nd, pipeline_mode=pl.Buffered(1))


def _rms_scale(x):
    return lax.rsqrt(jnp.mean(x * x, axis=-1, keepdims=True) + EPS)


def _layer_norm(x, g, b):
    mu = jnp.mean(x, axis=-1, keepdims=True)
    xc = x - mu
    return xc * lax.rsqrt(jnp.mean(xc * xc, axis=-1, keepdims=True) + EPS) * g + b


def _gelu_tanh(x):
    return 0.5 * x * (1.0 + jnp.tanh(0.7978845608028654 * (x + 0.044715 * (x * x * x))))


def _sigmoid(x):
    return 1.0 / (1.0 + jnp.exp(-x))


def _pack_bf16_pairs(hb):
    u = lax.bitcast_convert_type(hb.astype(_F32), jnp.uint32)
    c = hb.shape[1] // 2
    return (u[:, c:] & jnp.uint32(0xFFFF0000)) | (u[:, :c] >> 16)


def _unpack_pairs_f32(p):
    return (lax.bitcast_convert_type(p << 16, _F32), lax.bitcast_convert_type(p & jnp.uint32(0xFFFF0000), _F32))


def _unpack_bf16_pairs(p):
    lo, hi = _unpack_pairs_f32(p)
    return lo.astype(_BF16), hi.astype(_BF16)


def _store_token_tiles(ref, packed):
    rows = packed.shape[0]
    for s in range(TOK_TILE_ROWS):
        ref[pl.ds(s, rows, stride=TOK_TILE_ROWS), :] = packed[:, s * LANES:(s + 1) * LANES]


def _load_token_tiles(ref, rows):
    return jnp.concatenate([ref[pl.ds(s, rows, stride=TOK_TILE_ROWS), :] for s in range(TOK_TILE_ROWS)], axis=1)


def _start_expert_rows(y_hbm, dest_ref, ybuf, sem, slot):
    tm = ybuf.shape[2] // TOK_TILE_ROWS
    for j in range(tm):
        for k in range(TOP_K):
            pltpu.make_async_copy(y_hbm.at[dest_ref[0, k, j]],
                                  ybuf.at[slot, k, pl.ds(j * TOK_TILE_ROWS, TOK_TILE_ROWS)], sem.at[slot]).start()


def _wait_expert_rows(ybuf, sem, slot):
    pltpu.make_async_copy(ybuf.at[slot], ybuf.at[slot], sem.at[slot]).wait()


def _fetch_expert_rows(y_hbm, dest_ref, dest_next_ref, ybuf, sem):
    step = pl.program_id(0)
    slot = lax.rem(step, 2)

    @pl.when(step == 0)
    def _():
        _start_expert_rows(y_hbm, dest_ref, ybuf, sem, 0)

    for s in range(2):
        @pl.when((step < pl.num_programs(0) - 1) & (slot == s))
        def _(s=s):
            _start_expert_rows(y_hbm, dest_next_ref, ybuf, sem, 1 - s)

    _wait_expert_rows(ybuf, sem, slot)


def _combine(x_ref, rec_ref, ybuf, slot):
    rec = rec_ref[...]
    tm = x_ref.shape[0]
    moe = None
    for k in range(TOP_K):
        yk = jnp.concatenate(_unpack_pairs_f32(_load_token_tiles(ybuf.at[slot, k], tm)), axis=1)
        term = rec[:, REC_WEIGHT + k:REC_WEIGHT + k + 1] * yk
        moe = term if moe is None else moe + term
    return x_ref[...] + moe


def _combined_input(x_ref, rec_ref, y_hbm, dest_ref, dest_next_ref, ybuf, sem):
    _fetch_expert_rows(y_hbm, dest_ref, dest_next_ref, ybuf, sem)
    return _combine(x_ref, rec_ref, ybuf, lax.rem(pl.program_id(0), 2))


def _combine_specs(tm, n_tiles):
    row = lambda i: (jnp.minimum(i, n_tiles - 1), 0)
    in_specs = [
        pl.BlockSpec((tm, D_MODEL), row),
        pl.BlockSpec((tm, ROUTER_COLS), row),
        pl.BlockSpec(memory_space=pl.ANY),
        pl.BlockSpec((1, TOP_K, tm), lambda i: (jnp.minimum(i, n_tiles - 1), 0, 0), memory_space=pltpu.SMEM),
        pl.BlockSpec((1, TOP_K, tm), lambda i: (jnp.minimum(i + 1, n_tiles - 1), 0, 0), memory_space=pltpu.SMEM),
    ]
    scratch = [pltpu.VMEM((2, TOP_K, tm * TOK_TILE_ROWS, LANES), jnp.uint32), pltpu.SemaphoreType.DMA((2,))]
    return in_specs, scratch


def _mixer_kernel(combine, tiles_per_seq, *refs):
    if combine:
        x_ref, rec_ref, y_hbm, dest_ref, dest_next_ref = refs[:5]
        refs = refs[5:]
    else:
        x_ref = refs[0]
        refs = refs[1:]
    (g_ref, win_ref, poolw_ref, pscale_ref, sg_ref, sb_ref, sw_ref, sbias_ref,
     dww_ref, dwb_ref, cg_ref, cb_ref, pw_ref) = refs[:13]
    refs = refs[13:]
    if combine:
        ycat_ref, xo_ref = refs[:2]
        h_s, xa_s, z_s, cv_s, ybuf, ysem = refs[2:]
    else:
        ycat_ref = refs[0]
        h_s, xa_s, z_s, cv_s = refs[1:]
    tm = ycat_ref.shape[0]

    tile = pl.program_id(0) % tiles_per_seq
    first = tile == 0

    if combine:
        x = _combined_input(x_ref, rec_ref, y_hbm, dest_ref, dest_next_ref, ybuf, ysem)
        xo_ref[...] = x
    else:
        x = x_ref[...]
    n_ct = D_C // LANES
    halo_src = pl.ds(CONV_PITCH * tm, CONV_HALO, stride=CONV_PITCH)
    halo_dst = pl.ds(0, CONV_HALO, stride=CONV_PITCH)

    @pl.when(first)
    def _():
        xa_s[0:POOL_HALO, :] = jnp.zeros((POOL_HALO, D_A), _F32)
        for c in range(n_ct):
            z_s[c, halo_dst, :] = jnp.zeros((CONV_HALO, LANES), _F32)

    @pl.when(jnp.logical_not(first))
    def _():
        xa_s[0:POOL_HALO, :] = xa_s[tm:tm + POOL_HALO, :]
        for c in range(n_ct):
            z_s[c, halo_dst, :] = z_s[c, halo_src, :]

    h_s[...] = (x * _rms_scale(x) * g_ref[...]).astype(_BF16)

    pc = jnp.dot(h_s[...], win_ref[:, D_A + 2 * D_B:N_IN], preferred_element_type=_F32)
    zc = pc[:, :D_C] * _sigmoid(pc[:, D_C:])
    for c in range(n_ct):
        z_s[c, pl.ds(CONV_PITCH * CONV_HALO, tm, stride=CONV_PITCH), :] = zc[:, c * LANES:(c + 1) * LANES]
    for r0 in range(0, tm, CONV_ROWS):
        for c in range(n_ct):
            lanes = slice(c * LANES, (c + 1) * LANES)
            acc = jnp.zeros((CONV_ROWS, LANES), _F32)
            for k in range(CONV_WIDTH):
                off = r0 + CONV_HALO - (CONV_WIDTH - 1) + k
                acc = acc + dww_ref[k:k + 1, lanes] * z_s[c, pl.ds(CONV_PITCH * off, CONV_ROWS, stride=CONV_PITCH), :]
            cv_s[r0:r0 + CONV_ROWS, lanes] = acc + dwb_ref[:, lanes]

    xa_s[POOL_HALO:POOL_HALO + tm, :] = jnp.dot(h_s[...], win_ref[:, 0:D_A], preferred_element_type=_F32)
    pos = tile * tm + lax.broadcasted_iota(jnp.int32, (tm, 1), 0)
    for gi, w in enumerate(POOL_WINDOWS):
        cols = slice(gi * POOL_GROUP_DIM, (gi + 1) * POOL_GROUP_DIM)
        xg = xa_s[POOL_HALO:POOL_HALO + tm, cols]
        acc = xg
        for k in range(1, w):
            acc = acc + xa_s[POOL_HALO - k:POOL_HALO - k + tm, cols]
        cnt = jnp.minimum(pos + 1, w).astype(_F32)
        d = acc / cnt - xg
        yg = jnp.dot(d.astype(_BF16), poolw_ref[gi], preferred_element_type=_F32) * pscale_ref[:, cols]
        ycat_ref[:, cols] = yg.astype(_BF16)

    pb = jnp.dot(h_s[...], win_ref[:, D_A:D_A + 2 * D_B], preferred_element_type=_F32)
    zb = _gelu_tanh(pb)
    u = zb[:, :D_B]
    vb = _layer_norm(zb[:, D_B:], sg_ref[...], sb_ref[...]).astype(_BF16)
    r_i = lax.broadcasted_iota(jnp.int32, (CHUNK, CHUNK), 0)
    c_i = lax.broadcasted_iota(jnp.int32, (CHUNK, CHUNK), 1)
    causal = r_i >= c_i
    for hd in range(SGU_HEADS):
        hc = slice(hd * SGU_HEAD_DIM, (hd + 1) * SGU_HEAD_DIM)
        wm = jnp.where(causal, sw_ref[hd], 0.0).astype(_BF16)
        for c in range(tm // CHUNK):
            rows = slice(c * CHUNK, (c + 1) * CHUNK)
            gate = jnp.dot(wm, vb[rows, hc], preferred_element_type=_F32) + sbias_ref[:, hc]
            ycat_ref[rows, D_A + hd * SGU_HEAD_DIM:D_A + (hd + 1) * SGU_HEAD_DIM] = (u[rows, hc] * gate).astype(_BF16)

    zn = _layer_norm(cv_s[...], cg_ref[...], cb_ref[...])
    zs = (zn * _sigmoid(zn)).astype(_BF16)
    ycat_ref[:, D_A + D_B:D_MODEL] = jnp.dot(zs, pw_ref[...], preferred_element_type=_F32).astype(_BF16)


def _mixer_call(t, seq, x, moe, g, win, poolw, pscale, sg, sb, sw, sbias, dww, dwb, cg, cb, pw):
    tm = TM_MIX
    n_tiles = t // tm
    row = lambda i: (i, 0)
    combine = moe is not None
    scratch = [
        pltpu.VMEM((tm, D_MODEL), _BF16),
        pltpu.VMEM((POOL_HALO + tm, D_A), _F32),
        pltpu.VMEM((D_C // LANES, CONV_PITCH * (CONV_HALO + tm), LANES), _F32),
        pltpu.VMEM((tm, D_C), _F32),
    ]
    if combine:
        rec, ys, dest = moe
        in_specs, extra = _combine_specs(tm, n_tiles)
        scratch += extra
        args = [x, rec, ys, dest, dest]
    else:
        in_specs = [pl.BlockSpec((tm, D_MODEL), row)]
        args = [x]
    weights = [g, win, poolw, pscale, sg, sb, sw, sbias, dww, dwb, cg, cb, pw]
    in_specs += [_resident(w.shape) for w in weights]
    args += weights
    out_shape = [jax.ShapeDtypeStruct((t, D_MODEL), _BF16)]
    out_specs = [pl.BlockSpec((tm, D_MODEL), row)]
    if combine:
        out_shape.append(jax.ShapeDtypeStruct((t, D_MODEL), _F32))
        out_specs.append(pl.BlockSpec((tm, D_MODEL), row))
    return pl.pallas_call(
        functools.partial(_mixer_kernel, combine, seq // tm),
        grid=(n_tiles,),
        in_specs=in_specs,
        out_specs=out_specs,
        out_shape=out_shape,
        scratch_shapes=scratch,
        compiler_params=pltpu.CompilerParams(dimension_semantics=("arbitrary",), vmem_limit_bytes=VMEM_LIMIT),
        name="mixer_combine" if combine else "mixer",
    )(*args)


def _route_rows(logits, seen):
    rows = logits.shape[0]
    lane = lax.broadcasted_iota(jnp.int32, logits.shape, 1)
    neg = jnp.float32(-jnp.inf)
    big = jnp.int32(ROUTER_COLS)

    def first_max(v):
        m = jnp.max(v, axis=-1, keepdims=True)
        return m, jnp.min(jnp.where(v == m, lane, big), axis=-1, keepdims=True)

    lg = jnp.where(lane < N_GROUPS, logits, neg)
    mg, g_idx = first_max(lg)
    g_w = 1.0 / jnp.sum(jnp.exp(lg - mg), axis=-1, keepdims=True)
    e_lo = N_GROUPS + g_idx * E_PER_GROUP
    le = jnp.where((lane >= e_lo) & (lane < e_lo + E_PER_GROUP), logits, neg)
    m1, i1 = first_max(le)
    m2, i2 = first_max(jnp.where(lane == i1, neg, le))
    p2 = jnp.exp(m2 - m1)
    w1 = g_w / (1.0 + p2)
    w2 = g_w * p2 / (1.0 + p2)
    e1 = i1 - N_GROUPS
    e2 = i2 - N_GROUPS

    hit1 = lane == e1
    hit2 = lane == e2
    onehot = jnp.where(hit1, 1.0, 0.0) + jnp.where(hit2, 1.0, 0.0)
    r_i = lax.broadcasted_iota(jnp.int32, (rows, rows), 0)
    c_i = lax.broadcasted_iota(jnp.int32, (rows, rows), 1)
    earlier = jnp.where(r_i > c_i, 1.0, 0.0).astype(_BF16)
    before = seen + jnp.dot(earlier, onehot.astype(_BF16), preferred_element_type=_F32)
    rank1 = jnp.sum(jnp.where(hit1, before, 0.0), axis=-1, keepdims=True)
    rank2 = jnp.sum(jnp.where(hit2, before, 0.0), axis=-1, keepdims=True)
    seen = seen + jnp.sum(onehot, axis=0, keepdims=True)

    rec = jnp.zeros(logits.shape, _F32)
    for k, v in ((REC_EXPERT, e1.astype(_F32)), (REC_EXPERT + 1, e2.astype(_F32)), (REC_WEIGHT, w1),
                 (REC_WEIGHT + 1, w2), (REC_RANK, rank1), (REC_RANK + 1, rank2)):
        rec = jnp.where(lane == k, v, rec)
    return rec, seen


def _out_kernel(ycat_ref, x_ref, wout_ref, g_ref, wr_ref, br_ref, x1_ref, hp_ref, rec_ref, cnt_ref, lg_s, seen_s):
    step = pl.program_id(0)
    slot = lax.rem(step, 2)

    @pl.when(step == 0)
    def _():
        lg_s[...] = jnp.zeros(lg_s.shape, _F32)
        seen_s[...] = jnp.zeros(seen_s.shape, _F32)

    prev_logits = lg_s[1 - slot]
    prev_seen = seen_s[...]

    x1 = x_ref[...] + jnp.dot(ycat_ref[...], wout_ref[...], preferred_element_type=_F32)
    x1_ref[...] = x1
    hb = (x1 * _rms_scale(x1) * g_ref[...]).astype(_BF16)
    lg_s[slot] = jnp.dot(hb, wr_ref[...], preferred_element_type=_F32) + br_ref[...]

    rec, seen = _route_rows(prev_logits, prev_seen)
    rec_ref[...] = rec
    seen = jnp.where(step == 0, 0.0, seen)
    seen_s[...] = seen
    cnt_ref[...] = jnp.broadcast_to(seen, cnt_ref.shape)
    _store_token_tiles(hp_ref, _pack_bf16_pairs(hb))


def _out_call(t, ycat, x, wout, g, wr, br):
    tm = TM_PROJ
    n_tiles = t // tm
    row = lambda i: (jnp.minimum(i, n_tiles - 1), 0)
    prev = lambda i: (jnp.maximum(i - 1, 0), 0)
    return pl.pallas_call(
        _out_kernel,
        grid=(n_tiles + 1,),
        in_specs=[pl.BlockSpec((tm, D_MODEL), row), pl.BlockSpec((tm, D_MODEL), row),
                  _resident(wout.shape), _resident(g.shape), _resident(wr.shape), _resident(br.shape)],
        out_specs=[pl.BlockSpec((tm, D_MODEL), row), pl.BlockSpec((tm * TOK_TILE_ROWS, LANES), row),
                   pl.BlockSpec((tm, ROUTER_COLS), prev), pl.BlockSpec((SUBLANES, ROUTER_COLS), lambda i: (0, 0))],
        out_shape=[jax.ShapeDtypeStruct((t, D_MODEL), _F32),
                   jax.ShapeDtypeStruct((t * TOK_TILE_ROWS, LANES), jnp.uint32),
                   jax.ShapeDtypeStruct((t, ROUTER_COLS), _F32),
                   jax.ShapeDtypeStruct((SUBLANES, ROUTER_COLS), _F32)],
        scratch_shapes=[pltpu.VMEM((2, tm, ROUTER_COLS), _F32), pltpu.VMEM((1, ROUTER_COLS), _F32)],
        compiler_params=pltpu.CompilerParams(dimension_semantics=("arbitrary",), vmem_limit_bytes=VMEM_LIMIT),
        name="out_proj",
    )(ycat, x, wout, g, wr, br)


def _dest_kernel(rec_ref, seg_ref, out_ref):
    rec = rec_ref[...]
    lane = lax.broadcasted_iota(jnp.int32, rec.shape, 1)
    lane_f = lane.astype(_F32)
    rows = jnp.zeros(rec.shape, _F32)
    for k in range(TOP_K):
        expert = rec[:, REC_EXPERT + k:REC_EXPERT + k + 1]
        start = jnp.sum(jnp.where(lane_f == expert, seg_ref[...], 0.0), axis=-1, keepdims=True)
        rows = jnp.where(lane == k, start + rec[:, REC_RANK + k:REC_RANK + k + 1], rows)
    out_ref[...] = jnp.transpose(rows)[0:SUBLANES, :].astype(jnp.int32)


def _dest_call(t, rec, seg_start):
    td = TM_DEST
    by_k = pl.pallas_call(
        _dest_kernel,
        grid=(t // td,),
        in_specs=[pl.BlockSpec((td, ROUTER_COLS), lambda i: (i, 0)), _resident(seg_start.shape)],
        out_specs=pl.BlockSpec((SUBLANES, td), lambda i: (0, i)),
        out_shape=jax.ShapeDtypeStruct((SUBLANES, t), jnp.int32),
        compiler_params=pltpu.CompilerParams(dimension_semantics=("arbitrary",), vmem_limit_bytes=VMEM_LIMIT),
        name="dest_rows",
    )(rec, seg_start)
    return by_k[:TOP_K].reshape(TOP_K, t // TM_OUT, TM_OUT).transpose(1, 0, 2)


def _dispatch_kernel(pad_start_ref, pad_rows_ref, dest_ref, hp_hbm, xs_hbm, buf, zero_s, lsem, sem, zsem):
    step = pl.program_id(0)
    last = pl.num_programs(0) - 1
    tm = dest_ref.shape[2]
    slot = lax.rem(step, 2)

    def stage(tile, to_slot):
        return pltpu.make_async_copy(hp_hbm.at[pl.ds(tile * tm, tm)], buf.at[to_slot], lsem.at[to_slot])

    def wait_tile():
        view = xs_hbm.at[pl.ds(0, TOP_K * tm)]
        pltpu.make_async_copy(view, view, sem).wait()

    @pl.when(step == 0)
    def _():
        stage(0, 0).start()
        zero_s[...] = jnp.zeros(zero_s.shape, zero_s.dtype)
        for e in range(pad_rows_ref.shape[0]):
            n = pad_rows_ref[e]

            @pl.when(n > 0)
            def _():
                pltpu.make_async_copy(zero_s.at[pl.ds(0, n)], xs_hbm.at[pl.ds(pad_start_ref[e], n)], zsem).start()

        for e in range(pad_rows_ref.shape[0]):
            n = pad_rows_ref[e]

            @pl.when(n > 0)
            def _():
                pltpu.make_async_copy(zero_s.at[pl.ds(0, n)], xs_hbm.at[pl.ds(pad_start_ref[e], n)], zsem).wait()

    stage(step, slot).wait()

    @pl.when(step > 0)
    def _():
        wait_tile()

    @pl.when(step < last)
    def _():
        stage(step + 1, 1 - slot).start()

    for j in range(tm):
        for k in range(TOP_K):
            pltpu.make_async_copy(buf.at[slot, j], xs_hbm.at[dest_ref[0, k, j]], sem).start(priority=k % 2)

    @pl.when(step == last)
    def _():
        wait_tile()


def _dispatch_call(t, p_len, pad_start, pad_rows, dest, hp):
    tm = TM_OUT
    grid_spec = pltpu.PrefetchScalarGridSpec(
        num_scalar_prefetch=2,
        grid=(t // tm,),
        in_specs=[pl.BlockSpec((1, TOP_K, tm), lambda i, *_: (i, 0, 0), memory_space=pltpu.SMEM),
                  pl.BlockSpec(memory_space=pl.ANY)],
        out_specs=pl.BlockSpec(memory_space=pl.ANY),
        scratch_shapes=[pltpu.VMEM((2, tm, TOK_TILE_ROWS, LANES), jnp.uint32),
                        pltpu.VMEM((BM, TOK_TILE_ROWS, LANES), jnp.uint32),
                        pltpu.SemaphoreType.DMA((2,)), pltpu.SemaphoreType.DMA(()), pltpu.SemaphoreType.DMA(())],
    )
    return pl.pallas_call(
        _dispatch_kernel,
        grid_spec=grid_spec,
        out_shape=jax.ShapeDtypeStruct((p_len, TOK_TILE_ROWS, LANES), jnp.uint32),
        compiler_params=pltpu.CompilerParams(dimension_semantics=("arbitrary",), vmem_limit_bytes=VMEM_LIMIT),
        name="dispatch",
    )(pad_start, pad_rows, dest, hp.reshape(t, TOK_TILE_ROWS, LANES))


def _expert_kernel(layer, be_ref, cnt_ref, nv_ref, first_ref, par_ref, nxt_ref, xs_ref, w1_hbm, w3_hbm, w2_hbm,
                   y_ref, w1_s, w3_s, w2_s, wsem):
    del nv_ref
    half = D_MODEL // 2
    step = pl.program_id(0)
    rows = cnt_ref[step]
    slot = par_ref[step]

    def weight_copies(e, to_slot):
        return (pltpu.make_async_copy(w1_hbm.at[layer, e], w1_s.at[to_slot], wsem.at[to_slot]),
                pltpu.make_async_copy(w3_hbm.at[layer, e], w3_s.at[to_slot], wsem.at[to_slot]),
                pltpu.make_async_copy(w2_hbm.at[layer, e], w2_s.at[to_slot], wsem.at[to_slot]))

    @pl.when(step == 0)
    def _():
        for cp in weight_copies(be_ref[0], slot):
            cp.start()

    @pl.when(first_ref[step] == 1)
    def _():
        for cp in weight_copies(be_ref[step], slot):
            cp.wait()

        @pl.when(nxt_ref[step] >= 0)
        def _():
            for cp in weight_copies(nxt_ref[step], 1 - slot):
                cp.start()

    @pl.when(rows == 0)
    def _():
        y_ref[...] = jnp.zeros(y_ref.shape, y_ref.dtype)

    @pl.when(rows > 0)
    def _():
        lo, hi = _unpack_bf16_pairs(_load_token_tiles(xs_ref, BM))
        a1 = (jnp.dot(lo, w1_s[slot, 0:half, :].astype(_BF16), preferred_element_type=_F32)
              + jnp.dot(hi, w1_s[slot, half:D_MODEL, :].astype(_BF16), preferred_element_type=_F32))
        a3 = (jnp.dot(lo, w3_s[slot, 0:half, :].astype(_BF16), preferred_element_type=_F32)
              + jnp.dot(hi, w3_s[slot, half:D_MODEL, :].astype(_BF16), preferred_element_type=_F32))
        hid = (a1 * _sigmoid(a1) * a3).astype(_BF16)
        y = jnp.dot(hid, w2_s[slot].astype(_BF16), preferred_element_type=_F32)
        _store_token_tiles(y_ref, _pack_bf16_pairs(y.astype(_BF16)))


def _expert_call(layer, n_blocks, plan, xs, w1, w3, w2):
    blk = lambda i, be, cnt, nv, *_: (jnp.minimum(i, nv[0] - 1), 0)
    grid_spec = pltpu.PrefetchScalarGridSpec(
        num_scalar_prefetch=len(plan),
        grid=(n_blocks,),
        in_specs=[
            pl.BlockSpec((BM * TOK_TILE_ROWS, LANES), blk),
            pl.BlockSpec(memory_space=pl.ANY),
            pl.BlockSpec(memory_space=pl.ANY),
            pl.BlockSpec(memory_space=pl.ANY),
        ],
        out_specs=pl.BlockSpec((BM * TOK_TILE_ROWS, LANES), lambda i, *_: (i, 0)),
        scratch_shapes=[
            pltpu.VMEM((2, D_MODEL, D_EXPERT), w1.dtype),
            pltpu.VMEM((2, D_MODEL, D_EXPERT), w3.dtype),
            pltpu.VMEM((2, D_EXPERT, D_MODEL), w2.dtype),
            pltpu.SemaphoreType.DMA((2,)),
        ],
    )
    ys = pl.pallas_call(
        functools.partial(_expert_kernel, layer),
        grid_spec=grid_spec,
        out_shape=jax.ShapeDtypeStruct((n_blocks * BM * TOK_TILE_ROWS, LANES), jnp.uint32),
        compiler_params=pltpu.CompilerParams(dimension_semantics=("arbitrary",), vmem_limit_bytes=VMEM_LIMIT),
        name="experts",
    )(*plan, xs.reshape(n_blocks * BM * TOK_TILE_ROWS, LANES), w1, w3, w2)
    return ys.reshape(n_blocks * BM, TOK_TILE_ROWS, LANES)


def _slot_tables(t, counts_f):
    n_assign = t * TOP_K
    n_blocks = n_assign // BM + N_EXPERTS
    counts = counts_f[0, :N_EXPERTS].astype(jnp.int32)
    padded = ((counts + BM - 1) // BM) * BM
    pend = jnp.cumsum(padded)
    pstart = pend - padded
    starts = jnp.arange(n_blocks, dtype=jnp.int32) * BM
    block_e = jnp.minimum(jnp.sum(pend[None, :] <= starts[:, None], axis=1), N_EXPERTS - 1).astype(jnp.int32)
    seg_end = (pstart + counts)[block_e]
    block_cnt = jnp.where(starts < pend[-1], jnp.clip(seg_end - starts, 0, BM), 0).astype(jnp.int32)
    n_valid = jnp.maximum(pend[-1] // BM, 1).astype(jnp.int32).reshape(1)
    real = starts < pend[-1]
    block_e = jnp.where(real, block_e, block_e[n_valid[0] - 1])
    first = (real & (block_e != jnp.concatenate([jnp.full((1,), -1, jnp.int32), block_e[:-1]]))).astype(jnp.int32)
    parity = jnp.maximum(jnp.cumsum(first) - 1, 0) % 2
    e_ids = jnp.arange(N_EXPERTS, dtype=jnp.int32)
    later = jnp.where((counts[None, :] > 0) & (e_ids[None, :] > e_ids[:, None]), e_ids[None, :], N_EXPERTS)
    next_e = jnp.min(later, axis=1)
    next_e = jnp.where(next_e < N_EXPERTS, next_e, -1)[block_e]
    plan = (block_e, block_cnt, n_valid, first, parity.astype(jnp.int32), next_e.astype(jnp.int32))
    seg_start = jnp.zeros((1, ROUTER_COLS), _F32).at[0, :N_EXPERTS].set(pstart.astype(_F32))
    tail = pend[-1] + jnp.arange(N_EXPERTS, dtype=jnp.int32) * BM
    pad_start = jnp.concatenate([pstart + counts, tail])
    pad_rows = jnp.concatenate([padded - counts, jnp.where(tail < n_blocks * BM, BM, 0)])
    return n_blocks, plan, seg_start, pad_start, pad_rows.astype(jnp.int32)


def _final_kernel(x_ref, rec_ref, y_hbm, dest_ref, dest_next_ref, g_ref, o_ref, ybuf, ysem):
    x = _combined_input(x_ref, rec_ref, y_hbm, dest_ref, dest_next_ref, ybuf, ysem)
    o_ref[...] = x * _rms_scale(x) * g_ref[...]


def _final_call(t, x, moe, g):
    tm = TM_OUT
    n_tiles = t // tm
    rec, ys, dest = moe
    in_specs, scratch = _combine_specs(tm, n_tiles)
    return pl.pallas_call(
        _final_kernel,
        grid=(n_tiles,),
        in_specs=in_specs + [_resident(g.shape)],
        out_specs=pl.BlockSpec((tm, D_MODEL), lambda i: (i, 0)),
        out_shape=jax.ShapeDtypeStruct((t, D_MODEL), _F32),
        scratch_shapes=scratch,
        compiler_params=pltpu.CompilerParams(dimension_semantics=("arbitrary",), vmem_limit_bytes=VMEM_LIMIT),
        name="final_norm",
    )(x, rec, ys, dest, dest, g)


def kernel(x, norm_mix, w_in, pool_w, pool_scale, sgu_ln_g, sgu_ln_b, sgu_w, sgu_b, conv_dw_w, conv_dw_b,
           conv_ln_g, conv_ln_b, conv_pw, w_out, norm_ffn, router_g_w, router_g_b, router_e_w, router_e_b,
           exp_w1, exp_w3, exp_w2, norm_final):
    b, s, d = x.shape
    depth = w_in.shape[0]
    assert d == D_MODEL and s % TM_MIX == 0 and TM_MIX % CHUNK == 0 and TM_MIX == TM_OUT and (b * s) % TM_PROJ == 0
    t = b * s
    xf = x.reshape(t, d)
    moe = None
    for l in range(depth):
        row2 = lambda v: v[l].reshape(1, -1)
        sbias = jnp.repeat(sgu_b[l].T, SGU_HEAD_DIM, axis=1)
        outs = _mixer_call(
            t, s, xf, moe, row2(norm_mix), w_in[l].astype(_BF16), pool_w[l].astype(_BF16),
            row2(pool_scale), row2(sgu_ln_g), row2(sgu_ln_b), sgu_w[l], sbias, conv_dw_w[l], row2(conv_dw_b),
            row2(conv_ln_g), row2(conv_ln_b), conv_pw[l].astype(_BF16))
        if moe is not None:
            ycat, xf = outs
        else:
            (ycat,) = outs
        pad = ROUTER_COLS - N_GROUPS - N_EXPERTS
        wr = jnp.concatenate([router_g_w[l], router_e_w[l], jnp.zeros((d, pad), _F32)], axis=1).astype(_BF16)
        br = jnp.concatenate([router_g_b[l], router_e_b[l], jnp.zeros((pad,), _F32)]).reshape(1, ROUTER_COLS)
        xf, hp, rec, counts = _out_call(t, ycat, xf, w_out[l].astype(_BF16), row2(norm_ffn), wr, br)
        n_blocks, plan, seg_start, pad_start, pad_rows = _slot_tables(t, counts)
        dest = _dest_call(t, rec, seg_start)
        xs = _dispatch_call(t, n_blocks * BM, pad_start, pad_rows, dest, hp)
        ys = _expert_call(l, n_blocks, plan, xs, exp_w1, exp_w3, exp_w2)
        moe = (rec, ys, dest)
    out = _final_call(t, xf, moe, norm_final.reshape(1, -1))
    return out.reshape(b, s, d)
```

```python
import functools

import jax
import jax.numpy as jnp
from jax import lax
from jax.experimental import pallas as pl
from jax.experimental.pallas import tpu as pltpu

D_MODEL = 2048
D_A = D_MODEL // 4
D_B = 3 * D_MODEL // 8
D_C = D_MODEL - D_A - D_B
POOL_WINDOWS = (2, 4, 8, 16)
POOL_GROUP_DIM = D_A // len(POOL_WINDOWS)
CHUNK = 128
SGU_HEAD_DIM = 128
SGU_HEADS = D_B // SGU_HEAD_DIM
CONV_WIDTH = 31
N_IN = D_A + 2 * D_B + 2 * D_C
N_GROUPS = 4
E_PER_GROUP = 8
N_EXPERTS = N_GROUPS * E_PER_GROUP
TOP_K = 2
D_EXPERT = D_MODEL // 4
EPS = 1e-6

LANES = 128
SUBLANES = 8
POOL_HALO = 16
CONV_HALO = 32
CONV_ROWS = 64
CONV_PITCH = 2
TM_MIX = 256
TM_OUT = 256
TM_PROJ = 512
PROJ_PARTS = 2
TM_DEST = 1024
BM = 256
TOK_TILE_ROWS = D_MODEL // 2 // LANES
ROUTER_COLS = LANES
REC_EXPERT = 0
REC_WEIGHT = TOP_K
REC_RANK = 2 * TOP_K
VMEM_LIMIT = 56 * 1024 * 1024

_F32 = jnp.float32
_BF16 = jnp.bfloat16


def _resident(shape):
    nd = len(shape)
    return pl.BlockSpec(shape, lambda *_: (0,) * nd, pipeline_mode=pl.Buffered(1))


def _rms_scale(x):
    return lax.rsqrt(jnp.mean(x * x, axis=-1, keepdims=True) + EPS)


def _layer_norm(x, g, b):
    mu = jnp.mean(x, axis=-1, keepdims=True)
    xc = x - mu
    return xc * lax.rsqrt(jnp.mean(xc * xc, axis=-1, keepdims=True) + EPS) * g + b


def _gelu_tanh(x):
    return 0.5 * x * (1.0 + jnp.tanh(0.7978845608028654 * (x + 0.044715 * (x * x * x))))


def _sigmoid(x):
    return 1.0 / (1.0 + jnp.exp(-x))


def _pack_bf16_pairs(hb):
    u = lax.bitcast_convert_type(hb.astype(_F32), jnp.uint32)
    c = hb.shape[1] // 2
    return (u[:, c:] & jnp.uint32(0xFFFF0000)) | (u[:, :c] >> 16)


def _unpack_pairs_f32(p):
    return (lax.bitcast_convert_type(p << 16, _F32), lax.bitcast_convert_type(p & jnp.uint32(0xFFFF0000), _F32))


def _unpack_bf16_pairs(p):
    lo, hi = _unpack_pairs_f32(p)
    return lo.astype(_BF16), hi.astype(_BF16)


def _store_token_tiles(ref, packed):
    rows = packed.shape[0]
    for s in range(TOK_TILE_ROWS):
        ref[pl.ds(s, rows, stride=TOK_TILE_ROWS), :] = packed[:, s * LANES:(s + 1) * LANES]


def _load_token_tiles(ref, rows):
    return jnp.concatenate([ref[pl.ds(s, rows, stride=TOK_TILE_ROWS), :] for s in range(TOK_TILE_ROWS)], axis=1)


def _start_expert_rows(y_hbm, dest_ref, ybuf, sem, slot):
    tm = ybuf.shape[2] // TOK_TILE_ROWS
    for j in range(tm):
        for k in range(TOP_K):
            pltpu.make_async_copy(y_hbm.at[dest_ref[0, k, j]],
                                  ybuf.at[slot, k, pl.ds(j * TOK_TILE_ROWS, TOK_TILE_ROWS)], sem.at[slot]).start()


def _wait_expert_rows(ybuf, sem, slot):
    pltpu.make_async_copy(ybuf.at[slot], ybuf.at[slot], sem.at[slot]).wait()


def _fetch_expert_rows(y_hbm, dest_ref, dest_next_ref, ybuf, sem):
    step = pl.program_id(0)
    slot = lax.rem(step, 2)

    @pl.when(step == 0)
    def _():
        _start_expert_rows(y_hbm, dest_ref, ybuf, sem, 0)

    for s in range(2):
        @pl.when((step < pl.num_programs(0) - 1) & (slot == s))
        def _(s=s):
            _start_expert_rows(y_hbm, dest_next_ref, ybuf, sem, 1 - s)

    _wait_expert_rows(ybuf, sem, slot)


def _combine(x_ref, rec_ref, ybuf, slot):
    rec = rec_ref[...]
    tm = x_ref.shape[0]
    moe = None
    for k in range(TOP_K):
        yk = jnp.concatenate(_unpack_pairs_f32(_load_token_tiles(ybuf.at[slot, k], tm)), axis=1)
        term = rec[:, REC_WEIGHT + k:REC_WEIGHT + k + 1] * yk
        moe = term if moe is None else moe + term
    return x_ref[...] + moe


def _combined_input(x_ref, rec_ref, y_hbm, dest_ref, dest_next_ref, ybuf, sem):
    _fetch_expert_rows(y_hbm, dest_ref, dest_next_ref, ybuf, sem)
    return _combine(x_ref, rec_ref, ybuf, lax.rem(pl.program_id(0), 2))


def _combine_specs(tm, n_tiles):
    row = lambda i: (jnp.minimum(i, n_tiles - 1), 0)
    in_specs = [
        pl.BlockSpec((tm, D_MODEL), row),
        pl.BlockSpec((tm, ROUTER_COLS), row),
        pl.BlockSpec(memory_space=pl.ANY),
        pl.BlockSpec((1, TOP_K, tm), lambda i: (jnp.minimum(i, n_tiles - 1), 0, 0), memory_space=pltpu.SMEM),
        pl.BlockSpec((1, TOP_K, tm), lambda i: (jnp.minimum(i + 1, n_tiles - 1), 0, 0), memory_space=pltpu.SMEM),
    ]
    scratch = [pltpu.VMEM((2, TOP_K, tm * TOK_TILE_ROWS, LANES), jnp.uint32), pltpu.SemaphoreType.DMA((2,))]
    return in_specs, scratch


def _mixer_kernel(combine, tiles_per_seq, *refs):
    if combine:
        x_ref, rec_ref, y_hbm, dest_ref, dest_next_ref = refs[:5]
        refs = refs[5:]
    else:
        x_ref = refs[0]
        refs = refs[1:]
    (g_ref, win_ref, poolw_ref, pscale_ref, sg_ref, sb_ref, sw_ref, sbias_ref,
     dww_ref, dwb_ref, cg_ref, cb_ref, pw_ref) = refs[:13]
    refs = refs[13:]
    if combine:
        ycat_ref, xo_ref = refs[:2]
        h_s, xa_s, z_s, cv_s, ybuf, ysem = refs[2:]
    else:
        ycat_ref = refs[0]
        h_s, xa_s, z_s, cv_s = refs[1:]
    tm = ycat_ref.shape[0]

    tile = pl.program_id(0) % tiles_per_seq
    first = tile == 0

    if combine:
        x = _combined_input(x_ref, rec_ref, y_hbm, dest_ref, dest_next_ref, ybuf, ysem)
        xo_ref[...] = x
    else:
        x = x_ref[...]
    n_ct = D_C // LANES
    halo_src = pl.ds(CONV_PITCH * tm, CONV_HALO, stride=CONV_PITCH)
    halo_dst = pl.ds(0, CONV_HALO, stride=CONV_PITCH)

    @pl.when(first)
    def _():
        xa_s[0:POOL_HALO, :] = jnp.zeros((POOL_HALO, D_A), _F32)
        for c in range(n_ct):
            z_s[c, halo_dst, :] = jnp.zeros((CONV_HALO, LANES), _F32)

    @pl.when(jnp.logical_not(first))
    def _():
        xa_s[0:POOL_HALO, :] = xa_s[tm:tm + POOL_HALO, :]
        for c in range(n_ct):
            z_s[c, halo_dst, :] = z_s[c, halo_src, :]

    h_s[...] = (x * _rms_scale(x) * g_ref[...]).astype(_BF16)

    pc = jnp.dot(h_s[...], win_ref[:, D_A + 2 * D_B:N_IN], preferred_element_type=_F32)
    zc = pc[:, :D_C] * _sigmoid(pc[:, D_C:])
    for c in range(n_ct):
        z_s[c, pl.ds(CONV_PITCH * CONV_HALO, tm, stride=CONV_PITCH), :] = zc[:, c * LANES:(c + 1) * LANES]
    for r0 in range(0, tm, CONV_ROWS):
        for c in range(n_ct):
            lanes = slice(c * LANES, (c + 1) * LANES)
            acc = jnp.zeros((CONV_ROWS, LANES), _F32)
            for k in range(CONV_WIDTH):
                off = r0 + CONV_HALO - (CONV_WIDTH - 1) + k
                acc = acc + dww_ref[k:k + 1, lanes] * z_s[c, pl.ds(CONV_PITCH * off, CONV_ROWS, stride=CONV_PITCH), :]
            cv_s[r0:r0 + CONV_ROWS, lanes] = acc + dwb_ref[:, lanes]

    xa_s[POOL_HALO:POOL_HALO + tm, :] = jnp.dot(h_s[...], win_ref[:, 0:D_A], preferred_element_type=_F32)
    pos = tile * tm + lax.broadcasted_iota(jnp.int32, (tm, 1), 0)
    for gi, w in enumerate(POOL_WINDOWS):
        cols = slice(gi * POOL_GROUP_DIM, (gi + 1) * POOL_GROUP_DIM)
        xg = xa_s[POOL_HALO:POOL_HALO + tm, cols]
        acc = xg
        for k in range(1, w):
            acc = acc + xa_s[POOL_HALO - k:POOL_HALO - k + tm, cols]
        cnt = jnp.minimum(pos + 1, w).astype(_F32)
        d = acc / cnt - xg
        yg = jnp.dot(d.astype(_BF16), poolw_ref[gi], preferred_element_type=_F32) * pscale_ref[:, cols]
        ycat_ref[:, cols] = yg.astype(_BF16)

    pb = jnp.dot(h_s[...], win_ref[:, D_A:D_A + 2 * D_B], preferred_element_type=_F32)
    zb = _gelu_tanh(pb)
    u = zb[:, :D_B]
    vb = _layer_norm(zb[:, D_B:], sg_ref[...], sb_ref[...]).astype(_BF16)
    r_i = lax.broadcasted_iota(jnp.int32, (CHUNK, CHUNK), 0)
    c_i = lax.broadcasted_iota(jnp.int32, (CHUNK, CHUNK), 1)
    causal = r_i >= c_i
    for hd in range(SGU_HEADS):
        hc = slice(hd * SGU_HEAD_DIM, (hd + 1) * SGU_HEAD_DIM)
        wm = jnp.where(causal, sw_ref[hd], 0.0).astype(_BF16)
        for c in range(tm // CHUNK):
            rows = slice(c * CHUNK, (c + 1) * CHUNK)
            gate = jnp.dot(wm, vb[rows, hc], preferred_element_type=_F32) + sbias_ref[:, hc]
            ycat_ref[rows, D_A + hd * SGU_HEAD_DIM:D_A + (hd + 1) * SGU_HEAD_DIM] = (u[rows, hc] * gate).astype(_BF16)

    zn = _layer_norm(cv_s[...], cg_ref[...], cb_ref[...])
    zs = (zn * _sigmoid(zn)).astype(_BF16)
    ycat_ref[:, D_A + D_B:D_MODEL] = jnp.dot(zs, pw_ref[...], preferred_element_type=_F32).astype(_BF16)


def _mixer_call(t, seq, x, moe, g, win, poolw, pscale, sg, sb, sw, sbias, dww, dwb, cg, cb, pw):
    tm = TM_MIX
    n_tiles = t // tm
    row = lambda i: (i, 0)
    combine = moe is not None
    scratch = [
        pltpu.VMEM((tm, D_MODEL), _BF16),
        pltpu.VMEM((POOL_HALO + tm, D_A), _F32),
        pltpu.VMEM((D_C // LANES, CONV_PITCH * (CONV_HALO + tm), LANES), _F32),
        pltpu.VMEM((tm, D_C), _F32),
    ]
    if combine:
        rec, ys, dest = moe
        in_specs, extra = _combine_specs(tm, n_tiles)
        scratch += extra
        args = [x, rec, ys, dest, dest]
    else:
        in_specs = [pl.BlockSpec((tm, D_MODEL), row)]
        args = [x]
    weights = [g, win, poolw, pscale, sg, sb, sw, sbias, dww, dwb, cg, cb, pw]
    in_specs += [_resident(w.shape) for w in weights]
    args += weights
    out_shape = [jax.ShapeDtypeStruct((t, D_MODEL), _BF16)]
    out_specs = [pl.BlockSpec((tm, D_MODEL), row)]
    if combine:
        out_shape.append(jax.ShapeDtypeStruct((t, D_MODEL), _F32))
        out_specs.append(pl.BlockSpec((tm, D_MODEL), row))
    return pl.pallas_call(
        functools.partial(_mixer_kernel, combine, seq // tm),
        grid=(n_tiles,),
        in_specs=in_specs,
        out_specs=out_specs,
        out_shape=out_shape,
        scratch_shapes=scratch,
        compiler_params=pltpu.CompilerParams(dimension_semantics=("arbitrary",), vmem_limit_bytes=VMEM_LIMIT),
        name="mixer_combine" if combine else "mixer",
    )(*args)


def _route_rows(logits, seen):
    rows = logits.shape[0]
    lane = lax.broadcasted_iota(jnp.int32, logits.shape, 1)
    neg = jnp.float32(-jnp.inf)
    big = jnp.int32(ROUTER_COLS)

    def first_max(v):
        m = jnp.max(v, axis=-1, keepdims=True)
        return m, jnp.min(jnp.where(v == m, lane, big), axis=-1, keepdims=True)

    lg = jnp.where(lane < N_GROUPS, logits, neg)
    mg, g_idx = first_max(lg)
    g_w = 1.0 / jnp.sum(jnp.exp(lg - mg), axis=-1, keepdims=True)
    e_lo = N_GROUPS + g_idx * E_PER_GROUP
    le = jnp.where((lane >= e_lo) & (lane < e_lo + E_PER_GROUP), logits, neg)
    m1, i1 = first_max(le)
    m2, i2 = first_max(jnp.where(lane == i1, neg, le))
    p2 = jnp.exp(m2 - m1)
    w1 = g_w / (1.0 + p2)
    w2 = g_w * p2 / (1.0 + p2)
    e1 = i1 - N_GROUPS
    e2 = i2 - N_GROUPS

    hit1 = lane == e1
    hit2 = lane == e2
    onehot = jnp.where(hit1, 1.0, 0.0) + jnp.where(hit2, 1.0, 0.0)
    r_i = lax.broadcasted_iota(jnp.int32, (rows, rows), 0)
    c_i = lax.broadcasted_iota(jnp.int32, (rows, rows), 1)
    earlier = jnp.where(r_i > c_i, 1.0, 0.0).astype(_BF16)
    before = seen + jnp.dot(earlier, onehot.astype(_BF16), preferred_element_type=_F32)
    rank1 = jnp.sum(jnp.where(hit1, before, 0.0), axis=-1, keepdims=True)
    rank2 = jnp.sum(jnp.where(hit2, before, 0.0), axis=-1, keepdims=True)
    seen = seen + jnp.sum(onehot, axis=0, keepdims=True)

    rec = jnp.zeros(logits.shape, _F32)
    for k, v in ((REC_EXPERT, e1.astype(_F32)), (REC_EXPERT + 1, e2.astype(_F32)), (REC_WEIGHT, w1),
                 (REC_WEIGHT + 1, w2), (REC_RANK, rank1), (REC_RANK + 1, rank2)):
        rec = jnp.where(lane == k, v, rec)
    return rec, seen


def _out_kernel(ycat_ref, x_ref, wout_ref, g_ref, wr_ref, br_ref, x1_ref, hp_ref, rec_ref, cnt_ref, lg_s, seen_s):
    step = pl.program_id(0)
    slot = lax.rem(step, 2)

    @pl.when(step == 0)
    def _():
        lg_s[...] = jnp.zeros(lg_s.shape, _F32)
        seen_s[...] = jnp.zeros(seen_s.shape, _F32)

    prev_logits = lg_s[1 - slot]
    prev_seen = seen_s[...]

    tm = x_ref.shape[0]
    part = tm // PROJ_PARTS
    for q in range(PROJ_PARTS):
        rows = pl.ds(q * part, part)
        x1 = x_ref[rows, :] + jnp.dot(ycat_ref[rows, :], wout_ref[...], preferred_element_type=_F32)
        x1_ref[rows, :] = x1
        hb = (x1 * _rms_scale(x1) * g_ref[...]).astype(_BF16)
        lg_s[slot, rows, :] = jnp.dot(hb, wr_ref[...], preferred_element_type=_F32) + br_ref[...]
        _store_token_tiles(hp_ref.at[pl.ds(q * part * TOK_TILE_ROWS, part * TOK_TILE_ROWS)], _pack_bf16_pairs(hb))

    rec, seen = _route_rows(prev_logits, prev_seen)
    rec_ref[...] = rec
    seen = jnp.where(step == 0, 0.0, seen)
    seen_s[...] = seen
    cnt_ref[...] = jnp.broadcast_to(seen, cnt_ref.shape)


def _out_call(t, ycat, x, wout, g, wr, br):
    tm = TM_PROJ
    n_tiles = t // tm
    row = lambda i: (jnp.minimum(i, n_tiles - 1), 0)
    prev = lambda i: (jnp.maximum(i - 1, 0), 0)
    return pl.pallas_call(
        _out_kernel,
        grid=(n_tiles + 1,),
        in_specs=[pl.BlockSpec((tm, D_MODEL), row), pl.BlockSpec((tm, D_MODEL), row),
                  _resident(wout.shape), _resident(g.shape), _resident(wr.shape), _resident(br.shape)],
        out_specs=[pl.BlockSpec((tm, D_MODEL), row), pl.BlockSpec((tm * TOK_TILE_ROWS, LANES), row),
                   pl.BlockSpec((tm, ROUTER_COLS), prev), pl.BlockSpec((SUBLANES, ROUTER_COLS), lambda i: (0, 0))],
        out_shape=[jax.ShapeDtypeStruct((t, D_MODEL), _F32),
                   jax.ShapeDtypeStruct((t * TOK_TILE_ROWS, LANES), jnp.uint32),
                   jax.ShapeDtypeStruct((t, ROUTER_COLS), _F32),
                   jax.ShapeDtypeStruct((SUBLANES, ROUTER_COLS), _F32)],
        scratch_shapes=[pltpu.VMEM((2, tm, ROUTER_COLS), _F32), pltpu.VMEM((1, ROUTER_COLS), _F32)],
        compiler_params=pltpu.CompilerParams(dimension_semantics=("arbitrary",), vmem_limit_bytes=VMEM_LIMIT),
        name="out_proj",
    )(ycat, x, wout, g, wr, br)


def _dest_kernel(rec_ref, seg_ref, out_ref):
    rec = rec_ref[...]
    lane = lax.broadcasted_iota(jnp.int32, rec.shape, 1)
    lane_f = lane.astype(_F32)
    rows = jnp.zeros(rec.shape, _F32)
    for k in range(TOP_K):
        expert = rec[:, REC_EXPERT + k:REC_EXPERT + k + 1]
        start = jnp.sum(jnp.where(lane_f == expert, seg_ref[...], 0.0), axis=-1, keepdims=True)
        rows = jnp.where(lane == k, start + rec[:, REC_RANK + k:REC_RANK + k + 1], rows)
    out_ref[...] = jnp.transpose(rows)[0:SUBLANES, :].astype(jnp.int32)


def _dest_call(t, rec, seg_start):
    td = TM_DEST
    by_k = pl.pallas_call(
        _dest_kernel,
        grid=(t // td,),
        in_specs=[pl.BlockSpec((td, ROUTER_COLS), lambda i: (i, 0)), _resident(seg_start.shape)],
        out_specs=pl.BlockSpec((SUBLANES, td), lambda i: (0, i)),
        out_shape=jax.ShapeDtypeStruct((SUBLANES, t), jnp.int32),
        compiler_params=pltpu.CompilerParams(dimension_semantics=("arbitrary",), vmem_limit_bytes=VMEM_LIMIT),
        name="dest_rows",
    )(rec, seg_start)
    return by_k[:TOP_K].reshape(TOP_K, t // TM_OUT, TM_OUT).transpose(1, 0, 2)


def _dispatch_kernel(pad_start_ref, pad_rows_ref, dest_ref, hp_hbm, xs_hbm, buf, zero_s, lsem, sem, zsem):
    step = pl.program_id(0)
    last = pl.num_programs(0) - 1
    tm = dest_ref.shape[2]
    slot = lax.rem(step, 2)

    def stage(tile, to_slot):
        return pltpu.make_async_copy(hp_hbm.at[pl.ds(tile * tm, tm)], buf.at[to_slot], lsem.at[to_slot])

    def wait_tile():
        view = xs_hbm.at[pl.ds(0, TOP_K * tm)]
        pltpu.make_async_copy(view, view, sem).wait()

    @pl.when(step == 0)
    def _():
        stage(0, 0).start()
        zero_s[...] = jnp.zeros(zero_s.shape, zero_s.dtype)
        for e in range(pad_rows_ref.shape[0]):
            n = pad_rows_ref[e]

            @pl.when(n > 0)
            def _():
                pltpu.make_async_copy(zero_s.at[pl.ds(0, n)], xs_hbm.at[pl.ds(pad_start_ref[e], n)], zsem).start()

        for e in range(pad_rows_ref.shape[0]):
            n = pad_rows_ref[e]

            @pl.when(n > 0)
            def _():
                pltpu.make_async_copy(zero_s.at[pl.ds(0, n)], xs_hbm.at[pl.ds(pad_start_ref[e], n)], zsem).wait()

    stage(step, slot).wait()

    @pl.when(step > 0)
    def _():
        wait_tile()

    @pl.when(step < last)
    def _():
        stage(step + 1, 1 - slot).start()

    for j in range(tm):
        for k in range(TOP_K):
            pltpu.make_async_copy(buf.at[slot, j], xs_hbm.at[dest_ref[0, k, j]], sem).start(priority=k % 2)

    @pl.when(step == last)
    def _():
        wait_tile()


def _dispatch_call(t, p_len, pad_start, pad_rows, dest, hp):
    tm = TM_OUT
    grid_spec = pltpu.PrefetchScalarGridSpec(
        num_scalar_prefetch=2,
        grid=(t // tm,),
        in_specs=[pl.BlockSpec((1, TOP_K, tm), lambda i, *_: (i, 0, 0), memory_space=pltpu.SMEM),
                  pl.BlockSpec(memory_space=pl.ANY)],
        out_specs=pl.BlockSpec(memory_space=pl.ANY),
        scratch_shapes=[pltpu.VMEM((2, tm, TOK_TILE_ROWS, LANES), jnp.uint32),
                        pltpu.VMEM((BM, TOK_TILE_ROWS, LANES), jnp.uint32),
                        pltpu.SemaphoreType.DMA((2,)), pltpu.SemaphoreType.DMA(()), pltpu.SemaphoreType.DMA(())],
    )
    return pl.pallas_call(
        _dispatch_kernel,
        grid_spec=grid_spec,
        out_shape=jax.ShapeDtypeStruct((p_len, TOK_TILE_ROWS, LANES), jnp.uint32),
        compiler_params=pltpu.CompilerParams(dimension_semantics=("arbitrary",), vmem_limit_bytes=VMEM_LIMIT),
        name="dispatch",
    )(pad_start, pad_rows, dest, hp.reshape(t, TOK_TILE_ROWS, LANES))


def _expert_kernel(layer, be_ref, cnt_ref, nv_ref, first_ref, par_ref, nxt_ref, xs_ref, w1_hbm, w3_hbm, w2_hbm,
                   y_ref, w1_s, w3_s, w2_s, wsem):
    del nv_ref
    half = D_MODEL // 2
    step = pl.program_id(0)
    rows = cnt_ref[step]
    slot = par_ref[step]

    def weight_copies(e, to_slot):
        return (pltpu.make_async_copy(w1_hbm.at[layer, e], w1_s.at[to_slot], wsem.at[to_slot]),
                pltpu.make_async_copy(w3_hbm.at[layer, e], w3_s.at[to_slot], wsem.at[to_slot]),
                pltpu.make_async_copy(w2_hbm.at[layer, e], w2_s.at[to_slot], wsem.at[to_slot]))

    @pl.when(step == 0)
    def _():
        for cp in weight_copies(be_ref[0], slot):
            cp.start()

    @pl.when(first_ref[step] == 1)
    def _():
        for cp in weight_copies(be_ref[step], slot):
            cp.wait()

        @pl.when(nxt_ref[step] >= 0)
        def _():
            for cp in weight_copies(nxt_ref[step], 1 - slot):
                cp.start()

    @pl.when(rows == 0)
    def _():
        y_ref[...] = jnp.zeros(y_ref.shape, y_ref.dtype)

    @pl.when(rows > 0)
    def _():
        lo, hi = _unpack_bf16_pairs(_load_token_tiles(xs_ref, BM))
        a1 = (jnp.dot(lo, w1_s[slot, 0:half, :].astype(_BF16), preferred_element_type=_F32)
              + jnp.dot(hi, w1_s[slot, half:D_MODEL, :].astype(_BF16), preferred_element_type=_F32))
        a3 = (jnp.dot(lo, w3_s[slot, 0:half, :].astype(_BF16), preferred_element_type=_F32)
              + jnp.dot(hi, w3_s[slot, half:D_MODEL, :].astype(_BF16), preferred_element_type=_F32))
        hid = (a1 * _sigmoid(a1) * a3).astype(_BF16)
        y = jnp.dot(hid, w2_s[slot].astype(_BF16), preferred_element_type=_F32)
        _store_token_tiles(y_ref, _pack_bf16_pairs(y.astype(_BF16)))


def _expert_call(layer, n_blocks, plan, xs, w1, w3, w2):
    blk = lambda i, be, cnt, nv, *_: (jnp.minimum(i, nv[0] - 1), 0)
    grid_spec = pltpu.PrefetchScalarGridSpec(
        num_scalar_prefetch=len(plan),
        grid=(n_blocks,),
        in_specs=[
            pl.BlockSpec((BM * TOK_TILE_ROWS, LANES), blk),
            pl.BlockSpec(memory_space=pl.ANY),
            pl.BlockSpec(memory_space=pl.ANY),
            pl.BlockSpec(memory_space=pl.ANY),
        ],
        out_specs=pl.BlockSpec((BM * TOK_TILE_ROWS, LANES), lambda i, *_: (i, 0)),
        scratch_shapes=[
            pltpu.VMEM((2, D_MODEL, D_EXPERT), w1.dtype),
            pltpu.VMEM((2, D_MODEL, D_EXPERT), w3.dtype),
            pltpu.VMEM((2, D_EXPERT, D_MODEL), w2.dtype),
            pltpu.SemaphoreType.DMA((2,)),
        ],
    )
    ys = pl.pallas_call(
        functools.partial(_expert_kernel, layer),
        grid_spec=grid_spec,
        out_shape=jax.ShapeDtypeStruct((n_blocks * BM * TOK_TILE_ROWS, LANES), jnp.uint32),
        compiler_params=pltpu.CompilerParams(dimension_semantics=("arbitrary",), vmem_limit_bytes=VMEM_LIMIT),
        name="experts",
    )(*plan, xs.reshape(n_blocks * BM * TOK_TILE_ROWS, LANES), w1, w3, w2)
    return ys.reshape(n_blocks * BM, TOK_TILE_ROWS, LANES)


def _slot_tables(t, counts_f):
    n_assign = t * TOP_K
    n_blocks = n_assign // BM + N_EXPERTS
    counts = counts_f[0, :N_EXPERTS].astype(jnp.int32)
    padded = ((counts + BM - 1) // BM) * BM
    pend = jnp.cumsum(padded)
    pstart = pend - padded
    starts = jnp.arange(n_blocks, dtype=jnp.int32) * BM
    block_e = jnp.minimum(jnp.sum(pend[None, :] <= starts[:, None], axis=1), N_EXPERTS - 1).astype(jnp.int32)
    seg_end = (pstart + counts)[block_e]
    block_cnt = jnp.where(starts < pend[-1], jnp.clip(seg_end - starts, 0, BM), 0).astype(jnp.int32)
    n_valid = jnp.maximum(pend[-1] // BM, 1).astype(jnp.int32).reshape(1)
    real = starts < pend[-1]
    block_e = jnp.where(real, block_e, block_e[n_valid[0] - 1])
    first = (real & (block_e != jnp.concatenate([jnp.full((1,), -1, jnp.int32), block_e[:-1]]))).astype(jnp.int32)
    parity = jnp.maximum(jnp.cumsum(first) - 1, 0) % 2
    e_ids = jnp.arange(N_EXPERTS, dtype=jnp.int32)
    later = jnp.where((counts[None, :] > 0) & (e_ids[None, :] > e_ids[:, None]), e_ids[None, :], N_EXPERTS)
    next_e = jnp.min(later, axis=1)
    next_e = jnp.where(next_e < N_EXPERTS, next_e, -1)[block_e]
    plan = (block_e, block_cnt, n_valid, first, parity.astype(jnp.int32), next_e.astype(jnp.int32))
    seg_start = jnp.zeros((1, ROUTER_COLS), _F32).at[0, :N_EXPERTS].set(pstart.astype(_F32))
    tail = pend[-1] + jnp.arange(N_EXPERTS, dtype=jnp.int32) * BM
    pad_start = jnp.concatenate([pstart + counts, tail])
    pad_rows = jnp.concatenate([padded - counts, jnp.where(tail < n_blocks * BM, BM, 0)])
    return n_blocks, plan, seg_start, pad_start, pad_rows.astype(jnp.int32)


def _final_kernel(x_ref, rec_ref, y_hbm, dest_ref, dest_next_ref, g_ref, o_ref, ybuf, ysem):
    x = _combined_input(x_ref, rec_ref, y_hbm, dest_ref, dest_next_ref, ybuf, ysem)
    o_ref[...] = x * _rms_scale(x) * g_ref[...]


def _final_call(t, x, moe, g):
    tm = TM_OUT
    n_tiles = t // tm
    rec, ys, dest = moe
    in_specs, scratch = _combine_specs(tm, n_tiles)
    return pl.pallas_call(
        _final_kernel,
        grid=(n_tiles,),
        in_specs=in_specs + [_resident(g.shape)],
        out_specs=pl.BlockSpec((tm, D_MODEL), lambda i: (i, 0)),
        out_shape=jax.ShapeDtypeStruct((t, D_MODEL), _F32),
        scratch_shapes=scratch,
        compiler_params=pltpu.CompilerParams(dimension_semantics=("arbitrary",), vmem_limit_bytes=VMEM_LIMIT),
        name="final_norm",
    )(x, rec, ys, dest, dest, g)


def kernel(x, norm_mix, w_in, pool_w, pool_scale, sgu_ln_g, sgu_ln_b, sgu_w, sgu_b, conv_dw_w, conv_dw_b,
           conv_ln_g, conv_ln_b, conv_pw, w_out, norm_ffn, router_g_w, router_g_b, router_e_w, router_e_b,
           exp_w1, exp_w3, exp_w2, norm_final):
    b, s, d = x.shape
    depth = w_in.shape[0]
    assert d == D_MODEL and s % TM_MIX == 0 and TM_MIX % CHUNK == 0 and TM_MIX == TM_OUT and (b * s) % TM_PROJ == 0
    t = b * s
    xf = x.reshape(t, d)
    moe = None
    for l in range(depth):
        row2 = lambda v: v[l].reshape(1, -1)
        sbias = jnp.repeat(sgu_b[l].T, SGU_HEAD_DIM, axis=1)
        outs = _mixer_call(
            t, s, xf, moe, row2(norm_mix), w_in[l].astype(_BF16), pool_w[l].astype(_BF16),
            row2(pool_scale), row2(sgu_ln_g), row2(sgu_ln_b), sgu_w[l], sbias, conv_dw_w[l], row2(conv_dw_b),
            row2(conv_ln_g), row2(conv_ln_b), conv_pw[l].astype(_BF16))
        if moe is not None:
            ycat, xf = outs
        else:
            (ycat,) = outs
        pad = ROUTER_COLS - N_GROUPS - N_EXPERTS
        wr = jnp.concatenate([router_g_w[l], router_e_w[l], jnp.zeros((d, pad), _F32)], axis=1).astype(_BF16)
        br = jnp.concatenate([router_g_b[l], router_e_b[l], jnp.zeros((pad,), _F32)]).reshape(1, ROUTER_COLS)
        xf, hp, rec, counts = _out_call(t, ycat, xf, w_out[l].astype(_BF16), row2(norm_ffn), wr, br)
        n_blocks, plan, seg_start, pad_start, pad_rows = _slot_tables(t, counts)
        dest = _dest_call(t, rec, seg_start)
        xs = _dispatch_call(t, n_blocks * BM, pad_start, pad_rows, dest, hp)
        ys = _expert_call(l, n_blocks, plan, xs, exp_w1, exp_w3, exp_w2)
        moe = (rec, ys, dest)
    out = _final_call(t, xf, moe, norm_final.reshape(1, -1))
    return out.reshape(b, s, d)
```

```python
import functools

import jax
import jax.numpy as jnp
from jax import lax
from jax.experimental import pallas as pl
from jax.experimental.pallas import tpu as pltpu

D_MODEL = 2048
D_A = D_MODEL // 4
D_B = 3 * D_MODEL // 8
D_C = D_MODEL - D_A - D_B
POOL_WINDOWS = (2, 4, 8, 16)
POOL_GROUP_DIM = D_A // len(POOL_WINDOWS)
CHUNK = 128
SGU_HEAD_DIM = 128
SGU_HEADS = D_B // SGU_HEAD_DIM
CONV_WIDTH = 31
N_IN = D_A + 2 * D_B + 2 * D_C
N_GROUPS = 4
E_PER_GROUP = 8
N_EXPERTS = N_GROUPS * E_PER_GROUP
TOP_K = 2
D_EXPERT = D_MODEL // 4
EPS = 1e-6

LANES = 128
SUBLANES = 8
POOL_HALO = 16
CONV_HALO = 32
CONV_ROWS = 64
CONV_PITCH = 2
TM_MIX = 256
TM_OUT = 256
TM_PROJ = 512
PROJ_PARTS = 2
TM_DEST = 1024
BM = 256
TOK_TILE_ROWS = D_MODEL // 2 // LANES
ROUTER_COLS = LANES
REC_EXPERT = 0
REC_WEIGHT = TOP_K
REC_RANK = 2 * TOP_K
VMEM_LIMIT = 56 * 1024 * 1024

_F32 = jnp.float32
_BF16 = jnp.bfloat16


def _resident(shape):
    nd = len(shape)
    return pl.BlockSpec(shape, lambda *_: (0,) * nd, pipeline_mode=pl.Buffered(1))


def _rms_scale(x):
    return lax.rsqrt(jnp.mean(x * x, axis=-1, keepdims=True) + EPS)


def _layer_norm(x, g, b):
    mu = jnp.mean(x, axis=-1, keepdims=True)
    xc = x - mu
    return xc * lax.rsqrt(jnp.mean(xc * xc, axis=-1, keepdims=True) + EPS) * g + b


def _gelu_tanh(x):
    c = 0.7978845608028654
    half_x = 0.5 * x
    return half_x + half_x * jnp.tanh(x * (c + (c * 0.044715) * (x * x)))


def _sigmoid(x):
    return 1.0 / (1.0 + jnp.exp(-x))


def _pack_bf16_pairs(hb):
    u = lax.bitcast_convert_type(hb.astype(_F32), jnp.uint32)
    c = hb.shape[1] // 2
    return (u[:, c:] & jnp.uint32(0xFFFF0000)) | (u[:, :c] >> 16)


def _unpack_pairs_f32(p):
    return (lax.bitcast_convert_type(p << 16, _F32), lax.bitcast_convert_type(p & jnp.uint32(0xFFFF0000), _F32))


def _unpack_bf16_pairs(p):
    lo, hi = _unpack_pairs_f32(p)
    return lo.astype(_BF16), hi.astype(_BF16)


def _store_token_tiles(ref, packed):
    rows = packed.shape[0]
    for s in range(TOK_TILE_ROWS):
        ref[pl.ds(s, rows, stride=TOK_TILE_ROWS), :] = packed[:, s * LANES:(s + 1) * LANES]


def _load_token_tiles(ref, rows):
    return jnp.concatenate([ref[pl.ds(s, rows, stride=TOK_TILE_ROWS), :] for s in range(TOK_TILE_ROWS)], axis=1)


def _start_expert_rows(y_hbm, dest_ref, ybuf, sem, slot):
    tm = ybuf.shape[2] // TOK_TILE_ROWS
    for j in range(tm):
        for k in range(TOP_K):
            pltpu.make_async_copy(y_hbm.at[dest_ref[0, k, j]],
                                  ybuf.at[slot, k, pl.ds(j * TOK_TILE_ROWS, TOK_TILE_ROWS)], sem.at[slot]).start()


def _wait_expert_rows(ybuf, sem, slot):
    pltpu.make_async_copy(ybuf.at[slot], ybuf.at[slot], sem.at[slot]).wait()


def _fetch_expert_rows(y_hbm, dest_ref, dest_next_ref, ybuf, sem):
    step = pl.program_id(0)
    slot = lax.rem(step, 2)

    @pl.when(step == 0)
    def _():
        _start_expert_rows(y_hbm, dest_ref, ybuf, sem, 0)

    for s in range(2):
        @pl.when((step < pl.num_programs(0) - 1) & (slot == s))
        def _(s=s):
            _start_expert_rows(y_hbm, dest_next_ref, ybuf, sem, 1 - s)

    _wait_expert_rows(ybuf, sem, slot)


def _combine(x_ref, rec_ref, ybuf, slot):
    rec = rec_ref[...]
    tm = x_ref.shape[0]
    moe = None
    for k in range(TOP_K):
        yk = jnp.concatenate(_unpack_pairs_f32(_load_token_tiles(ybuf.at[slot, k], tm)), axis=1)
        term = rec[:, REC_WEIGHT + k:REC_WEIGHT + k + 1] * yk
        moe = term if moe is None else moe + term
    return x_ref[...] + moe


def _combined_input(x_ref, rec_ref, y_hbm, dest_ref, dest_next_ref, ybuf, sem):
    _fetch_expert_rows(y_hbm, dest_ref, dest_next_ref, ybuf, sem)
    return _combine(x_ref, rec_ref, ybuf, lax.rem(pl.program_id(0), 2))


def _combine_specs(tm, n_tiles):
    row = lambda i: (jnp.minimum(i, n_tiles - 1), 0)
    in_specs = [
        pl.BlockSpec((tm, D_MODEL), row),
        pl.BlockSpec((tm, ROUTER_COLS), row),
        pl.BlockSpec(memory_space=pl.ANY),
        pl.BlockSpec((1, TOP_K, tm), lambda i: (jnp.minimum(i, n_tiles - 1), 0, 0), memory_space=pltpu.SMEM),
        pl.BlockSpec((1, TOP_K, tm), lambda i: (jnp.minimum(i + 1, n_tiles - 1), 0, 0), memory_space=pltpu.SMEM),
    ]
    scratch = [pltpu.VMEM((2, TOP_K, tm * TOK_TILE_ROWS, LANES), jnp.uint32), pltpu.SemaphoreType.DMA((2,))]
    return in_specs, scratch


def _mixer_kernel(combine, tiles_per_seq, *refs):
    if combine:
        x_ref, rec_ref, y_hbm, dest_ref, dest_next_ref = refs[:5]
        refs = refs[5:]
    else:
        x_ref = refs[0]
        refs = refs[1:]
    (g_ref, win_ref, poolw_ref, pscale_ref, sg_ref, sb_ref, sw_ref, sbias_ref,
     dww_ref, dwb_ref, cg_ref, cb_ref, pw_ref) = refs[:13]
    refs = refs[13:]
    if combine:
        ycat_ref, xo_ref = refs[:2]
        h_s, xa_s, z_s, cv_s, ybuf, ysem = refs[2:]
    else:
        ycat_ref = refs[0]
        h_s, xa_s, z_s, cv_s = refs[1:]
    tm = ycat_ref.shape[0]

    tile = pl.program_id(0) % tiles_per_seq
    first = tile == 0

    if combine:
        x = _combined_input(x_ref, rec_ref, y_hbm, dest_ref, dest_next_ref, ybuf, ysem)
        xo_ref[...] = x
    else:
        x = x_ref[...]
    n_ct = D_C // LANES
    halo_src = pl.ds(CONV_PITCH * tm, CONV_HALO, stride=CONV_PITCH)
    halo_dst = pl.ds(0, CONV_HALO, stride=CONV_PITCH)

    n_pg = len(POOL_WINDOWS)
    pool_src = pl.ds(CONV_PITCH * tm, POOL_HALO, stride=CONV_PITCH)
    pool_dst = pl.ds(0, POOL_HALO, stride=CONV_PITCH)

    @pl.when(first)
    def _():
        for gi in range(n_pg):
            xa_s[gi, pool_dst, :] = jnp.zeros((POOL_HALO, LANES), _F32)
        for c in range(n_ct):
            z_s[c, halo_dst, :] = jnp.zeros((CONV_HALO, LANES), _F32)

    @pl.when(jnp.logical_not(first))
    def _():
        for gi in range(n_pg):
            xa_s[gi, pool_dst, :] = xa_s[gi, pool_src, :]
        for c in range(n_ct):
            z_s[c, halo_dst, :] = z_s[c, halo_src, :]

    h_s[...] = (x * _rms_scale(x) * g_ref[...]).astype(_BF16)

    pc = jnp.dot(h_s[...], win_ref[:, D_A + 2 * D_B:N_IN], preferred_element_type=_F32)
    zc = pc[:, :D_C] * _sigmoid(pc[:, D_C:])
    for c in range(n_ct):
        z_s[c, pl.ds(CONV_PITCH * CONV_HALO, tm, stride=CONV_PITCH), :] = zc[:, c * LANES:(c + 1) * LANES]
    for r0 in range(0, tm, CONV_ROWS):
        for c in range(n_ct):
            lanes = slice(c * LANES, (c + 1) * LANES)
            acc = jnp.zeros((CONV_ROWS, LANES), _F32)
            for k in range(CONV_WIDTH):
                off = r0 + CONV_HALO - (CONV_WIDTH - 1) + k
                acc = acc + dww_ref[k:k + 1, lanes] * z_s[c, pl.ds(CONV_PITCH * off, CONV_ROWS, stride=CONV_PITCH), :]
            cv_s[r0:r0 + CONV_ROWS, lanes] = acc + dwb_ref[:, lanes]

    pa = jnp.dot(h_s[...], win_ref[:, 0:D_A], preferred_element_type=_F32)
    for gi in range(n_pg):
        xa_s[gi, pl.ds(CONV_PITCH * POOL_HALO, tm, stride=CONV_PITCH), :] = pa[:, gi * LANES:(gi + 1) * LANES]
    pos = tile * tm + lax.broadcasted_iota(jnp.int32, (tm, 1), 0)
    for gi, w in enumerate(POOL_WINDOWS):
        cols = slice(gi * POOL_GROUP_DIM, (gi + 1) * POOL_GROUP_DIM)
        xg = pa[:, cols]
        acc = xg
        for k in range(1, w):
            acc = acc + xa_s[gi, pl.ds(CONV_PITCH * (POOL_HALO - k), tm, stride=CONV_PITCH), :]
        cnt = jnp.minimum(pos + 1, w).astype(_F32)
        d = acc / cnt - xg
        yg = jnp.dot(d.astype(_BF16), poolw_ref[gi], preferred_element_type=_F32) * pscale_ref[:, cols]
        ycat_ref[:, cols] = yg.astype(_BF16)

    pb = jnp.dot(h_s[...], win_ref[:, D_A:D_A + 2 * D_B], preferred_element_type=_F32)
    zb = _gelu_tanh(pb)
    u = zb[:, :D_B]
    vb = _layer_norm(zb[:, D_B:], sg_ref[...], sb_ref[...]).astype(_BF16)
    r_i = lax.broadcasted_iota(jnp.int32, (CHUNK, CHUNK), 0)
    c_i = lax.broadcasted_iota(jnp.int32, (CHUNK, CHUNK), 1)
    causal = r_i >= c_i
    for hd in range(SGU_HEADS):
        hc = slice(hd * SGU_HEAD_DIM, (hd + 1) * SGU_HEAD_DIM)
        wm = jnp.where(causal, sw_ref[hd], 0.0).astype(_BF16)
        for c in range(tm // CHUNK):
            rows = slice(c * CHUNK, (c + 1) * CHUNK)
            gate = jnp.dot(wm, vb[rows, hc], preferred_element_type=_F32) + sbias_ref[:, hc]
            ycat_ref[rows, D_A + hd * SGU_HEAD_DIM:D_A + (hd + 1) * SGU_HEAD_DIM] = (u[rows, hc] * gate).astype(_BF16)

    zn = _layer_norm(cv_s[...], cg_ref[...], cb_ref[...])
    zs = (zn * _sigmoid(zn)).astype(_BF16)
    ycat_ref[:, D_A + D_B:D_MODEL] = jnp.dot(zs, pw_ref[...], preferred_element_type=_F32).astype(_BF16)


def _mixer_call(t, seq, x, moe, g, win, poolw, pscale, sg, sb, sw, sbias, dww, dwb, cg, cb, pw):
    tm = TM_MIX
    n_tiles = t // tm
    row = lambda i: (i, 0)
    combine = moe is not None
    scratch = [
        pltpu.VMEM((tm, D_MODEL), _BF16),
        pltpu.VMEM((D_A // LANES, CONV_PITCH * (POOL_HALO + tm), LANES), _F32),
        pltpu.VMEM((D_C // LANES, CONV_PITCH * (CONV_HALO + tm), LANES), _F32),
        pltpu.VMEM((tm, D_C), _F32),
    ]
    if combine:
        rec, ys, dest = moe
        in_specs, extra = _combine_specs(tm, n_tiles)
        scratch += extra
        args = [x, rec, ys, dest, dest]
    else:
        in_specs = [pl.BlockSpec((tm, D_MODEL), row)]
        args = [x]
    weights = [g, win, poolw, pscale, sg, sb, sw, sbias, dww, dwb, cg, cb, pw]
    in_specs += [_resident(w.shape) for w in weights]
    args += weights
    out_shape = [jax.ShapeDtypeStruct((t, D_MODEL), _BF16)]
    out_specs = [pl.BlockSpec((tm, D_MODEL), row)]
    if combine:
        out_shape.append(jax.ShapeDtypeStruct((t, D_MODEL), _F32))
        out_specs.append(pl.BlockSpec((tm, D_MODEL), row))
    return pl.pallas_call(
        functools.partial(_mixer_kernel, combine, seq // tm),
        grid=(n_tiles,),
        in_specs=in_specs,
        out_specs=out_specs,
        out_shape=out_shape,
        scratch_shapes=scratch,
        compiler_params=pltpu.CompilerParams(dimension_semantics=("arbitrary",), vmem_limit_bytes=VMEM_LIMIT),
        name="mixer_combine" if combine else "mixer",
    )(*args)


def _route_rows(logits, seen):
    rows = logits.shape[0]
    lane = lax.broadcasted_iota(jnp.int32, logits.shape, 1)
    neg = jnp.float32(-jnp.inf)
    big = jnp.int32(ROUTER_COLS)

    def first_max(v):
        m = jnp.max(v, axis=-1, keepdims=True)
        return m, jnp.min(jnp.where(v == m, lane, big), axis=-1, keepdims=True)

    lg = jnp.where(lane < N_GROUPS, logits, neg)
    mg, g_idx = first_max(lg)
    g_w = 1.0 / jnp.sum(jnp.exp(lg - mg), axis=-1, keepdims=True)
    e_lo = N_GROUPS + g_idx * E_PER_GROUP
    le = jnp.where((lane >= e_lo) & (lane < e_lo + E_PER_GROUP), logits, neg)
    m1, i1 = first_max(le)
    m2, i2 = first_max(jnp.where(lane == i1, neg, le))
    p2 = jnp.exp(m2 - m1)
    w1 = g_w / (1.0 + p2)
    w2 = g_w * p2 / (1.0 + p2)
    e1 = i1 - N_GROUPS
    e2 = i2 - N_GROUPS

    hit1 = lane == e1
    hit2 = lane == e2
    onehot = jnp.where(hit1, 1.0, 0.0) + jnp.where(hit2, 1.0, 0.0)
    r_i = lax.broadcasted_iota(jnp.int32, (rows, rows), 0)
    c_i = lax.broadcasted_iota(jnp.int32, (rows, rows), 1)
    earlier = jnp.where(r_i > c_i, 1.0, 0.0).astype(_BF16)
    before = seen + jnp.dot(earlier, onehot.astype(_BF16), preferred_element_type=_F32)
    rank1 = jnp.sum(jnp.where(hit1, before, 0.0), axis=-1, keepdims=True)
    rank2 = jnp.sum(jnp.where(hit2, before, 0.0), axis=-1, keepdims=True)
    seen = seen + jnp.sum(onehot, axis=0, keepdims=True)

    rec = jnp.zeros(logits.shape, _F32)
    for k, v in ((REC_EXPERT, e1.astype(_F32)), (REC_EXPERT + 1, e2.astype(_F32)), (REC_WEIGHT, w1),
                 (REC_WEIGHT + 1, w2), (REC_RANK, rank1), (REC_RANK + 1, rank2)):
        rec = jnp.where(lane == k, v, rec)
    return rec, seen


def _out_kernel(ycat_ref, x_ref, wout_ref, g_ref, wr_ref, br_ref, x1_ref, hp_ref, rec_ref, cnt_ref, lg_s, seen_s):
    step = pl.program_id(0)
    slot = lax.rem(step, 2)

    @pl.when(step == 0)
    def _():
        lg_s[...] = jnp.zeros(lg_s.shape, _F32)
        seen_s[...] = jnp.zeros(seen_s.shape, _F32)

    prev_logits = lg_s[1 - slot]
    prev_seen = seen_s[...]

    tm = x_ref.shape[0]
    part = tm // PROJ_PARTS
    for q in range(PROJ_PARTS):
        rows = pl.ds(q * part, part)
        x1 = x_ref[rows, :] + jnp.dot(ycat_ref[rows, :], wout_ref[...], preferred_element_type=_F32)
        x1_ref[rows, :] = x1
        hb = (x1 * _rms_scale(x1) * g_ref[...]).astype(_BF16)
        lg_s[slot, rows, :] = jnp.dot(hb, wr_ref[...], preferred_element_type=_F32) + br_ref[...]
        _store_token_tiles(hp_ref.at[pl.ds(q * part * TOK_TILE_ROWS, part * TOK_TILE_ROWS)], _pack_bf16_pairs(hb))

    rec, seen = _route_rows(prev_logits, prev_seen)
    rec_ref[...] = rec
    seen = jnp.where(step == 0, 0.0, seen)
    seen_s[...] = seen
    cnt_ref[...] = jnp.broadcast_to(seen, cnt_ref.shape)


def _out_call(t, ycat, x, wout, g, wr, br):
    tm = TM_PROJ
    n_tiles = t // tm
    row = lambda i: (jnp.minimum(i, n_tiles - 1), 0)
    prev = lambda i: (jnp.maximum(i - 1, 0), 0)
    return pl.pallas_call(
        _out_kernel,
        grid=(n_tiles + 1,),
        in_specs=[pl.BlockSpec((tm, D_MODEL), row), pl.BlockSpec((tm, D_MODEL), row),
                  _resident(wout.shape), _resident(g.shape), _resident(wr.shape), _resident(br.shape)],
        out_specs=[pl.BlockSpec((tm, D_MODEL), row), pl.BlockSpec((tm * TOK_TILE_ROWS, LANES), row),
                   pl.BlockSpec((tm, ROUTER_COLS), prev), pl.BlockSpec((SUBLANES, ROUTER_COLS), lambda i: (0, 0))],
        out_shape=[jax.ShapeDtypeStruct((t, D_MODEL), _F32),
                   jax.ShapeDtypeStruct((t * TOK_TILE_ROWS, LANES), jnp.uint32),
                   jax.ShapeDtypeStruct((t, ROUTER_COLS), _F32),
                   jax.ShapeDtypeStruct((SUBLANES, ROUTER_COLS), _F32)],
        scratch_shapes=[pltpu.VMEM((2, tm, ROUTER_COLS), _F32), pltpu.VMEM((1, ROUTER_COLS), _F32)],
        compiler_params=pltpu.CompilerParams(dimension_semantics=("arbitrary",), vmem_limit_bytes=VMEM_LIMIT),
        name="out_proj",
    )(ycat, x, wout, g, wr, br)


def _dest_kernel(rec_ref, seg_ref, out_ref):
    rec = rec_ref[...]
    lane = lax.broadcasted_iota(jnp.int32, rec.shape, 1)
    lane_f = lane.astype(_F32)
    rows = jnp.zeros(rec.shape, _F32)
    for k in range(TOP_K):
        expert = rec[:, REC_EXPERT + k:REC_EXPERT + k + 1]
        start = jnp.sum(jnp.where(lane_f == expert, seg_ref[...], 0.0), axis=-1, keepdims=True)
        rows = jnp.where(lane == k, start + rec[:, REC_RANK + k:REC_RANK + k + 1], rows)
    out_ref[...] = jnp.transpose(rows)[0:SUBLANES, :].astype(jnp.int32)


def _dest_call(t, rec, seg_start):
    td = TM_DEST
    by_k = pl.pallas_call(
        _dest_kernel,
        grid=(t // td,),
        in_specs=[pl.BlockSpec((td, ROUTER_COLS), lambda i: (i, 0)), _resident(seg_start.shape)],
        out_specs=pl.BlockSpec((SUBLANES, td), lambda i: (0, i)),
        out_shape=jax.ShapeDtypeStruct((SUBLANES, t), jnp.int32),
        compiler_params=pltpu.CompilerParams(dimension_semantics=("arbitrary",), vmem_limit_bytes=VMEM_LIMIT),
        name="dest_rows",
    )(rec, seg_start)
    return by_k[:TOP_K].reshape(TOP_K, t // TM_OUT, TM_OUT).transpose(1, 0, 2)


def _dispatch_kernel(pad_start_ref, pad_rows_ref, dest_ref, hp_hbm, xs_hbm, buf, zero_s, lsem, sem, zsem):
    step = pl.program_id(0)
    last = pl.num_programs(0) - 1
    tm = dest_ref.shape[2]
    slot = lax.rem(step, 2)

    def stage(tile, to_slot):
        return pltpu.make_async_copy(hp_hbm.at[pl.ds(tile * tm, tm)], buf.at[to_slot], lsem.at[to_slot])

    def wait_tile():
        view = xs_hbm.at[pl.ds(0, TOP_K * tm)]
        pltpu.make_async_copy(view, view, sem).wait()

    @pl.when(step == 0)
    def _():
        stage(0, 0).start()
        zero_s[...] = jnp.zeros(zero_s.shape, zero_s.dtype)
        for e in range(pad_rows_ref.shape[0]):
            n = pad_rows_ref[e]

            @pl.when(n > 0)
            def _():
                pltpu.make_async_copy(zero_s.at[pl.ds(0, n)], xs_hbm.at[pl.ds(pad_start_ref[e], n)], zsem).start()

        for e in range(pad_rows_ref.shape[0]):
            n = pad_rows_ref[e]

            @pl.when(n > 0)
            def _():
                pltpu.make_async_copy(zero_s.at[pl.ds(0, n)], xs_hbm.at[pl.ds(pad_start_ref[e], n)], zsem).wait()

    stage(step, slot).wait()

    @pl.when(step > 0)
    def _():
        wait_tile()

    @pl.when(step < last)
    def _():
        stage(step + 1, 1 - slot).start()

    for j in range(tm):
        for k in range(TOP_K):
            pltpu.make_async_copy(buf.at[slot, j], xs_hbm.at[dest_ref[0, k, j]], sem).start(priority=k % 2)

    @pl.when(step == last)
    def _():
        wait_tile()


def _dispatch_call(t, p_len, pad_start, pad_rows, dest, hp):
    tm = TM_OUT
    grid_spec = pltpu.PrefetchScalarGridSpec(
        num_scalar_prefetch=2,
        grid=(t // tm,),
        in_specs=[pl.BlockSpec((1, TOP_K, tm), lambda i, *_: (i, 0, 0), memory_space=pltpu.SMEM),
                  pl.BlockSpec(memory_space=pl.ANY)],
        out_specs=pl.BlockSpec(memory_space=pl.ANY),
        scratch_shapes=[pltpu.VMEM((2, tm, TOK_TILE_ROWS, LANES), jnp.uint32),
                        pltpu.VMEM((BM, TOK_TILE_ROWS, LANES), jnp.uint32),
                        pltpu.SemaphoreType.DMA((2,)), pltpu.SemaphoreType.DMA(()), pltpu.SemaphoreType.DMA(())],
    )
    return pl.pallas_call(
        _dispatch_kernel,
        grid_spec=grid_spec,
        out_shape=jax.ShapeDtypeStruct((p_len, TOK_TILE_ROWS, LANES), jnp.uint32),
        compiler_params=pltpu.CompilerParams(dimension_semantics=("arbitrary",), vmem_limit_bytes=VMEM_LIMIT),
        name="dispatch",
    )(pad_start, pad_rows, dest, hp.reshape(t, TOK_TILE_ROWS, LANES))


def _expert_kernel(layer, be_ref, cnt_ref, nv_ref, first_ref, par_ref, nxt_ref, xs_ref, w1_hbm, w3_hbm, w2_hbm,
                   y_ref, w1_s, w3_s, w2_s, wsem):
    del nv_ref
    half = D_MODEL // 2
    step = pl.program_id(0)
    rows = cnt_ref[step]
    slot = par_ref[step]

    def weight_copies(e, to_slot):
        return (pltpu.make_async_copy(w1_hbm.at[layer, e], w1_s.at[to_slot], wsem.at[to_slot]),
                pltpu.make_async_copy(w3_hbm.at[layer, e], w3_s.at[to_slot], wsem.at[to_slot]),
                pltpu.make_async_copy(w2_hbm.at[layer, e], w2_s.at[to_slot], wsem.at[to_slot]))

    @pl.when(step == 0)
    def _():
        for cp in weight_copies(be_ref[0], slot):
            cp.start()

    @pl.when(first_ref[step] == 1)
    def _():
        for cp in weight_copies(be_ref[step], slot):
            cp.wait()

        @pl.when(nxt_ref[step] >= 0)
        def _():
            for cp in weight_copies(nxt_ref[step], 1 - slot):
                cp.start()

    @pl.when(rows == 0)
    def _():
        y_ref[...] = jnp.zeros(y_ref.shape, y_ref.dtype)

    @pl.when(rows > 0)
    def _():
        lo, hi = _unpack_bf16_pairs(_load_token_tiles(xs_ref, BM))
        a1 = (jnp.dot(lo, w1_s[slot, 0:half, :].astype(_BF16), preferred_element_type=_F32)
              + jnp.dot(hi, w1_s[slot, half:D_MODEL, :].astype(_BF16), preferred_element_type=_F32))
        a3 = (jnp.dot(lo, w3_s[slot, 0:half, :].astype(_BF16), preferred_element_type=_F32)
              + jnp.dot(hi, w3_s[slot, half:D_MODEL, :].astype(_BF16), preferred_element_type=_F32))
        hid = (a1 * _sigmoid(a1) * a3).astype(_BF16)
        y = jnp.dot(hid, w2_s[slot].astype(_BF16), preferred_element_type=_F32)
        _store_token_tiles(y_ref, _pack_bf16_pairs(y.astype(_BF16)))


def _expert_call(layer, n_blocks, plan, xs, w1, w3, w2):
    blk = lambda i, be, cnt, nv, *_: (jnp.minimum(i, nv[0] - 1), 0)
    grid_spec = pltpu.PrefetchScalarGridSpec(
        num_scalar_prefetch=len(plan),
        grid=(n_blocks,),
        in_specs=[
            pl.BlockSpec((BM * TOK_TILE_ROWS, LANES), blk),
            pl.BlockSpec(memory_space=pl.ANY),
            pl.BlockSpec(memory_space=pl.ANY),
            pl.BlockSpec(memory_space=pl.ANY),
        ],
        out_specs=pl.BlockSpec((BM * TOK_TILE_ROWS, LANES), lambda i, *_: (i, 0)),
        scratch_shapes=[
            pltpu.VMEM((2, D_MODEL, D_EXPERT), w1.dtype),
            pltpu.VMEM((2, D_MODEL, D_EXPERT), w3.dtype),
            pltpu.VMEM((2, D_EXPERT, D_MODEL), w2.dtype),
            pltpu.SemaphoreType.DMA((2,)),
        ],
    )
    ys = pl.pallas_call(
        functools.partial(_expert_kernel, layer),
        grid_spec=grid_spec,
        out_shape=jax.ShapeDtypeStruct((n_blocks * BM * TOK_TILE_ROWS, LANES), jnp.uint32),
        compiler_params=pltpu.CompilerParams(dimension_semantics=("arbitrary",), vmem_limit_bytes=VMEM_LIMIT),
        name="experts",
    )(*plan, xs.reshape(n_blocks * BM * TOK_TILE_ROWS, LANES), w1, w3, w2)
    return ys.reshape(n_blocks * BM, TOK_TILE_ROWS, LANES)


def _slot_tables(t, counts_f):
    n_assign = t * TOP_K
    n_blocks = n_assign // BM + N_EXPERTS
    counts = counts_f[0, :N_EXPERTS].astype(jnp.int32)
    padded = ((counts + BM - 1) // BM) * BM
    pend = jnp.cumsum(padded)
    pstart = pend - padded
    starts = jnp.arange(n_blocks, dtype=jnp.int32) * BM
    block_e = jnp.minimum(jnp.sum(pend[None, :] <= starts[:, None], axis=1), N_EXPERTS - 1).astype(jnp.int32)
    seg_end = (pstart + counts)[block_e]
    block_cnt = jnp.where(starts < pend[-1], jnp.clip(seg_end - starts, 0, BM), 0).astype(jnp.int32)
    n_valid = jnp.maximum(pend[-1] // BM, 1).astype(jnp.int32).reshape(1)
    real = starts < pend[-1]
    block_e = jnp.where(real, block_e, block_e[n_valid[0] - 1])
    first = (real & (block_e != jnp.concatenate([jnp.full((1,), -1, jnp.int32), block_e[:-1]]))).astype(jnp.int32)
    parity = jnp.maximum(jnp.cumsum(first) - 1, 0) % 2
    e_ids = jnp.arange(N_EXPERTS, dtype=jnp.int32)
    later = jnp.where((counts[None, :] > 0) & (e_ids[None, :] > e_ids[:, None]), e_ids[None, :], N_EXPERTS)
    next_e = jnp.min(later, axis=1)
    next_e = jnp.where(next_e < N_EXPERTS, next_e, -1)[block_e]
    plan = (block_e, block_cnt, n_valid, first, parity.astype(jnp.int32), next_e.astype(jnp.int32))
    seg_start = jnp.zeros((1, ROUTER_COLS), _F32).at[0, :N_EXPERTS].set(pstart.astype(_F32))
    tail = pend[-1] + jnp.arange(N_EXPERTS, dtype=jnp.int32) * BM
    pad_start = jnp.concatenate([pstart + counts, tail])
    pad_rows = jnp.concatenate([padded - counts, jnp.where(tail < n_blocks * BM, BM, 0)])
    return n_blocks, plan, seg_start, pad_start, pad_rows.astype(jnp.int32)


def _final_kernel(x_ref, rec_ref, y_hbm, dest_ref, dest_next_ref, g_ref, o_ref, ybuf, ysem):
    x = _combined_input(x_ref, rec_ref, y_hbm, dest_ref, dest_next_ref, ybuf, ysem)
    o_ref[...] = x * _rms_scale(x) * g_ref[...]


def _final_call(t, x, moe, g):
    tm = TM_OUT
    n_tiles = t // tm
    rec, ys, dest = moe
    in_specs, scratch = _combine_specs(tm, n_tiles)
    return pl.pallas_call(
        _final_kernel,
        grid=(n_tiles,),
        in_specs=in_specs + [_resident(g.shape)],
        out_specs=pl.BlockSpec((tm, D_MODEL), lambda i: (i, 0)),
        out_shape=jax.ShapeDtypeStruct((t, D_MODEL), _F32),
        scratch_shapes=scratch,
        compiler_params=pltpu.CompilerParams(dimension_semantics=("arbitrary",), vmem_limit_bytes=VMEM_LIMIT),
        name="final_norm",
    )(x, rec, ys, dest, dest, g)


def kernel(x, norm_mix, w_in, pool_w, pool_scale, sgu_ln_g, sgu_ln_b, sgu_w, sgu_b, conv_dw_w, conv_dw_b,
           conv_ln_g, conv_ln_b, conv_pw, w_out, norm_ffn, router_g_w, router_g_b, router_e_w, router_e_b,
           exp_w1, exp_w3, exp_w2, norm_final):
    b, s, d = x.shape
    depth = w_in.shape[0]
    assert d == D_MODEL and s % TM_MIX == 0 and TM_MIX % CHUNK == 0 and TM_MIX == TM_OUT and (b * s) % TM_PROJ == 0
    t = b * s
    xf = x.reshape(t, d)
    moe = None
    for l in range(depth):
        row2 = lambda v: v[l].reshape(1, -1)
        sbias = jnp.repeat(sgu_b[l].T, SGU_HEAD_DIM, axis=1)
        outs = _mixer_call(
            t, s, xf, moe, row2(norm_mix), w_in[l].astype(_BF16), pool_w[l].astype(_BF16),
            row2(pool_scale), row2(sgu_ln_g), row2(sgu_ln_b), sgu_w[l], sbias, conv_dw_w[l], row2(conv_dw_b),
            row2(conv_ln_g), row2(conv_ln_b), conv_pw[l].astype(_BF16))
        if moe is not None:
            ycat, xf = outs
        else:
            (ycat,) = outs
        pad = ROUTER_COLS - N_GROUPS - N_EXPERTS
        wr = jnp.concatenate([router_g_w[l], router_e_w[l], jnp.zeros((d, pad), _F32)], axis=1).astype(_BF16)
        br = jnp.concatenate([router_g_b[l], router_e_b[l], jnp.zeros((pad,), _F32)]).reshape(1, ROUTER_COLS)
        xf, hp, rec, counts = _out_call(t, ycat, xf, w_out[l].astype(_BF16), row2(norm_ffn), wr, br)
        n_blocks, plan, seg_start, pad_start, pad_rows = _slot_tables(t, counts)
        dest = _dest_call(t, rec, seg_start)
        xs = _dispatch_call(t, n_blocks * BM, pad_start, pad_rows, dest, hp)
        ys = _expert_call(l, n_blocks, plan, xs, exp_w1, exp_w3, exp_w2)
        moe = (rec, ys, dest)
    out = _final_call(t, xf, moe, norm_final.reshape(1, -1))
    return out.reshape(b, s, d)
```

```python
import functools

import jax
import jax.numpy as jnp
from jax import lax
from jax.experimental import pallas as pl
from jax.experimental.pallas import tpu as pltpu

D_MODEL = 2048
D_A = D_MODEL // 4
D_B = 3 * D_MODEL // 8
D_C = D_MODEL - D_A - D_B
POOL_WINDOWS = (2, 4, 8, 16)
POOL_GROUP_DIM = D_A // len(POOL_WINDOWS)
CHUNK = 128
SGU_HEAD_DIM = 128
SGU_HEADS = D_B // SGU_HEAD_DIM
CONV_WIDTH = 31
N_IN = D_A + 2 * D_B + 2 * D_C
N_GROUPS = 4
E_PER_GROUP = 8
N_EXPERTS = N_GROUPS * E_PER_GROUP
TOP_K = 2
D_EXPERT = D_MODEL // 4
EPS = 1e-6

LANES = 128
SUBLANES = 8
POOL_HALO = 16
CONV_HALO = 32
CONV_ROWS = 64
CONV_PITCH = 2
TM_MIX = 256
TM_OUT = 256
TM_PROJ = 512
PROJ_PARTS = 2
TM_DEST = 1024
BM = 256
TOK_TILE_ROWS = D_MODEL // 2 // LANES
ROUTER_COLS = LANES
REC_EXPERT = 0
REC_WEIGHT = TOP_K
REC_RANK = 2 * TOP_K
VMEM_LIMIT = 56 * 1024 * 1024

_F32 = jnp.float32
_BF16 = jnp.bfloat16


def _resident(shape):
    nd = len(shape)
    return pl.BlockSpec(shape, lambda *_: (0,) * nd, pipeline_mode=pl.Buffered(1))


def _rms_scale(x):
    return lax.rsqrt(jnp.mean(x * x, axis=-1, keepdims=True) + EPS)


def _layer_norm(x, g, b):
    mu = jnp.mean(x, axis=-1, keepdims=True)
    xc = x - mu
    return xc * lax.rsqrt(jnp.mean(xc * xc, axis=-1, keepdims=True) + EPS) * g + b


def _gelu_tanh(x):
    c = 0.7978845608028654
    half_x = 0.5 * x
    return half_x + half_x * jnp.tanh(x * (c + (c * 0.044715) * (x * x)))


def _sigmoid(x):
    return 1.0 / (1.0 + jnp.exp(-x))


def _pack_bf16_pairs(hb):
    u = lax.bitcast_convert_type(hb.astype(_F32), jnp.uint32)
    c = hb.shape[1] // 2
    return (u[:, c:] & jnp.uint32(0xFFFF0000)) | (u[:, :c] >> 16)


def _unpack_pairs_f32(p):
    return (lax.bitcast_convert_type(p << 16, _F32), lax.bitcast_convert_type(p & jnp.uint32(0xFFFF0000), _F32))


def _unpack_bf16_pairs(p):
    lo, hi = _unpack_pairs_f32(p)
    return lo.astype(_BF16), hi.astype(_BF16)


def _store_token_tiles(ref, packed):
    rows = packed.shape[0]
    for s in range(TOK_TILE_ROWS):
        ref[pl.ds(s, rows, stride=TOK_TILE_ROWS), :] = packed[:, s * LANES:(s + 1) * LANES]


def _load_token_tiles(ref, rows):
    return jnp.concatenate([ref[pl.ds(s, rows, stride=TOK_TILE_ROWS), :] for s in range(TOK_TILE_ROWS)], axis=1)


def _start_expert_rows(y_hbm, dest_ref, ybuf, sem, slot):
    tm = ybuf.shape[2] // TOK_TILE_ROWS
    for j in range(tm):
        for k in range(TOP_K):
            pltpu.make_async_copy(y_hbm.at[dest_ref[0, k, j]],
                                  ybuf.at[slot, k, pl.ds(j * TOK_TILE_ROWS, TOK_TILE_ROWS)], sem.at[slot]).start()


def _wait_expert_rows(ybuf, sem, slot):
    pltpu.make_async_copy(ybuf.at[slot], ybuf.at[slot], sem.at[slot]).wait()


def _fetch_expert_rows(y_hbm, dest_ref, dest_next_ref, ybuf, sem):
    step = pl.program_id(0)
    slot = lax.rem(step, 2)

    @pl.when(step == 0)
    def _():
        _start_expert_rows(y_hbm, dest_ref, ybuf, sem, 0)

    for s in range(2):
        @pl.when((step < pl.num_programs(0) - 1) & (slot == s))
        def _(s=s):
            _start_expert_rows(y_hbm, dest_next_ref, ybuf, sem, 1 - s)

    _wait_expert_rows(ybuf, sem, slot)


def _combine(x_ref, rec_ref, ybuf, slot):
    rec = rec_ref[...]
    tm = x_ref.shape[0]
    moe = None
    for k in range(TOP_K):
        yk = jnp.concatenate(_unpack_pairs_f32(_load_token_tiles(ybuf.at[slot, k], tm)), axis=1)
        term = rec[:, REC_WEIGHT + k:REC_WEIGHT + k + 1] * yk
        moe = term if moe is None else moe + term
    return x_ref[...] + moe


def _combined_input(x_ref, rec_ref, y_hbm, dest_ref, dest_next_ref, ybuf, sem):
    _fetch_expert_rows(y_hbm, dest_ref, dest_next_ref, ybuf, sem)
    return _combine(x_ref, rec_ref, ybuf, lax.rem(pl.program_id(0), 2))


def _combine_specs(tm, n_tiles):
    row = lambda i: (jnp.minimum(i, n_tiles - 1), 0)
    in_specs = [
        pl.BlockSpec((tm, D_MODEL), row),
        pl.BlockSpec((tm, ROUTER_COLS), row),
        pl.BlockSpec(memory_space=pl.ANY),
        pl.BlockSpec((1, TOP_K, tm), lambda i: (jnp.minimum(i, n_tiles - 1), 0, 0), memory_space=pltpu.SMEM),
        pl.BlockSpec((1, TOP_K, tm), lambda i: (jnp.minimum(i + 1, n_tiles - 1), 0, 0), memory_space=pltpu.SMEM),
    ]
    scratch = [pltpu.VMEM((2, TOP_K, tm * TOK_TILE_ROWS, LANES), jnp.uint32), pltpu.SemaphoreType.DMA((2,))]
    return in_specs, scratch


def _mixer_kernel(combine, tiles_per_seq, *refs):
    if combine:
        x_ref, rec_ref, y_hbm, dest_ref, dest_next_ref = refs[:5]
        refs = refs[5:]
    else:
        x_ref = refs[0]
        refs = refs[1:]
    (g_ref, win_ref, poolw_ref, pscale_ref, sg_ref, sb_ref, sw_ref, sbias_ref,
     dww_ref, dwb_ref, cg_ref, cb_ref, pw_ref) = refs[:13]
    refs = refs[13:]
    if combine:
        ycat_ref, xo_ref = refs[:2]
        h_s, xa_s, z_s, cv_s, ybuf, ysem = refs[2:]
    else:
        ycat_ref = refs[0]
        h_s, xa_s, z_s, cv_s = refs[1:]
    tm = ycat_ref.shape[0]

    tile = pl.program_id(0) % tiles_per_seq
    first = tile == 0

    if combine:
        x = _combined_input(x_ref, rec_ref, y_hbm, dest_ref, dest_next_ref, ybuf, ysem)
        xo_ref[...] = x
    else:
        x = x_ref[...]
    n_ct = D_C // LANES
    halo_src = pl.ds(CONV_PITCH * tm, CONV_HALO, stride=CONV_PITCH)
    halo_dst = pl.ds(0, CONV_HALO, stride=CONV_PITCH)

    n_pg = len(POOL_WINDOWS)
    pool_src = pl.ds(CONV_PITCH * tm, POOL_HALO, stride=CONV_PITCH)
    pool_dst = pl.ds(0, POOL_HALO, stride=CONV_PITCH)

    @pl.when(first)
    def _():
        for gi in range(n_pg):
            xa_s[gi, pool_dst, :] = jnp.zeros((POOL_HALO, LANES), _F32)
        for c in range(n_ct):
            z_s[c, halo_dst, :] = jnp.zeros((CONV_HALO, LANES), _F32)

    @pl.when(jnp.logical_not(first))
    def _():
        for gi in range(n_pg):
            xa_s[gi, pool_dst, :] = xa_s[gi, pool_src, :]
        for c in range(n_ct):
            z_s[c, halo_dst, :] = z_s[c, halo_src, :]

    h_s[...] = (x * _rms_scale(x) * g_ref[...]).astype(_BF16)

    pc = jnp.dot(h_s[...], win_ref[:, D_A + 2 * D_B:N_IN], preferred_element_type=_F32)
    zc = pc[:, :D_C] * _sigmoid(pc[:, D_C:])
    for c in range(n_ct):
        z_s[c, pl.ds(CONV_PITCH * CONV_HALO, tm, stride=CONV_PITCH), :] = zc[:, c * LANES:(c + 1) * LANES]
    for r0 in range(0, tm, CONV_ROWS):
        for c in range(n_ct):
            lanes = slice(c * LANES, (c + 1) * LANES)
            acc = jnp.zeros((CONV_ROWS, LANES), _F32)
            for k in range(CONV_WIDTH):
                off = r0 + CONV_HALO - (CONV_WIDTH - 1) + k
                acc = acc + dww_ref[k:k + 1, lanes] * z_s[c, pl.ds(CONV_PITCH * off, CONV_ROWS, stride=CONV_PITCH), :]
            cv_s[r0:r0 + CONV_ROWS, lanes] = acc + dwb_ref[:, lanes]

    pa = jnp.dot(h_s[...], win_ref[:, 0:D_A], preferred_element_type=_F32)
    for gi in range(n_pg):
        xa_s[gi, pl.ds(CONV_PITCH * POOL_HALO, tm, stride=CONV_PITCH), :] = pa[:, gi * LANES:(gi + 1) * LANES]
    pos = tile * tm + lax.broadcasted_iota(jnp.int32, (tm, 1), 0)
    for gi, w in enumerate(POOL_WINDOWS):
        cols = slice(gi * POOL_GROUP_DIM, (gi + 1) * POOL_GROUP_DIM)
        xg = pa[:, cols]
        acc = xg
        for k in range(1, w):
            acc = acc + xa_s[gi, pl.ds(CONV_PITCH * (POOL_HALO - k), tm, stride=CONV_PITCH), :]
        cnt = jnp.minimum(pos + 1, w).astype(_F32)
        d = acc / cnt - xg
        yg = jnp.dot(d.astype(_BF16), poolw_ref[gi], preferred_element_type=_F32) * pscale_ref[:, cols]
        ycat_ref[:, cols] = yg.astype(_BF16)

    pb = jnp.dot(h_s[...], win_ref[:, D_A:D_A + 2 * D_B], preferred_element_type=_F32)
    zb = _gelu_tanh(pb)
    u = zb[:, :D_B]
    vb = _layer_norm(zb[:, D_B:], sg_ref[...], sb_ref[...]).astype(_BF16)
    r_i = lax.broadcasted_iota(jnp.int32, (CHUNK, CHUNK), 0)
    c_i = lax.broadcasted_iota(jnp.int32, (CHUNK, CHUNK), 1)
    causal = r_i >= c_i
    for hd in range(SGU_HEADS):
        hc = slice(hd * SGU_HEAD_DIM, (hd + 1) * SGU_HEAD_DIM)
        wm = jnp.where(causal, sw_ref[hd], 0.0).astype(_BF16)
        for c in range(tm // CHUNK):
            rows = slice(c * CHUNK, (c + 1) * CHUNK)
            gate = jnp.dot(wm, vb[rows, hc], preferred_element_type=_F32) + sbias_ref[:, hc]
            ycat_ref[rows, D_A + hd * SGU_HEAD_DIM:D_A + (hd + 1) * SGU_HEAD_DIM] = (u[rows, hc] * gate).astype(_BF16)

    zn = _layer_norm(cv_s[...], cg_ref[...], cb_ref[...])
    zs = (zn * _sigmoid(zn)).astype(_BF16)
    ycat_ref[:, D_A + D_B:D_MODEL] = jnp.dot(zs, pw_ref[...], preferred_element_type=_F32).astype(_BF16)


def _mixer_call(t, seq, x, moe, g, win, poolw, pscale, sg, sb, sw, sbias, dww, dwb, cg, cb, pw):
    combine = moe is not None
    tm = TM_MIX if combine else 2 * TM_MIX
    n_tiles = t // tm
    row = lambda i: (i, 0)
    scratch = [
        pltpu.VMEM((tm, D_MODEL), _BF16),
        pltpu.VMEM((D_A // LANES, CONV_PITCH * (POOL_HALO + tm), LANES), _F32),
        pltpu.VMEM((D_C // LANES, CONV_PITCH * (CONV_HALO + tm), LANES), _F32),
        pltpu.VMEM((tm, D_C), _F32),
    ]
    if combine:
        rec, ys, dest = moe
        in_specs, extra = _combine_specs(tm, n_tiles)
        scratch += extra
        args = [x, rec, ys, dest, dest]
    else:
        in_specs = [pl.BlockSpec((tm, D_MODEL), row)]
        args = [x]
    weights = [g, win, poolw, pscale, sg, sb, sw, sbias, dww, dwb, cg, cb, pw]
    in_specs += [_resident(w.shape) for w in weights]
    args += weights
    out_shape = [jax.ShapeDtypeStruct((t, D_MODEL), _BF16)]
    out_specs = [pl.BlockSpec((tm, D_MODEL), row)]
    if combine:
        out_shape.append(jax.ShapeDtypeStruct((t, D_MODEL), _F32))
        out_specs.append(pl.BlockSpec((tm, D_MODEL), row))
    return pl.pallas_call(
        functools.partial(_mixer_kernel, combine, seq // tm),
        grid=(n_tiles,),
        in_specs=in_specs,
        out_specs=out_specs,
        out_shape=out_shape,
        scratch_shapes=scratch,
        compiler_params=pltpu.CompilerParams(dimension_semantics=("arbitrary",), vmem_limit_bytes=VMEM_LIMIT),
        name="mixer_combine" if combine else "mixer",
    )(*args)


def _route_rows(logits, seen):
    rows = logits.shape[0]
    lane = lax.broadcasted_iota(jnp.int32, logits.shape, 1)
    neg = jnp.float32(-jnp.inf)
    big = jnp.int32(ROUTER_COLS)

    def first_max(v):
        m = jnp.max(v, axis=-1, keepdims=True)
        return m, jnp.min(jnp.where(v == m, lane, big), axis=-1, keepdims=True)

    lg = jnp.where(lane < N_GROUPS, logits, neg)
    mg, g_idx = first_max(lg)
    g_w = 1.0 / jnp.sum(jnp.exp(lg - mg), axis=-1, keepdims=True)
    e_lo = N_GROUPS + g_idx * E_PER_GROUP
    le = jnp.where((lane >= e_lo) & (lane < e_lo + E_PER_GROUP), logits, neg)
    m1, i1 = first_max(le)
    m2, i2 = first_max(jnp.where(lane == i1, neg, le))
    p2 = jnp.exp(m2 - m1)
    w1 = g_w / (1.0 + p2)
    w2 = g_w * p2 / (1.0 + p2)
    e1 = i1 - N_GROUPS
    e2 = i2 - N_GROUPS

    hit1 = lane == e1
    hit2 = lane == e2
    onehot = jnp.where(hit1, 1.0, 0.0) + jnp.where(hit2, 1.0, 0.0)
    r_i = lax.broadcasted_iota(jnp.int32, (rows, rows), 0)
    c_i = lax.broadcasted_iota(jnp.int32, (rows, rows), 1)
    earlier = jnp.where(r_i > c_i, 1.0, 0.0).astype(_BF16)
    before = seen + jnp.dot(earlier, onehot.astype(_BF16), preferred_element_type=_F32)
    rank1 = jnp.sum(jnp.where(hit1, before, 0.0), axis=-1, keepdims=True)
    rank2 = jnp.sum(jnp.where(hit2, before, 0.0), axis=-1, keepdims=True)
    seen = seen + jnp.sum(onehot, axis=0, keepdims=True)

    rec = jnp.zeros(logits.shape, _F32)
    for k, v in ((REC_EXPERT, e1.astype(_F32)), (REC_EXPERT + 1, e2.astype(_F32)), (REC_WEIGHT, w1),
                 (REC_WEIGHT + 1, w2), (REC_RANK, rank1), (REC_RANK + 1, rank2)):
        rec = jnp.where(lane == k, v, rec)
    return rec, seen


def _out_kernel(ycat_ref, x_ref, wout_ref, g_ref, wr_ref, br_ref, x1_ref, hp_ref, rec_ref, cnt_ref, lg_s, seen_s):
    step = pl.program_id(0)
    slot = lax.rem(step, 2)

    @pl.when(step == 0)
    def _():
        lg_s[...] = jnp.zeros(lg_s.shape, _F32)
        seen_s[...] = jnp.zeros(seen_s.shape, _F32)

    prev_logits = lg_s[1 - slot]
    prev_seen = seen_s[...]

    tm = x_ref.shape[0]
    part = tm // PROJ_PARTS
    for q in range(PROJ_PARTS):
        rows = pl.ds(q * part, part)
        x1 = x_ref[rows, :] + jnp.dot(ycat_ref[rows, :], wout_ref[...], preferred_element_type=_F32)
        x1_ref[rows, :] = x1
        hb = (x1 * _rms_scale(x1) * g_ref[...]).astype(_BF16)
        lg_s[slot, rows, :] = jnp.dot(hb, wr_ref[...], preferred_element_type=_F32) + br_ref[...]
        _store_token_tiles(hp_ref.at[pl.ds(q * part * TOK_TILE_ROWS, part * TOK_TILE_ROWS)], _pack_bf16_pairs(hb))

    rec, seen = _route_rows(prev_logits, prev_seen)
    rec_ref[...] = rec
    seen = jnp.where(step == 0, 0.0, seen)
    seen_s[...] = seen
    cnt_ref[...] = jnp.broadcast_to(seen, cnt_ref.shape)


def _out_call(t, ycat, x, wout, g, wr, br):
    tm = TM_PROJ
    n_tiles = t // tm
    row = lambda i: (jnp.minimum(i, n_tiles - 1), 0)
    prev = lambda i: (jnp.maximum(i - 1, 0), 0)
    return pl.pallas_call(
        _out_kernel,
        grid=(n_tiles + 1,),
        in_specs=[pl.BlockSpec((tm, D_MODEL), row), pl.BlockSpec((tm, D_MODEL), row),
                  _resident(wout.shape), _resident(g.shape), _resident(wr.shape), _resident(br.shape)],
        out_specs=[pl.BlockSpec((tm, D_MODEL), row), pl.BlockSpec((tm * TOK_TILE_ROWS, LANES), row),
                   pl.BlockSpec((tm, ROUTER_COLS), prev), pl.BlockSpec((SUBLANES, ROUTER_COLS), lambda i: (0, 0))],
        out_shape=[jax.ShapeDtypeStruct((t, D_MODEL), _F32),
                   jax.ShapeDtypeStruct((t * TOK_TILE_ROWS, LANES), jnp.uint32),
                   jax.ShapeDtypeStruct((t, ROUTER_COLS), _F32),
                   jax.ShapeDtypeStruct((SUBLANES, ROUTER_COLS), _F32)],
        scratch_shapes=[pltpu.VMEM((2, tm, ROUTER_COLS), _F32), pltpu.VMEM((1, ROUTER_COLS), _F32)],
        compiler_params=pltpu.CompilerParams(dimension_semantics=("arbitrary",), vmem_limit_bytes=VMEM_LIMIT),
        name="out_proj",
    )(ycat, x, wout, g, wr, br)


def _dest_kernel(rec_ref, seg_ref, out_ref):
    rec = rec_ref[...]
    lane = lax.broadcasted_iota(jnp.int32, rec.shape, 1)
    lane_f = lane.astype(_F32)
    rows = jnp.zeros(rec.shape, _F32)
    for k in range(TOP_K):
        expert = rec[:, REC_EXPERT + k:REC_EXPERT + k + 1]
        start = jnp.sum(jnp.where(lane_f == expert, seg_ref[...], 0.0), axis=-1, keepdims=True)
        rows = jnp.where(lane == k, start + rec[:, REC_RANK + k:REC_RANK + k + 1], rows)
    out_ref[...] = jnp.transpose(rows)[0:SUBLANES, :].astype(jnp.int32)


def _dest_call(t, rec, seg_start):
    td = TM_DEST
    by_k = pl.pallas_call(
        _dest_kernel,
        grid=(t // td,),
        in_specs=[pl.BlockSpec((td, ROUTER_COLS), lambda i: (i, 0)), _resident(seg_start.shape)],
        out_specs=pl.BlockSpec((SUBLANES, td), lambda i: (0, i)),
        out_shape=jax.ShapeDtypeStruct((SUBLANES, t), jnp.int32),
        compiler_params=pltpu.CompilerParams(dimension_semantics=("arbitrary",), vmem_limit_bytes=VMEM_LIMIT),
        name="dest_rows",
    )(rec, seg_start)
    return by_k[:TOP_K].reshape(TOP_K, t // TM_OUT, TM_OUT).transpose(1, 0, 2)


def _dispatch_kernel(pad_start_ref, pad_rows_ref, dest_ref, hp_hbm, xs_hbm, buf, zero_s, lsem, sem, zsem):
    step = pl.program_id(0)
    last = pl.num_programs(0) - 1
    tm = dest_ref.shape[2]
    slot = lax.rem(step, 2)

    def stage(tile, to_slot):
        return pltpu.make_async_copy(hp_hbm.at[pl.ds(tile * tm, tm)], buf.at[to_slot], lsem.at[to_slot])

    def wait_tile():
        view = xs_hbm.at[pl.ds(0, TOP_K * tm)]
        pltpu.make_async_copy(view, view, sem).wait()

    @pl.when(step == 0)
    def _():
        stage(0, 0).start()
        zero_s[...] = jnp.zeros(zero_s.shape, zero_s.dtype)
        for e in range(pad_rows_ref.shape[0]):
            n = pad_rows_ref[e]

            @pl.when(n > 0)
            def _():
                pltpu.make_async_copy(zero_s.at[pl.ds(0, n)], xs_hbm.at[pl.ds(pad_start_ref[e], n)], zsem).start()

        for e in range(pad_rows_ref.shape[0]):
            n = pad_rows_ref[e]

            @pl.when(n > 0)
            def _():
                pltpu.make_async_copy(zero_s.at[pl.ds(0, n)], xs_hbm.at[pl.ds(pad_start_ref[e], n)], zsem).wait()

    stage(step, slot).wait()

    @pl.when(step > 0)
    def _():
        wait_tile()

    @pl.when(step < last)
    def _():
        stage(step + 1, 1 - slot).start()

    for j in range(tm):
        for k in range(TOP_K):
            pltpu.make_async_copy(buf.at[slot, j], xs_hbm.at[dest_ref[0, k, j]], sem).start(priority=k % 2)

    @pl.when(step == last)
    def _():
        wait_tile()


def _dispatch_call(t, p_len, pad_start, pad_rows, dest, hp):
    tm = TM_OUT
    grid_spec = pltpu.PrefetchScalarGridSpec(
        num_scalar_prefetch=2,
        grid=(t // tm,),
        in_specs=[pl.BlockSpec((1, TOP_K, tm), lambda i, *_: (i, 0, 0), memory_space=pltpu.SMEM),
                  pl.BlockSpec(memory_space=pl.ANY)],
        out_specs=pl.BlockSpec(memory_space=pl.ANY),
        scratch_shapes=[pltpu.VMEM((2, tm, TOK_TILE_ROWS, LANES), jnp.uint32),
                        pltpu.VMEM((BM, TOK_TILE_ROWS, LANES), jnp.uint32),
                        pltpu.SemaphoreType.DMA((2,)), pltpu.SemaphoreType.DMA(()), pltpu.SemaphoreType.DMA(())],
    )
    return pl.pallas_call(
        _dispatch_kernel,
        grid_spec=grid_spec,
        out_shape=jax.ShapeDtypeStruct((p_len, TOK_TILE_ROWS, LANES), jnp.uint32),
        compiler_params=pltpu.CompilerParams(dimension_semantics=("arbitrary",), vmem_limit_bytes=VMEM_LIMIT),
        name="dispatch",
    )(pad_start, pad_rows, dest, hp.reshape(t, TOK_TILE_ROWS, LANES))


def _expert_kernel(layer, be_ref, cnt_ref, nv_ref, first_ref, par_ref, nxt_ref, xs_ref, w1_hbm, w3_hbm, w2_hbm,
                   y_ref, w1_s, w3_s, w2_s, wsem):
    del nv_ref
    half = D_MODEL // 2
    step = pl.program_id(0)
    rows = cnt_ref[step]
    slot = par_ref[step]

    def weight_copies(e, to_slot):
        return (pltpu.make_async_copy(w1_hbm.at[layer, e], w1_s.at[to_slot], wsem.at[to_slot]),
                pltpu.make_async_copy(w3_hbm.at[layer, e], w3_s.at[to_slot], wsem.at[to_slot]),
                pltpu.make_async_copy(w2_hbm.at[layer, e], w2_s.at[to_slot], wsem.at[to_slot]))

    @pl.when(step == 0)
    def _():
        for cp in weight_copies(be_ref[0], slot):
            cp.start()

    @pl.when(first_ref[step] == 1)
    def _():
        for cp in weight_copies(be_ref[step], slot):
            cp.wait()

        @pl.when(nxt_ref[step] >= 0)
        def _():
            for cp in weight_copies(nxt_ref[step], 1 - slot):
                cp.start()

    @pl.when(rows == 0)
    def _():
        y_ref[...] = jnp.zeros(y_ref.shape, y_ref.dtype)

    @pl.when(rows > 0)
    def _():
        lo, hi = _unpack_bf16_pairs(_load_token_tiles(xs_ref, BM))
        a1 = (jnp.dot(lo, w1_s[slot, 0:half, :].astype(_BF16), preferred_element_type=_F32)
              + jnp.dot(hi, w1_s[slot, half:D_MODEL, :].astype(_BF16), preferred_element_type=_F32))
        a3 = (jnp.dot(lo, w3_s[slot, 0:half, :].astype(_BF16), preferred_element_type=_F32)
              + jnp.dot(hi, w3_s[slot, half:D_MODEL, :].astype(_BF16), preferred_element_type=_F32))
        hid = (a1 * _sigmoid(a1) * a3).astype(_BF16)
        y = jnp.dot(hid, w2_s[slot].astype(_BF16), preferred_element_type=_F32)
        _store_token_tiles(y_ref, _pack_bf16_pairs(y.astype(_BF16)))


def _expert_call(layer, n_blocks, plan, xs, w1, w3, w2):
    blk = lambda i, be, cnt, nv, *_: (jnp.minimum(i, nv[0] - 1), 0)
    grid_spec = pltpu.PrefetchScalarGridSpec(
        num_scalar_prefetch=len(plan),
        grid=(n_blocks,),
        in_specs=[
            pl.BlockSpec((BM * TOK_TILE_ROWS, LANES), blk),
            pl.BlockSpec(memory_space=pl.ANY),
            pl.BlockSpec(memory_space=pl.ANY),
            pl.BlockSpec(memory_space=pl.ANY),
        ],
        out_specs=pl.BlockSpec((BM * TOK_TILE_ROWS, LANES), lambda i, *_: (i, 0)),
        scratch_shapes=[
            pltpu.VMEM((2, D_MODEL, D_EXPERT), w1.dtype),
            pltpu.VMEM((2, D_MODEL, D_EXPERT), w3.dtype),
            pltpu.VMEM((2, D_EXPERT, D_MODEL), w2.dtype),
            pltpu.SemaphoreType.DMA((2,)),
        ],
    )
    ys = pl.pallas_call(
        functools.partial(_expert_kernel, layer),
        grid_spec=grid_spec,
        out_shape=jax.ShapeDtypeStruct((n_blocks * BM * TOK_TILE_ROWS, LANES), jnp.uint32),
        compiler_params=pltpu.CompilerParams(dimension_semantics=("arbitrary",), vmem_limit_bytes=VMEM_LIMIT),
        name="experts",
    )(*plan, xs.reshape(n_blocks * BM * TOK_TILE_ROWS, LANES), w1, w3, w2)
    return ys.reshape(n_blocks * BM, TOK_TILE_ROWS, LANES)


def _slot_tables(t, counts_f):
    n_assign = t * TOP_K
    n_blocks = n_assign // BM + N_EXPERTS
    counts = counts_f[0, :N_EXPERTS].astype(jnp.int32)
    padded = ((counts + BM - 1) // BM) * BM
    pend = jnp.cumsum(padded)
    pstart = pend - padded
    starts = jnp.arange(n_blocks, dtype=jnp.int32) * BM
    block_e = jnp.minimum(jnp.sum(pend[None, :] <= starts[:, None], axis=1), N_EXPERTS - 1).astype(jnp.int32)
    seg_end = (pstart + counts)[block_e]
    block_cnt = jnp.where(starts < pend[-1], jnp.clip(seg_end - starts, 0, BM), 0).astype(jnp.int32)
    n_valid = jnp.maximum(pend[-1] // BM, 1).astype(jnp.int32).reshape(1)
    real = starts < pend[-1]
    block_e = jnp.where(real, block_e, block_e[n_valid[0] - 1])
    first = (real & (block_e != jnp.concatenate([jnp.full((1,), -1, jnp.int32), block_e[:-1]]))).astype(jnp.int32)
    parity = jnp.maximum(jnp.cumsum(first) - 1, 0) % 2
    e_ids = jnp.arange(N_EXPERTS, dtype=jnp.int32)
    later = jnp.where((counts[None, :] > 0) & (e_ids[None, :] > e_ids[:, None]), e_ids[None, :], N_EXPERTS)
    next_e = jnp.min(later, axis=1)
    next_e = jnp.where(next_e < N_EXPERTS, next_e, -1)[block_e]
    plan = (block_e, block_cnt, n_valid, first, parity.astype(jnp.int32), next_e.astype(jnp.int32))
    seg_start = jnp.zeros((1, ROUTER_COLS), _F32).at[0, :N_EXPERTS].set(pstart.astype(_F32))
    tail = pend[-1] + jnp.arange(N_EXPERTS, dtype=jnp.int32) * BM
    pad_start = jnp.concatenate([pstart + counts, tail])
    pad_rows = jnp.concatenate([padded - counts, jnp.where(tail < n_blocks * BM, BM, 0)])
    return n_blocks, plan, seg_start, pad_start, pad_rows.astype(jnp.int32)


def _final_kernel(x_ref, rec_ref, y_hbm, dest_ref, dest_next_ref, g_ref, o_ref, ybuf, ysem):
    x = _combined_input(x_ref, rec_ref, y_hbm, dest_ref, dest_next_ref, ybuf, ysem)
    o_ref[...] = x * _rms_scale(x) * g_ref[...]


def _final_call(t, x, moe, g):
    tm = TM_OUT
    n_tiles = t // tm
    rec, ys, dest = moe
    in_specs, scratch = _combine_specs(tm, n_tiles)
    return pl.pallas_call(
        _final_kernel,
        grid=(n_tiles,),
        in_specs=in_specs + [_resident(g.shape)],
        out_specs=pl.BlockSpec((tm, D_MODEL), lambda i: (i, 0)),
        out_shape=jax.ShapeDtypeStruct((t, D_MODEL), _F32),
        scratch_shapes=scratch,
        compiler_params=pltpu.CompilerParams(dimension_semantics=("arbitrary",), vmem_limit_bytes=VMEM_LIMIT),
        name="final_norm",
    )(x, rec, ys, dest, dest, g)


def kernel(x, norm_mix, w_in, pool_w, pool_scale, sgu_ln_g, sgu_ln_b, sgu_w, sgu_b, conv_dw_w, conv_dw_b,
           conv_ln_g, conv_ln_b, conv_pw, w_out, norm_ffn, router_g_w, router_g_b, router_e_w, router_e_b,
           exp_w1, exp_w3, exp_w2, norm_final):
    b, s, d = x.shape
    depth = w_in.shape[0]
    assert d == D_MODEL and s % TM_MIX == 0 and TM_MIX % CHUNK == 0 and TM_MIX == TM_OUT and (b * s) % TM_PROJ == 0
    t = b * s
    xf = x.reshape(t, d)
    moe = None
    for l in range(depth):
        row2 = lambda v: v[l].reshape(1, -1)
        sbias = jnp.repeat(sgu_b[l].T, SGU_HEAD_DIM, axis=1)
        outs = _mixer_call(
            t, s, xf, moe, row2(norm_mix), w_in[l].astype(_BF16), pool_w[l].astype(_BF16),
            row2(pool_scale), row2(sgu_ln_g), row2(sgu_ln_b), sgu_w[l], sbias, conv_dw_w[l], row2(conv_dw_b),
            row2(conv_ln_g), row2(conv_ln_b), conv_pw[l].astype(_BF16))
        if moe is not None:
            ycat, xf = outs
        else:
            (ycat,) = outs
        pad = ROUTER_COLS - N_GROUPS - N_EXPERTS
        wr = jnp.concatenate([router_g_w[l], router_e_w[l], jnp.zeros((d, pad), _F32)], axis=1).astype(_BF16)
        br = jnp.concatenate([router_g_b[l], router_e_b[l], jnp.zeros((pad,), _F32)]).reshape(1, ROUTER_COLS)
        xf, hp, rec, counts = _out_call(t, ycat, xf, w_out[l].astype(_BF16), row2(norm_ffn), wr, br)
        n_blocks, plan, seg_start, pad_start, pad_rows = _slot_tables(t, counts)
        dest = _dest_call(t, rec, seg_start)
        xs = _dispatch_call(t, n_blocks * BM, pad_start, pad_rows, dest, hp)
        ys = _expert_call(l, n_blocks, plan, xs, exp_w1, exp_w3, exp_w2)
        moe = (rec, ys, dest)
    out = _final_call(t, xf, moe, norm_final.reshape(1, -1))
    return out.reshape(b, s, d)
```

```python
import functools

import jax
import jax.numpy as jnp
from jax import lax
from jax.experimental import pallas as pl
from jax.experimental.pallas import tpu as pltpu

D_MODEL = 2048
D_A = D_MODEL // 4
D_B = 3 * D_MODEL // 8
D_C = D_MODEL - D_A - D_B
POOL_WINDOWS = (2, 4, 8, 16)
POOL_GROUP_DIM = D_A // len(POOL_WINDOWS)
CHUNK = 128
SGU_HEAD_DIM = 128
SGU_HEADS = D_B // SGU_HEAD_DIM
CONV_WIDTH = 31
N_IN = D_A + 2 * D_B + 2 * D_C
N_GROUPS = 4
E_PER_GROUP = 8
N_EXPERTS = N_GROUPS * E_PER_GROUP
TOP_K = 2
D_EXPERT = D_MODEL // 4
EPS = 1e-6

LANES = 128
SUBLANES = 8
POOL_HALO = 16
CONV_HALO = 32
CONV_ROWS = 64
CONV_PITCH = 2
TM_MIX = 256
TM_OUT = 256
TM_PROJ = 512
PROJ_PARTS = 2
TM_DEST = 1024
BM = 256
TOK_TILE_ROWS = D_MODEL // 2 // LANES
ROUTER_COLS = LANES
REC_EXPERT = 0
REC_WEIGHT = TOP_K
REC_RANK = 2 * TOP_K
VMEM_LIMIT = 56 * 1024 * 1024

_F32 = jnp.float32
_BF16 = jnp.bfloat16


def _resident(shape):
    nd = len(shape)
    return pl.BlockSpec(shape, lambda *_: (0,) * nd, pipeline_mode=pl.Buffered(1))


def _rms_scale(x):
    return lax.rsqrt(jnp.mean(x * x, axis=-1, keepdims=True) + EPS)


def _layer_norm(x, g, b):
    mu = jnp.mean(x, axis=-1, keepdims=True)
    xc = x - mu
    return xc * lax.rsqrt(jnp.mean(xc * xc, axis=-1, keepdims=True) + EPS) * g + b


def _gelu_tanh(x):
    c = 0.7978845608028654
    half_x = 0.5 * x
    return half_x + half_x * jnp.tanh(x * (c + (c * 0.044715) * (x * x)))


def _sigmoid(x):
    return 1.0 / (1.0 + jnp.exp(-x))


def _pack_bf16_pairs(hb):
    u = lax.bitcast_convert_type(hb.astype(_F32), jnp.uint32)
    c = hb.shape[1] // 2
    return (u[:, c:] & jnp.uint32(0xFFFF0000)) | (u[:, :c] >> 16)


def _unpack_pairs_f32(p):
    return (lax.bitcast_convert_type(p << 16, _F32), lax.bitcast_convert_type(p & jnp.uint32(0xFFFF0000), _F32))


def _unpack_bf16_pairs(p):
    lo, hi = _unpack_pairs_f32(p)
    return lo.astype(_BF16), hi.astype(_BF16)


def _store_token_tiles(ref, packed):
    rows = packed.shape[0]
    for s in range(TOK_TILE_ROWS):
        ref[pl.ds(s, rows, stride=TOK_TILE_ROWS), :] = packed[:, s * LANES:(s + 1) * LANES]


def _load_token_tiles(ref, rows):
    return jnp.concatenate([ref[pl.ds(s, rows, stride=TOK_TILE_ROWS), :] for s in range(TOK_TILE_ROWS)], axis=1)


def _start_expert_rows(y_hbm, dest_ref, ybuf, sem, slot):
    tm = ybuf.shape[2] // TOK_TILE_ROWS
    for j in range(tm):
        for k in range(TOP_K):
            pltpu.make_async_copy(y_hbm.at[dest_ref[0, k, j]],
                                  ybuf.at[slot, k, pl.ds(j * TOK_TILE_ROWS, TOK_TILE_ROWS)],
                                  sem.at[slot]).start(priority=k % 2)


def _wait_expert_rows(ybuf, sem, slot):
    pltpu.make_async_copy(ybuf.at[slot], ybuf.at[slot], sem.at[slot]).wait()


def _fetch_expert_rows(y_hbm, dest_ref, dest_next_ref, ybuf, sem):
    step = pl.program_id(0)
    slot = lax.rem(step, 2)

    @pl.when(step == 0)
    def _():
        _start_expert_rows(y_hbm, dest_ref, ybuf, sem, 0)

    for s in range(2):
        @pl.when((step < pl.num_programs(0) - 1) & (slot == s))
        def _(s=s):
            _start_expert_rows(y_hbm, dest_next_ref, ybuf, sem, 1 - s)

    _wait_expert_rows(ybuf, sem, slot)


def _combine(x_ref, rec_ref, ybuf, slot):
    rec = rec_ref[...]
    tm = x_ref.shape[0]
    moe = None
    for k in range(TOP_K):
        yk = jnp.concatenate(_unpack_pairs_f32(_load_token_tiles(ybuf.at[slot, k], tm)), axis=1)
        term = rec[:, REC_WEIGHT + k:REC_WEIGHT + k + 1] * yk
        moe = term if moe is None else moe + term
    return x_ref[...] + moe


def _combined_input(x_ref, rec_ref, y_hbm, dest_ref, dest_next_ref, ybuf, sem):
    _fetch_expert_rows(y_hbm, dest_ref, dest_next_ref, ybuf, sem)
    return _combine(x_ref, rec_ref, ybuf, lax.rem(pl.program_id(0), 2))


def _combine_specs(tm, n_tiles):
    row = lambda i: (jnp.minimum(i, n_tiles - 1), 0)
    in_specs = [
        pl.BlockSpec((tm, D_MODEL), row),
        pl.BlockSpec((tm, ROUTER_COLS), row),
        pl.BlockSpec(memory_space=pl.ANY),
        pl.BlockSpec((1, TOP_K, tm), lambda i: (jnp.minimum(i, n_tiles - 1), 0, 0), memory_space=pltpu.SMEM),
        pl.BlockSpec((1, TOP_K, tm), lambda i: (jnp.minimum(i + 1, n_tiles - 1), 0, 0), memory_space=pltpu.SMEM),
    ]
    scratch = [pltpu.VMEM((2, TOP_K, tm * TOK_TILE_ROWS, LANES), jnp.uint32), pltpu.SemaphoreType.DMA((2,))]
    return in_specs, scratch


def _mixer_kernel(combine, tiles_per_seq, *refs):
    if combine:
        x_ref, rec_ref, y_hbm, dest_ref, dest_next_ref = refs[:5]
        refs = refs[5:]
    else:
        x_ref = refs[0]
        refs = refs[1:]
    (g_ref, win_ref, poolw_ref, pscale_ref, sg_ref, sb_ref, sw_ref, sbias_ref,
     dww_ref, dwb_ref, cg_ref, cb_ref, pw_ref) = refs[:13]
    refs = refs[13:]
    if combine:
        ycat_ref, xo_ref = refs[:2]
        h_s, xa_s, z_s, cv_s, ybuf, ysem = refs[2:]
    else:
        ycat_ref = refs[0]
        h_s, xa_s, z_s, cv_s = refs[1:]
    tm = ycat_ref.shape[0]

    tile = pl.program_id(0) % tiles_per_seq
    first = tile == 0

    if combine:
        x = _combined_input(x_ref, rec_ref, y_hbm, dest_ref, dest_next_ref, ybuf, ysem)
        xo_ref[...] = x
    else:
        x = x_ref[...]
    n_ct = D_C // LANES
    halo_src = pl.ds(CONV_PITCH * tm, CONV_HALO, stride=CONV_PITCH)
    halo_dst = pl.ds(0, CONV_HALO, stride=CONV_PITCH)

    n_pg = len(POOL_WINDOWS)
    pool_src = pl.ds(CONV_PITCH * tm, POOL_HALO, stride=CONV_PITCH)
    pool_dst = pl.ds(0, POOL_HALO, stride=CONV_PITCH)

    @pl.when(first)
    def _():
        for gi in range(n_pg):
            xa_s[gi, pool_dst, :] = jnp.zeros((POOL_HALO, LANES), _F32)
        for c in range(n_ct):
            z_s[c, halo_dst, :] = jnp.zeros((CONV_HALO, LANES), _F32)

    @pl.when(jnp.logical_not(first))
    def _():
        for gi in range(n_pg):
            xa_s[gi, pool_dst, :] = xa_s[gi, pool_src, :]
        for c in range(n_ct):
            z_s[c, halo_dst, :] = z_s[c, halo_src, :]

    h_s[...] = (x * _rms_scale(x) * g_ref[...]).astype(_BF16)

    pc = jnp.dot(h_s[...], win_ref[:, D_A + 2 * D_B:N_IN], preferred_element_type=_F32)
    zc = pc[:, :D_C] * _sigmoid(pc[:, D_C:])
    for c in range(n_ct):
        z_s[c, pl.ds(CONV_PITCH * CONV_HALO, tm, stride=CONV_PITCH), :] = zc[:, c * LANES:(c + 1) * LANES]
    for r0 in range(0, tm, CONV_ROWS):
        for c in range(n_ct):
            lanes = slice(c * LANES, (c + 1) * LANES)
            acc = jnp.zeros((CONV_ROWS, LANES), _F32)
            for k in range(CONV_WIDTH):
                off = r0 + CONV_HALO - (CONV_WIDTH - 1) + k
                acc = acc + dww_ref[k:k + 1, lanes] * z_s[c, pl.ds(CONV_PITCH * off, CONV_ROWS, stride=CONV_PITCH), :]
            cv_s[r0:r0 + CONV_ROWS, lanes] = acc + dwb_ref[:, lanes]

    pa = jnp.dot(h_s[...], win_ref[:, 0:D_A], preferred_element_type=_F32)
    for gi in range(n_pg):
        xa_s[gi, pl.ds(CONV_PITCH * POOL_HALO, tm, stride=CONV_PITCH), :] = pa[:, gi * LANES:(gi + 1) * LANES]
    pos = tile * tm + lax.broadcasted_iota(jnp.int32, (tm, 1), 0)
    for gi, w in enumerate(POOL_WINDOWS):
        cols = slice(gi * POOL_GROUP_DIM, (gi + 1) * POOL_GROUP_DIM)
        xg = pa[:, cols]
        acc = xg
        for k in range(1, w):
            acc = acc + xa_s[gi, pl.ds(CONV_PITCH * (POOL_HALO - k), tm, stride=CONV_PITCH), :]
        cnt = jnp.minimum(pos + 1, w).astype(_F32)
        d = acc / cnt - xg
        yg = jnp.dot(d.astype(_BF16), poolw_ref[gi], preferred_element_type=_F32) * pscale_ref[:, cols]
        ycat_ref[:, cols] = yg.astype(_BF16)

    pb = jnp.dot(h_s[...], win_ref[:, D_A:D_A + 2 * D_B], preferred_element_type=_F32)
    zb = _gelu_tanh(pb)
    u = zb[:, :D_B]
    vb = _layer_norm(zb[:, D_B:], sg_ref[...], sb_ref[...]).astype(_BF16)
    r_i = lax.broadcasted_iota(jnp.int32, (CHUNK, CHUNK), 0)
    c_i = lax.broadcasted_iota(jnp.int32, (CHUNK, CHUNK), 1)
    causal = r_i >= c_i
    for hd in range(SGU_HEADS):
        hc = slice(hd * SGU_HEAD_DIM, (hd + 1) * SGU_HEAD_DIM)
        wm = jnp.where(causal, sw_ref[hd], 0.0).astype(_BF16)
        for c in range(tm // CHUNK):
            rows = slice(c * CHUNK, (c + 1) * CHUNK)
            gate = jnp.dot(wm, vb[rows, hc], preferred_element_type=_F32) + sbias_ref[:, hc]
            ycat_ref[rows, D_A + hd * SGU_HEAD_DIM:D_A + (hd + 1) * SGU_HEAD_DIM] = (u[rows, hc] * gate).astype(_BF16)

    zn = _layer_norm(cv_s[...], cg_ref[...], cb_ref[...])
    zs = (zn * _sigmoid(zn)).astype(_BF16)
    ycat_ref[:, D_A + D_B:D_MODEL] = jnp.dot(zs, pw_ref[...], preferred_element_type=_F32).astype(_BF16)


def _mixer_call(t, seq, x, moe, g, win, poolw, pscale, sg, sb, sw, sbias, dww, dwb, cg, cb, pw):
    combine = moe is not None
    tm = TM_MIX if combine else 2 * TM_MIX
    n_tiles = t // tm
    row = lambda i: (i, 0)
    scratch = [
        pltpu.VMEM((tm, D_MODEL), _BF16),
        pltpu.VMEM((D_A // LANES, CONV_PITCH * (POOL_HALO + tm), LANES), _F32),
        pltpu.VMEM((D_C // LANES, CONV_PITCH * (CONV_HALO + tm), LANES), _F32),
        pltpu.VMEM((tm, D_C), _F32),
    ]
    if combine:
        rec, ys, dest = moe
        in_specs, extra = _combine_specs(tm, n_tiles)
        scratch += extra
        args = [x, rec, ys, dest, dest]
    else:
        in_specs = [pl.BlockSpec((tm, D_MODEL), row)]
        args = [x]
    weights = [g, win, poolw, pscale, sg, sb, sw, sbias, dww, dwb, cg, cb, pw]
    in_specs += [_resident(w.shape) for w in weights]
    args += weights
    out_shape = [jax.ShapeDtypeStruct((t, D_MODEL), _BF16)]
    out_specs = [pl.BlockSpec((tm, D_MODEL), row)]
    if combine:
        out_shape.append(jax.ShapeDtypeStruct((t, D_MODEL), _F32))
        out_specs.append(pl.BlockSpec((tm, D_MODEL), row))
    return pl.pallas_call(
        functools.partial(_mixer_kernel, combine, seq // tm),
        grid=(n_tiles,),
        in_specs=in_specs,
        out_specs=out_specs,
        out_shape=out_shape,
        scratch_shapes=scratch,
        compiler_params=pltpu.CompilerParams(dimension_semantics=("arbitrary",), vmem_limit_bytes=VMEM_LIMIT),
        name="mixer_combine" if combine else "mixer",
    )(*args)


def _route_rows(logits, seen):
    rows = logits.shape[0]
    lane = lax.broadcasted_iota(jnp.int32, logits.shape, 1)
    neg = jnp.float32(-jnp.inf)
    big = jnp.int32(ROUTER_COLS)

    def first_max(v):
        m = jnp.max(v, axis=-1, keepdims=True)
        return m, jnp.min(jnp.where(v == m, lane, big), axis=-1, keepdims=True)

    lg = jnp.where(lane < N_GROUPS, logits, neg)
    mg, g_idx = first_max(lg)
    g_w = 1.0 / jnp.sum(jnp.exp(lg - mg), axis=-1, keepdims=True)
    e_lo = N_GROUPS + g_idx * E_PER_GROUP
    le = jnp.where((lane >= e_lo) & (lane < e_lo + E_PER_GROUP), logits, neg)
    m1, i1 = first_max(le)
    m2, i2 = first_max(jnp.where(lane == i1, neg, le))
    p2 = jnp.exp(m2 - m1)
    w1 = g_w / (1.0 + p2)
    w2 = g_w * p2 / (1.0 + p2)
    e1 = i1 - N_GROUPS
    e2 = i2 - N_GROUPS

    hit1 = lane == e1
    hit2 = lane == e2
    onehot = jnp.where(hit1, 1.0, 0.0) + jnp.where(hit2, 1.0, 0.0)
    r_i = lax.broadcasted_iota(jnp.int32, (rows, rows), 0)
    c_i = lax.broadcasted_iota(jnp.int32, (rows, rows), 1)
    earlier = jnp.where(r_i > c_i, 1.0, 0.0).astype(_BF16)
    before = seen + jnp.dot(earlier, onehot.astype(_BF16), preferred_element_type=_F32)
    rank1 = jnp.sum(jnp.where(hit1, before, 0.0), axis=-1, keepdims=True)
    rank2 = jnp.sum(jnp.where(hit2, before, 0.0), axis=-1, keepdims=True)
    seen = seen + jnp.sum(onehot, axis=0, keepdims=True)

    rec = jnp.zeros(logits.shape, _F32)
    for k, v in ((REC_EXPERT, e1.astype(_F32)), (REC_EXPERT + 1, e2.astype(_F32)), (REC_WEIGHT, w1),
                 (REC_WEIGHT + 1, w2), (REC_RANK, rank1), (REC_RANK + 1, rank2)):
        rec = jnp.where(lane == k, v, rec)
    return rec, seen


def _out_kernel(ycat_ref, x_ref, wout_ref, g_ref, wr_ref, br_ref, x1_ref, hp_ref, rec_ref, cnt_ref, lg_s, seen_s):
    step = pl.program_id(0)
    slot = lax.rem(step, 2)

    @pl.when(step == 0)
    def _():
        lg_s[...] = jnp.zeros(lg_s.shape, _F32)
        seen_s[...] = jnp.zeros(seen_s.shape, _F32)

    prev_logits = lg_s[1 - slot]
    prev_seen = seen_s[...]

    tm = x_ref.shape[0]
    part = tm // PROJ_PARTS
    for q in range(PROJ_PARTS):
        rows = pl.ds(q * part, part)
        x1 = x_ref[rows, :] + jnp.dot(ycat_ref[rows, :], wout_ref[...], preferred_element_type=_F32)
        x1_ref[rows, :] = x1
        hb = (x1 * _rms_scale(x1) * g_ref[...]).astype(_BF16)
        lg_s[slot, rows, :] = jnp.dot(hb, wr_ref[...], preferred_element_type=_F32) + br_ref[...]
        _store_token_tiles(hp_ref.at[pl.ds(q * part * TOK_TILE_ROWS, part * TOK_TILE_ROWS)], _pack_bf16_pairs(hb))

    rec, seen = _route_rows(prev_logits, prev_seen)
    rec_ref[...] = rec
    seen = jnp.where(step == 0, 0.0, seen)
    seen_s[...] = seen
    cnt_ref[...] = jnp.broadcast_to(seen, cnt_ref.shape)


def _out_call(t, ycat, x, wout, g, wr, br):
    tm = TM_PROJ
    n_tiles = t // tm
    row = lambda i: (jnp.minimum(i, n_tiles - 1), 0)
    prev = lambda i: (jnp.maximum(i - 1, 0), 0)
    return pl.pallas_call(
        _out_kernel,
        grid=(n_tiles + 1,),
        in_specs=[pl.BlockSpec((tm, D_MODEL), row), pl.BlockSpec((tm, D_MODEL), row),
                  _resident(wout.shape), _resident(g.shape), _resident(wr.shape), _resident(br.shape)],
        out_specs=[pl.BlockSpec((tm, D_MODEL), row), pl.BlockSpec((tm * TOK_TILE_ROWS, LANES), row),
                   pl.BlockSpec((tm, ROUTER_COLS), prev), pl.BlockSpec((SUBLANES, ROUTER_COLS), lambda i: (0, 0))],
        out_shape=[jax.ShapeDtypeStruct((t, D_MODEL), _F32),
                   jax.ShapeDtypeStruct((t * TOK_TILE_ROWS, LANES), jnp.uint32),
                   jax.ShapeDtypeStruct((t, ROUTER_COLS), _F32),
                   jax.ShapeDtypeStruct((SUBLANES, ROUTER_COLS), _F32)],
        scratch_shapes=[pltpu.VMEM((2, tm, ROUTER_COLS), _F32), pltpu.VMEM((1, ROUTER_COLS), _F32)],
        compiler_params=pltpu.CompilerParams(dimension_semantics=("arbitrary",), vmem_limit_bytes=VMEM_LIMIT),
        name="out_proj",
    )(ycat, x, wout, g, wr, br)


def _dest_kernel(rec_ref, seg_ref, out_ref):
    rec = rec_ref[...]
    lane = lax.broadcasted_iota(jnp.int32, rec.shape, 1)
    lane_f = lane.astype(_F32)
    rows = jnp.zeros(rec.shape, _F32)
    for k in range(TOP_K):
        expert = rec[:, REC_EXPERT + k:REC_EXPERT + k + 1]
        start = jnp.sum(jnp.where(lane_f == expert, seg_ref[...], 0.0), axis=-1, keepdims=True)
        rows = jnp.where(lane == k, start + rec[:, REC_RANK + k:REC_RANK + k + 1], rows)
    out_ref[...] = jnp.transpose(rows)[0:SUBLANES, :].astype(jnp.int32)


def _dest_call(t, rec, seg_start):
    td = TM_DEST
    by_k = pl.pallas_call(
        _dest_kernel,
        grid=(t // td,),
        in_specs=[pl.BlockSpec((td, ROUTER_COLS), lambda i: (i, 0)), _resident(seg_start.shape)],
        out_specs=pl.BlockSpec((SUBLANES, td), lambda i: (0, i)),
        out_shape=jax.ShapeDtypeStruct((SUBLANES, t), jnp.int32),
        compiler_params=pltpu.CompilerParams(dimension_semantics=("arbitrary",), vmem_limit_bytes=VMEM_LIMIT),
        name="dest_rows",
    )(rec, seg_start)
    return by_k[:TOP_K].reshape(TOP_K, t // TM_OUT, TM_OUT).transpose(1, 0, 2)


def _dispatch_kernel(pad_start_ref, pad_rows_ref, dest_ref, hp_hbm, xs_hbm, buf, zero_s, lsem, sem, zsem):
    step = pl.program_id(0)
    last = pl.num_programs(0) - 1
    tm = dest_ref.shape[2]
    slot = lax.rem(step, 2)

    def stage(tile, to_slot):
        return pltpu.make_async_copy(hp_hbm.at[pl.ds(tile * tm, tm)], buf.at[to_slot], lsem.at[to_slot])

    def wait_tile():
        view = xs_hbm.at[pl.ds(0, TOP_K * tm)]
        pltpu.make_async_copy(view, view, sem).wait()

    @pl.when(step == 0)
    def _():
        stage(0, 0).start()
        zero_s[...] = jnp.zeros(zero_s.shape, zero_s.dtype)
        for e in range(pad_rows_ref.shape[0]):
            n = pad_rows_ref[e]

            @pl.when(n > 0)
            def _():
                pltpu.make_async_copy(zero_s.at[pl.ds(0, n)], xs_hbm.at[pl.ds(pad_start_ref[e], n)], zsem).start()

        for e in range(pad_rows_ref.shape[0]):
            n = pad_rows_ref[e]

            @pl.when(n > 0)
            def _():
                pltpu.make_async_copy(zero_s.at[pl.ds(0, n)], xs_hbm.at[pl.ds(pad_start_ref[e], n)], zsem).wait()

    stage(step, slot).wait()

    @pl.when(step > 0)
    def _():
        wait_tile()

    @pl.when(step < last)
    def _():
        stage(step + 1, 1 - slot).start()

    for j in range(tm):
        for k in range(TOP_K):
            pltpu.make_async_copy(buf.at[slot, j], xs_hbm.at[dest_ref[0, k, j]], sem).start(priority=k % 2)

    @pl.when(step == last)
    def _():
        wait_tile()


def _dispatch_call(t, p_len, pad_start, pad_rows, dest, hp):
    tm = TM_OUT
    grid_spec = pltpu.PrefetchScalarGridSpec(
        num_scalar_prefetch=2,
        grid=(t // tm,),
        in_specs=[pl.BlockSpec((1, TOP_K, tm), lambda i, *_: (i, 0, 0), memory_space=pltpu.SMEM),
                  pl.BlockSpec(memory_space=pl.ANY)],
        out_specs=pl.BlockSpec(memory_space=pl.ANY),
        scratch_shapes=[pltpu.VMEM((2, tm, TOK_TILE_ROWS, LANES), jnp.uint32),
                        pltpu.VMEM((BM, TOK_TILE_ROWS, LANES), jnp.uint32),
                        pltpu.SemaphoreType.DMA((2,)), pltpu.SemaphoreType.DMA(()), pltpu.SemaphoreType.DMA(())],
    )
    return pl.pallas_call(
        _dispatch_kernel,
        grid_spec=grid_spec,
        out_shape=jax.ShapeDtypeStruct((p_len, TOK_TILE_ROWS, LANES), jnp.uint32),
        compiler_params=pltpu.CompilerParams(dimension_semantics=("arbitrary",), vmem_limit_bytes=VMEM_LIMIT),
        name="dispatch",
    )(pad_start, pad_rows, dest, hp.reshape(t, TOK_TILE_ROWS, LANES))


def _expert_kernel(layer, be_ref, cnt_ref, nv_ref, first_ref, par_ref, nxt_ref, xs_ref, w1_hbm, w3_hbm, w2_hbm,
                   y_ref, w1_s, w3_s, w2_s, wsem):
    del nv_ref
    half = D_MODEL // 2
    step = pl.program_id(0)
    rows = cnt_ref[step]
    slot = par_ref[step]

    def weight_copies(e, to_slot):
        return (pltpu.make_async_copy(w1_hbm.at[layer, e], w1_s.at[to_slot], wsem.at[to_slot]),
                pltpu.make_async_copy(w3_hbm.at[layer, e], w3_s.at[to_slot], wsem.at[to_slot]),
                pltpu.make_async_copy(w2_hbm.at[layer, e], w2_s.at[to_slot], wsem.at[to_slot]))

    @pl.when(step == 0)
    def _():
        for cp in weight_copies(be_ref[0], slot):
            cp.start()

    @pl.when(first_ref[step] == 1)
    def _():
        for cp in weight_copies(be_ref[step], slot):
            cp.wait()

        @pl.when(nxt_ref[step] >= 0)
        def _():
            for cp in weight_copies(nxt_ref[step], 1 - slot):
                cp.start()

    @pl.when(rows == 0)
    def _():
        y_ref[...] = jnp.zeros(y_ref.shape, y_ref.dtype)

    @pl.when(rows > 0)
    def _():
        lo, hi = _unpack_bf16_pairs(_load_token_tiles(xs_ref, BM))
        a1 = (jnp.dot(lo, w1_s[slot, 0:half, :].astype(_BF16), preferred_element_type=_F32)
              + jnp.dot(hi, w1_s[slot, half:D_MODEL, :].astype(_BF16), preferred_element_type=_F32))
        a3 = (jnp.dot(lo, w3_s[slot, 0:half, :].astype(_BF16), preferred_element_type=_F32)
              + jnp.dot(hi, w3_s[slot, half:D_MODEL, :].astype(_BF16), preferred_element_type=_F32))
        hid = (a1 * _sigmoid(a1) * a3).astype(_BF16)
        y = jnp.dot(hid, w2_s[slot].astype(_BF16), preferred_element_type=_F32)
        _store_token_tiles(y_ref, _pack_bf16_pairs(y.astype(_BF16)))


def _expert_call(layer, n_blocks, plan, xs, w1, w3, w2):
    blk = lambda i, be, cnt, nv, *_: (jnp.minimum(i, nv[0] - 1), 0)
    grid_spec = pltpu.PrefetchScalarGridSpec(
        num_scalar_prefetch=len(plan),
        grid=(n_blocks,),
        in_specs=[
            pl.BlockSpec((BM * TOK_TILE_ROWS, LANES), blk),
            pl.BlockSpec(memory_space=pl.ANY),
            pl.BlockSpec(memory_space=pl.ANY),
            pl.BlockSpec(memory_space=pl.ANY),
        ],
        out_specs=pl.BlockSpec((BM * TOK_TILE_ROWS, LANES), lambda i, *_: (i, 0)),
        scratch_shapes=[
            pltpu.VMEM((2, D_MODEL, D_EXPERT), w1.dtype),
            pltpu.VMEM((2, D_MODEL, D_EXPERT), w3.dtype),
            pltpu.VMEM((2, D_EXPERT, D_MODEL), w2.dtype),
            pltpu.SemaphoreType.DMA((2,)),
        ],
    )
    ys = pl.pallas_call(
        functools.partial(_expert_kernel, layer),
        grid_spec=grid_spec,
        out_shape=jax.ShapeDtypeStruct((n_blocks * BM * TOK_TILE_ROWS, LANES), jnp.uint32),
        compiler_params=pltpu.CompilerParams(dimension_semantics=("arbitrary",), vmem_limit_bytes=VMEM_LIMIT),
        name="experts",
    )(*plan, xs.reshape(n_blocks * BM * TOK_TILE_ROWS, LANES), w1, w3, w2)
    return ys.reshape(n_blocks * BM, TOK_TILE_ROWS, LANES)


def _slot_tables(t, counts_f):
    n_assign = t * TOP_K
    n_blocks = n_assign // BM + N_EXPERTS
    counts = counts_f[0, :N_EXPERTS].astype(jnp.int32)
    padded = ((counts + BM - 1) // BM) * BM
    pend = jnp.cumsum(padded)
    pstart = pend - padded
    starts = jnp.arange(n_blocks, dtype=jnp.int32) * BM
    block_e = jnp.minimum(jnp.sum(pend[None, :] <= starts[:, None], axis=1), N_EXPERTS - 1).astype(jnp.int32)
    seg_end = (pstart + counts)[block_e]
    block_cnt = jnp.where(starts < pend[-1], jnp.clip(seg_end - starts, 0, BM), 0).astype(jnp.int32)
    n_valid = jnp.maximum(pend[-1] // BM, 1).astype(jnp.int32).reshape(1)
    real = starts < pend[-1]
    block_e = jnp.where(real, block_e, block_e[n_valid[0] - 1])
    first = (real & (block_e != jnp.concatenate([jnp.full((1,), -1, jnp.int32), block_e[:-1]]))).astype(jnp.int32)
    parity = jnp.maximum(jnp.cumsum(first) - 1, 0) % 2
    e_ids = jnp.arange(N_EXPERTS, dtype=jnp.int32)
    later = jnp.where((counts[None, :] > 0) & (e_ids[None, :] > e_ids[:, None]), e_ids[None, :], N_EXPERTS)
    next_e = jnp.min(later, axis=1)
    next_e = jnp.where(next_e < N_EXPERTS, next_e, -1)[block_e]
    plan = (block_e, block_cnt, n_valid, first, parity.astype(jnp.int32), next_e.astype(jnp.int32))
    seg_start = jnp.zeros((1, ROUTER_COLS), _F32).at[0, :N_EXPERTS].set(pstart.astype(_F32))
    tail = pend[-1] + jnp.arange(N_EXPERTS, dtype=jnp.int32) * BM
    pad_start = jnp.concatenate([pstart + counts, tail])
    pad_rows = jnp.concatenate([padded - counts, jnp.where(tail < n_blocks * BM, BM, 0)])
    return n_blocks, plan, seg_start, pad_start, pad_rows.astype(jnp.int32)


def _final_kernel(x_ref, rec_ref, y_hbm, dest_ref, dest_next_ref, g_ref, o_ref, ybuf, ysem):
    x = _combined_input(x_ref, rec_ref, y_hbm, dest_ref, dest_next_ref, ybuf, ysem)
    o_ref[...] = x * _rms_scale(x) * g_ref[...]


def _final_call(t, x, moe, g):
    tm = TM_OUT
    n_tiles = t // tm
    rec, ys, dest = moe
    in_specs, scratch = _combine_specs(tm, n_tiles)
    return pl.pallas_call(
        _final_kernel,
        grid=(n_tiles,),
        in_specs=in_specs + [_resident(g.shape)],
        out_specs=pl.BlockSpec((tm, D_MODEL), lambda i: (i, 0)),
        out_shape=jax.ShapeDtypeStruct((t, D_MODEL), _F32),
        scratch_shapes=scratch,
        compiler_params=pltpu.CompilerParams(dimension_semantics=("arbitrary",), vmem_limit_bytes=VMEM_LIMIT),
        name="final_norm",
    )(x, rec, ys, dest, dest, g)


def kernel(x, norm_mix, w_in, pool_w, pool_scale, sgu_ln_g, sgu_ln_b, sgu_w, sgu_b, conv_dw_w, conv_dw_b,
           conv_ln_g, conv_ln_b, conv_pw, w_out, norm_ffn, router_g_w, router_g_b, router_e_w, router_e_b,
           exp_w1, exp_w3, exp_w2, norm_final):
    b, s, d = x.shape
    depth = w_in.shape[0]
    assert d == D_MODEL and s % TM_MIX == 0 and TM_MIX % CHUNK == 0 and TM_MIX == TM_OUT and (b * s) % TM_PROJ == 0
    t = b * s
    xf = x.reshape(t, d)
    moe = None
    for l in range(depth):
        row2 = lambda v: v[l].reshape(1, -1)
        sbias = jnp.repeat(sgu_b[l].T, SGU_HEAD_DIM, axis=1)
        outs = _mixer_call(
            t, s, xf, moe, row2(norm_mix), w_in[l].astype(_BF16), pool_w[l].astype(_BF16),
            row2(pool_scale), row2(sgu_ln_g), row2(sgu_ln_b), sgu_w[l], sbias, conv_dw_w[l], row2(conv_dw_b),
            row2(conv_ln_g), row2(conv_ln_b), conv_pw[l].astype(_BF16))
        if moe is not None:
            ycat, xf = outs
        else:
            (ycat,) = outs
        pad = ROUTER_COLS - N_GROUPS - N_EXPERTS
        wr = jnp.concatenate([router_g_w[l], router_e_w[l], jnp.zeros((d, pad), _F32)], axis=1).astype(_BF16)
        br = jnp.concatenate([router_g_b[l], router_e_b[l], jnp.zeros((pad,), _F32)]).reshape(1, ROUTER_COLS)
        xf, hp, rec, counts = _out_call(t, ycat, xf, w_out[l].astype(_BF16), row2(norm_ffn), wr, br)
        n_blocks, plan, seg_start, pad_start, pad_rows = _slot_tables(t, counts)
        dest = _dest_call(t, rec, seg_start)
        xs = _dispatch_call(t, n_blocks * BM, pad_start, pad_rows, dest, hp)
        ys = _expert_call(l, n_blocks, plan, xs, exp_w1, exp_w3, exp_w2)
        moe = (rec, ys, dest)
    out = _final_call(t, xf, moe, norm_final.reshape(1, -1))
    return out.reshape(b, s, d)
```
